```python
import jax, jax.numpy as jnp
from jax import lax
import numpy as np

D_MODEL = 2048
BATCH = 2
SEQ = 8192
DEPTH = 1
DEC_BATCH = 8
DEC_SEQ = 4096
PAST_LEN = 128

MIX_WIDTH = D_MODEL
HEAD_DIM = 64
ATT_WIDTH = MIX_WIDTH // 2
ATT_HEADS = ATT_WIDTH // HEAD_DIM
RWKV_WIDTH = MIX_WIDTH - ATT_WIDTH
RWKV_HEAD_DIM = 64
RWKV_HEADS = RWKV_WIDTH // RWKV_HEAD_DIM
DECAY_RANK = 64
ICLR_RANK = 64
GATE_RANK = 128
D_FF = 5632
WINDOWS = (128, 512, 2048)
DILATIONS = (1, 4, 16)
NORM_EPS = 1e-6
RWKV_LN_EPS = 64e-5
NEG_INF = -1e30
N_ATT_COLS = 3 * ATT_WIDTH
RWKV_COL_SIZES = (RWKV_WIDTH, RWKV_WIDTH, RWKV_WIDTH, DECAY_RANK, DECAY_RANK, ICLR_RANK, ICLR_RANK, GATE_RANK)
N_RWKV_COLS = sum(RWKV_COL_SIZES)
N_IN_COLS = N_ATT_COLS + N_RWKV_COLS

kernel_name = 'hymba_longnet_rwkv7_macaron_encoder'


def _rmsnorm(x, g):
    xf = x.astype(jnp.float32)
    y = xf * lax.rsqrt(jnp.mean(xf * xf, axis=-1, keepdims=True) + NORM_EPS) * g.astype(jnp.float32)
    return y.astype(x.dtype)


def _swiglu(x, w_gate, w_up, w_down):
    return (jax.nn.silu(x @ w_gate) * (x @ w_up)) @ w_down


def _to_residue(x, d):
    B, T = x.shape[0], x.shape[1]
    rest = x.shape[2:]
    return x.reshape((B, T // d, d) + rest).swapaxes(1, 2).reshape((B * d, T // d) + rest)


def _from_residue(x, B, d):
    L = x.shape[1]
    rest = x.shape[2:]
    return x.reshape((B, d, L) + rest).swapaxes(1, 2).reshape((B, d * L) + rest)


def _band_attention(q, k, v, slopes, dil, half):
    n, L, H, E = q.shape
    W = half
    nb = -(-L // W)
    Lp = nb * W
    pad = Lp - L
    qb = jnp.pad(q * (E ** -0.5), ((0, 0), (0, pad), (0, 0), (0, 0))).reshape(n, nb, W, H, E)
    kp = jnp.pad(k, ((0, 0), (W, pad + W), (0, 0), (0, 0))).reshape(n, nb + 2, W, H, E)
    vp = jnp.pad(v, ((0, 0), (W, pad + W), (0, 0), (0, 0))).reshape(n, nb + 2, W, H, E)
    kw = jnp.concatenate([kp[:, :-2], kp[:, 1:-1], kp[:, 2:]], axis=2)
    vw = jnp.concatenate([vp[:, :-2], vp[:, 1:-1], vp[:, 2:]], axis=2)
    qpos = jnp.arange(nb)[:, None] * W + jnp.arange(W)[None, :]
    kpos = jnp.arange(nb)[:, None] * W - W + jnp.arange(3 * W)[None, :]
    rel = kpos[:, None, :] - qpos[:, :, None]
    valid = (jnp.abs(rel) <= W) & (kpos[:, None, :] >= 0) & (kpos[:, None, :] < L)
    dist = (jnp.abs(rel) * dil).astype(jnp.float32)
    logits = jnp.einsum('nbqhe,nbkhe->nbhqk', qb, kw).astype(jnp.float32)
    logits = logits - slopes[:, None, None] * dist[:, None]
    logits = jnp.where(valid[:, None], logits, NEG_INF)
    m = jnp.max(logits, axis=-1, keepdims=True)
    p = jnp.exp(logits - m)
    l = jnp.sum(p, axis=-1, keepdims=True)
    o = jnp.einsum('nbhqk,nbkhe->nbqhe', p, vw) / jnp.moveaxis(l, 2, 3)
    lse = jnp.moveaxis((m + jnp.log(l))[..., 0], 2, 3)
    o = o.reshape(n, Lp, H, E)[:, :L]
    lse = lse.reshape(n, Lp, H)[:, :L]
    return o, lse


def _dilated_alibi_attention(q, k, v):
    B, T, H, E = q.shape
    slopes = jnp.exp2(-8.0 * jnp.arange(1, H + 1, dtype=jnp.float32) / H)
    outs, lses = [], []
    for window, dil in zip(WINDOWS, DILATIONS):
        half = window // (2 * dil)
        o, lse = _band_attention(_to_residue(q, dil), _to_residue(k, dil), _to_residue(v, dil), slopes, dil, half)
        outs.append(_from_residue(o, B, dil))
        lses.append(_from_residue(lse, B, dil))
    wts = jax.nn.softmax(jnp.stack(lses), axis=0)
    return jnp.sum(wts[..., None] * jnp.stack(outs), axis=0)


def _centred_shift(z, mu_prev, mu_next):
    z_prev = jnp.pad(z, ((0, 0), (1, 0), (0, 0)))[:, :-1]
    z_next = jnp.pad(z, ((0, 0), (0, 1), (0, 0)))[:, 1:]
    return z + mu_prev * (z_prev - z) + mu_next * (z_next - z)


def _wkv7_scan(r, w, k, v, a, b, reverse):
    B, T, H, E = r.shape

    def step(S, inp):
        r_t, w_t, k_t, v_t, a_t, b_t = inp
        sa = jnp.einsum('bhij,bhj->bhi', S, a_t)
        S = S * w_t[:, :, None, :] + sa[..., None] * b_t[:, :, None, :] + v_t[..., None] * k_t[:, :, None, :]
        y = jnp.einsum('bhij,bhj->bhi', S, r_t)
        return S, y

    xs = tuple(jnp.moveaxis(t, 1, 0) for t in (r, w, k, v, a, b))
    S0 = jnp.zeros((B, H, E, E), jnp.float32)
    _, ys = lax.scan(step, S0, xs, reverse=reverse)
    return jnp.moveaxis(ys, 0, 1)


def _rwkv7_bidir(z, mu_prev, mu_next, w0_f, w2_f, w0_b, w2_b, a0_f, a2_f, a0_b, a2_b, g2, k_k, k_a, r_k, ln_x_w, ln_x_b):
    B, T, _ = z.shape
    z = _centred_shift(z.astype(jnp.float32), mu_prev, mu_next)
    splits = list(np.cumsum(RWKV_COL_SIZES)[:-1])
    r, k, v, wdf, wdb, adf, adb, gd = jnp.split(z, splits, axis=-1)

    def decay(wd, w0, w2):
        w = -jax.nn.softplus(-(w0 + jnp.tanh(wd) @ w2)) - 0.5
        return jnp.exp(-jnp.exp(w))

    def heads(t):
        return t.reshape(B, T, RWKV_HEADS, RWKV_HEAD_DIM)

    dec_f = decay(wdf, w0_f, w2_f)
    dec_b = decay(wdb, w0_b, w2_b)
    a_f = jax.nn.sigmoid(a0_f + adf @ a2_f)
    a_b = jax.nn.sigmoid(a0_b + adb @ a2_b)
    g = jax.nn.sigmoid(gd) @ g2
    kk = heads(k * k_k)
    kk = kk / jnp.maximum(jnp.sqrt(jnp.sum(kk * kk, axis=-1, keepdims=True)), 1e-12)
    k_f = heads(k * (1.0 + (a_f - 1.0) * k_a))
    k_b = heads(k * (1.0 + (a_b - 1.0) * k_a))
    rh, vh = heads(r), heads(v)
    y_f = _wkv7_scan(rh, heads(dec_f), k_f, vh, -kk, kk * heads(a_f), reverse=False)
    y_b = _wkv7_scan(rh, heads(dec_b), k_b, vh, -kk, kk * heads(a_b), reverse=True)
    y = y_f + y_b
    mu = jnp.mean(y, axis=-1, keepdims=True)
    var = jnp.mean(jnp.square(y - mu), axis=-1, keepdims=True)
    y = ((y - mu) * lax.rsqrt(var + RWKV_LN_EPS)).reshape(B, T, RWKV_WIDTH) * ln_x_w + ln_x_b
    bonus = (jnp.sum(rh * k_f * r_k, axis=-1, keepdims=True) + jnp.sum(rh * k_b * r_k, axis=-1, keepdims=True)) * vh
    return (y + bonus.reshape(B, T, RWKV_WIDTH)) * g


def _layer(x, ffn1_norm, ffn1_gate, ffn1_up, ffn1_down, mix_norm, w_in, w_out, mu_prev, mu_next,
           w0_f, w2_f, w0_b, w2_b, a0_f, a2_f, a0_b, a2_b, g2, k_k, k_a, r_k, ln_x_w, ln_x_b,
           ffn2_norm, ffn2_gate, ffn2_up, ffn2_down):
    B, T, _ = x.shape
    x = x + 0.5 * _swiglu(_rmsnorm(x, ffn1_norm), ffn1_gate, ffn1_up, ffn1_down)
    proj = _rmsnorm(x, mix_norm) @ w_in
    q, k, v = jnp.split(proj[..., :N_ATT_COLS], 3, axis=-1)
    q, k, v = (t.reshape(B, T, ATT_HEADS, HEAD_DIM).astype(jnp.float32) for t in (q, k, v))
    att = _dilated_alibi_attention(q, k, v).reshape(B, T, ATT_WIDTH)
    rw = _rwkv7_bidir(proj[..., N_ATT_COLS:], mu_prev, mu_next, w0_f, w2_f, w0_b, w2_b,
                      a0_f, a2_f, a0_b, a2_b, g2, k_k, k_a, r_k, ln_x_w, ln_x_b)
    mixed = jnp.concatenate([att, rw], axis=-1).astype(x.dtype)
    x = x + mixed @ w_out
    x = x + 0.5 * _swiglu(_rmsnorm(x, ffn2_norm), ffn2_gate, ffn2_up, ffn2_down)
    return x


def setup_inputs(seed: int = 0) -> dict:
    key = jax.random.key(seed)
    ks = iter(jax.random.split(key, 40))
    f32 = jnp.float32

    def nrm(shape, scale):
        return jax.random.normal(next(ks), shape, f32) * scale

    def gain(shape):
        return 1.0 + 0.02 * jax.random.normal(next(ks), shape, f32)

    def unif(shape, lo, hi):
        return jax.random.uniform(next(ks), shape, f32, lo, hi)

    L = DEPTH
    return {
        'x_prompt': nrm((BATCH, SEQ, D_MODEL), 1.0),
        'x_sample': nrm((DEC_BATCH, DEC_SEQ, D_MODEL), 1.0),
        'ffn1_norm': gain((L, D_MODEL)),
        'ffn1_gate': nrm((L, D_MODEL, D_FF), D_MODEL ** -0.5),
        'ffn1_up': nrm((L, D_MODEL, D_FF), D_MODEL ** -0.5),
        'ffn1_down': nrm((L, D_FF, D_MODEL), D_FF ** -0.5),
        'mix_norm': gain((L, D_MODEL)),
        'w_in': nrm((L, D_MODEL, N_IN_COLS), D_MODEL ** -0.5),
        'w_out': nrm((L, MIX_WIDTH, D_MODEL), MIX_WIDTH ** -0.5),
        'mu_prev': unif((L, N_RWKV_COLS), 0.0, 0.5),
        'mu_next': unif((L, N_RWKV_COLS), 0.0, 0.5),
        'w0_f': unif((L, RWKV_WIDTH), -6.0, -1.0),
        'w2_f': nrm((L, DECAY_RANK, RWKV_WIDTH), 0.5 * DECAY_RANK ** -0.5),
        'w0_b': unif((L, RWKV_WIDTH), -6.0, -1.0),
        'w2_b': nrm((L, DECAY_RANK, RWKV_WIDTH), 0.5 * DECAY_RANK ** -0.5),
        'a0_f': nrm((L, RWKV_WIDTH), 0.5),
        'a2_f': nrm((L, ICLR_RANK, RWKV_WIDTH), 0.5 * ICLR_RANK ** -0.5),
        'a0_b': nrm((L, RWKV_WIDTH), 0.5),
        'a2_b': nrm((L, ICLR_RANK, RWKV_WIDTH), 0.5 * ICLR_RANK ** -0.5),
        'g2': nrm((L, GATE_RANK, RWKV_WIDTH), GATE_RANK ** -0.5),
        'k_k': 0.85 + nrm((L, RWKV_WIDTH), 0.05),
        'k_a': 1.0 + nrm((L, RWKV_WIDTH), 0.05),
        'r_k': nrm((L, RWKV_HEADS, RWKV_HEAD_DIM), 0.1),
        'ln_x_w': gain((L, RWKV_WIDTH)),
        'ln_x_b': nrm((L, RWKV_WIDTH), 0.01),
        'ffn2_norm': gain((L, D_MODEL)),
        'ffn2_gate': nrm((L, D_MODEL, D_FF), D_MODEL ** -0.5),
        'ffn2_up': nrm((L, D_MODEL, D_FF), D_MODEL ** -0.5),
        'ffn2_down': nrm((L, D_FF, D_MODEL), D_FF ** -0.5),
        'final_norm': gain((D_MODEL,)),
    }


def reference(x_prompt, x_sample, ffn1_norm, ffn1_gate, ffn1_up, ffn1_down, mix_norm, w_in, w_out,
              mu_prev, mu_next, w0_f, w2_f, w0_b, w2_b, a0_f, a2_f, a0_b, a2_b, g2, k_k, k_a, r_k,
              ln_x_w, ln_x_b, ffn2_norm, ffn2_gate, ffn2_up, ffn2_down, final_norm):
    layer_params = (ffn1_norm, ffn1_gate, ffn1_up, ffn1_down, mix_norm, w_in, w_out, mu_prev, mu_next,
                    w0_f, w2_f, w0_b, w2_b, a0_f, a2_f, a0_b, a2_b, g2, k_k, k_a, r_k, ln_x_w, ln_x_b,
                    ffn2_norm, ffn2_gate, ffn2_up, ffn2_down)

    def trunk(x):
        for layer in range(DEPTH):
            x = _layer(x, *[p[layer] for p in layer_params])
        return _rmsnorm(x, final_norm)

    y_prompt = trunk(x_prompt)
    y_sample = trunk(x_sample)
    return (y_prompt, y_sample)
```

```python
import functools

import jax
import jax.numpy as jnp
from jax import lax
from jax.experimental import pallas as pl
from jax.experimental.pallas import tpu as pltpu

F32 = jnp.float32
BF16 = jnp.bfloat16

D_MODEL = 2048
D_FF = 5632
HEAD_DIM = 64
ATT_WIDTH = 1024
ATT_HEADS = 16
RWKV_WIDTH = 1024
N_ATT_COLS = 3 * ATT_WIDTH
N_RWKV_COLS = 3 * RWKV_WIDTH + 64 * 4 + 128
WD_COL = 3 * RWKV_WIDTH
AD_COL = WD_COL + 128
GD_COL = AD_COL + 128
DILATIONS = (1, 4, 16)
BAND_HALF = 64
NORM_EPS = 1e-6
RWKV_LN_EPS = 64e-5
NEG_INF = -1e30

LANES = 128
CHUNK = 64
VMEM_LIMIT = 56 * 1024 * 1024


def _cparams(sem):
    return pltpu.CompilerParams(dimension_semantics=sem, vmem_limit_bytes=VMEM_LIMIT)


def _dot(a, b):
    return jnp.dot(a.astype(BF16), b.astype(BF16), preferred_element_type=F32)


def _dot_nt(a, b):
    return lax.dot_general(a.astype(BF16), b.astype(BF16), (((1,), (1,)), ((), ())),
                           preferred_element_type=F32)


def _dot_tn(a, b):
    return lax.dot_general(a.astype(BF16), b.astype(BF16), (((0,), (0,)), ((), ())),
                           preferred_element_type=F32)


def _split3(x):
    hi = x.astype(BF16)
    r1 = x - hi.astype(F32)
    mid = r1.astype(BF16)
    lo = (r1 - mid.astype(F32)).astype(BF16)
    return hi, mid, lo


def _dot_exact_rhs(a_f32, b_bf16):
    hi, mid, lo = _split3(a_f32)
    return (jnp.dot(hi, b_bf16, preferred_element_type=F32)
            + jnp.dot(mid, b_bf16, preferred_element_type=F32)
            + jnp.dot(lo, b_bf16, preferred_element_type=F32))


def _dot_exact_lhs(a_bf16, b_f32):
    hi, mid, lo = _split3(b_f32)
    return (jnp.dot(a_bf16, hi, preferred_element_type=F32)
            + jnp.dot(a_bf16, mid, preferred_element_type=F32)
            + jnp.dot(a_bf16, lo, preferred_element_type=F32))


def _rmsnorm(x, g):
    return x * lax.rsqrt(jnp.mean(x * x, axis=-1, keepdims=True) + NORM_EPS) * g


def _ffn_kernel(x_ref, g_ref, wg_ref, wu_ref, wd_ref, fg_ref, o_ref, n_scr, acc_scr, *, final_norm):
    f = pl.program_id(1)

    @pl.when(f == 0)
    def _():
        n_scr[...] = _rmsnorm(x_ref[...], g_ref[...]).astype(BF16)
        acc_scr[...] = jnp.zeros_like(acc_scr)

    n = n_scr[...]
    hg = jnp.dot(n, wg_ref[...], preferred_element_type=F32)
    hu = jnp.dot(n, wu_ref[...], preferred_element_type=F32)
    h = (hg * jax.nn.sigmoid(hg) * hu).astype(BF16)
    acc_scr[...] += jnp.dot(h, wd_ref[...], preferred_element_type=F32)

    @pl.when(f == pl.num_programs(1) - 1)
    def _():
        out = x_ref[...] + 0.5 * acc_scr[...]
        if final_norm:
            out = _rmsnorm(out, fg_ref[...])
        o_ref[...] = out


def _ffn(x, g, wg, wu, wd, fg, *, final_norm, tm=512, tf=512):
    n_tok = x.shape[0]
    return pl.pallas_call(
        functools.partial(_ffn_kernel, final_norm=final_norm),
        out_shape=jax.ShapeDtypeStruct((n_tok, D_MODEL), F32),
        grid=(n_tok // tm, D_FF // tf),
        in_specs=[
            pl.BlockSpec((tm, D_MODEL), lambda i, f: (i, 0)),
            pl.BlockSpec((1, D_MODEL), lambda i, f: (0, 0)),
            pl.BlockSpec((D_MODEL, tf), lambda i, f: (0, f)),
            pl.BlockSpec((D_MODEL, tf), lambda i, f: (0, f)),
            pl.BlockSpec((tf, D_MODEL), lambda i, f: (f, 0)),
            pl.BlockSpec((1, D_MODEL), lambda i, f: (0, 0)),
        ],
        out_specs=pl.BlockSpec((tm, D_MODEL), lambda i, f: (i, 0)),
        scratch_shapes=[pltpu.VMEM((tm, D_MODEL), BF16), pltpu.VMEM((tm, D_MODEL), F32)],
        compiler_params=_cparams(("parallel", "arbitrary")),
        name="ffn_final" if final_norm else "ffn",
    )(x, g, wg, wu, wd, fg)


def _inproj_kernel(x_ref, g_ref, w_ref, o_ref, n_scr):
    @pl.when(pl.program_id(1) == 0)
    def _():
        n_scr[...] = _rmsnorm(x_ref[...], g_ref[...]).astype(BF16)

    o_ref[...] = jnp.dot(n_scr[...], w_ref[...], preferred_element_type=F32).astype(o_ref.dtype)


def _inproj(x, g, w, out_dtype, *, tn, tm=512):
    n_tok = x.shape[0]
    n_cols = w.shape[1]
    return pl.pallas_call(
        _inproj_kernel,
        out_shape=jax.ShapeDtypeStruct((n_tok, n_cols), out_dtype),
        grid=(n_tok // tm, n_cols // tn),
        in_specs=[
            pl.BlockSpec((tm, D_MODEL), lambda i, j: (i, 0)),
            pl.BlockSpec((1, D_MODEL), lambda i, j: (0, 0)),
            pl.BlockSpec((D_MODEL, tn), lambda i, j: (0, j)),
        ],
        out_specs=pl.BlockSpec((tm, tn), lambda i, j: (i, j)),
        scratch_shapes=[pltpu.VMEM((tm, D_MODEL), BF16)],
        compiler_params=_cparams(("parallel", "arbitrary")),
        name="inproj",
    )(x, g, w)


ATT_TQ = 128
ATT_TK = 64


def _attn_kernel(slopes_ref, q_ref, k0, k1, k2, k3, v0, v1, v2, v3, o_ref, lse_ref, kc_scr, vc_scr,
                 *, seq_len, dil):
    i = pl.program_id(2)
    for j, (kr, vr) in enumerate(((k0, v0), (k1, v1), (k2, v2), (k3, v3))):
        kc_scr[j * ATT_TK:(j + 1) * ATT_TK, :] = kr[...]
        vc_scr[j * ATT_TK:(j + 1) * ATT_TK, :] = vr[...]

    nk = 4 * ATT_TK
    row = lax.broadcasted_iota(jnp.int32, (ATT_TQ, nk), 0)
    col = lax.broadcasted_iota(jnp.int32, (ATT_TQ, nk), 1)
    rel = col - BAND_HALF - row
    kpos = i * ATT_TQ - BAND_HALF + col
    valid = (jnp.abs(rel) <= BAND_HALF) & (kpos >= 0) & (kpos < seq_len)
    dist = (jnp.abs(rel) * dil).astype(F32)
    lane = lax.broadcasted_iota(jnp.int32, (ATT_TQ, LANES), 1)
    first = lane < HEAD_DIM

    def pair_body(pp, carry):
        cols = pl.ds(pl.multiple_of(pp * LANES, LANES), LANES)
        q = q_ref[:, cols]
        kc = kc_scr[:, cols]
        vc = vc_scr[:, cols]
        outs, lses = [], []
        for hh in range(2):
            sel = first if hh == 0 else jnp.logical_not(first)
            qm = jnp.where(sel, q, jnp.zeros_like(q))
            s = _dot_nt(qm, kc) * (HEAD_DIM ** -0.5)
            s = s - slopes_ref[2 * pp + hh] * dist
            s = jnp.where(valid, s, NEG_INF)
            m = jnp.max(s, axis=-1, keepdims=True)
            p = jnp.exp(s - m)
            l = jnp.sum(p, axis=-1, keepdims=True)
            outs.append(jnp.dot(p.astype(BF16), vc, preferred_element_type=F32) / l)
            lses.append(m + jnp.log(l))
        o_ref[:, cols] = jnp.where(first, outs[0], outs[1]).astype(o_ref.dtype)
        lse_ref[:, cols] = jnp.where(first, lses[0], lses[1])
        return carry

    lax.fori_loop(0, ATT_HEADS // 2, pair_body, 0)


def _attention(qkv, slopes, batch, seq, dil):
    L = seq // dil
    view = qkv.reshape(batch, L, dil * N_ATT_COLS)
    nkb = L // ATT_TK
    ratio = ATT_TQ // ATT_TK

    def kv_spec(which, j):
        def imap(b, r, i, slopes_ref):
            blk = jnp.clip(i * ratio - 1 + j, 0, nkb - 1)
            return (b, blk, 3 * r + which)
        return pl.BlockSpec((None, ATT_TK, ATT_WIDTH), imap)

    grid_spec = pltpu.PrefetchScalarGridSpec(
        num_scalar_prefetch=1,
        grid=(batch, dil, L // ATT_TQ),
        in_specs=[pl.BlockSpec((None, ATT_TQ, ATT_WIDTH), lambda b, r, i, s: (b, i, 3 * r))]
        + [kv_spec(1, j) for j in range(4)] + [kv_spec(2, j) for j in range(4)],
        out_specs=[pl.BlockSpec((None, ATT_TQ, ATT_WIDTH), lambda b, r, i, s: (b, i, r)),
                   pl.BlockSpec((None, ATT_TQ, ATT_WIDTH), lambda b, r, i, s: (b, i, r))],
        scratch_shapes=[pltpu.VMEM((4 * ATT_TK, ATT_WIDTH), BF16),
                        pltpu.VMEM((4 * ATT_TK, ATT_WIDTH), BF16)],
    )
    o, lse = pl.pallas_call(
        functools.partial(_attn_kernel, seq_len=L, dil=dil),
        out_shape=[jax.ShapeDtypeStruct((batch, L, dil * ATT_WIDTH), BF16),
                   jax.ShapeDtypeStruct((batch, L, dil * ATT_WIDTH), F32)],
        grid_spec=grid_spec,
        compiler_params=_cparams(("parallel", "parallel", "arbitrary")),
        name=f"attn_d{dil}",
    )(slopes, view, *([view] * 8))
    return o.reshape(batch, seq, ATT_WIDTH), lse.reshape(batch, seq, ATT_WIDTH)


RWKV_TB = 256
PAIRS = RWKV_WIDTH // LANES


def _shifted(z, prev_row, next_row, mu_prev, mu_next):
    n = z.shape[0]
    ridx = lax.broadcasted_iota(jnp.int32, z.shape, 0)
    z_prev = jnp.where(ridx == 0, prev_row, pltpu.roll(z, 1, 0))
    z_next = jnp.where(ridx == n - 1, next_row, pltpu.roll(z, n - 1, 0))
    return z + mu_prev * (z_prev - z) + mu_next * (z_next - z)


def _halo_rows(zp_ref, zn_ref, t_blk, n_blk):
    prev_row = jnp.where(t_blk == 0, 0.0, zp_ref[7:8, :])
    next_row = jnp.where(t_blk == n_blk - 1, 0.0, zn_ref[0:1, :])
    return prev_row, next_row


def _stack_pair(x, first):
    zero = jnp.zeros_like(x)
    return jnp.concatenate([jnp.where(first, x, zero), jnp.where(first, zero, x)], axis=0)


def _rwkv_kernel(z_ref, zp_ref, zn_ref, mup_ref, mun_ref, w0_ref, w2_ref, a0_ref, a2_ref,
                 kk_ref, ka_ref, rk_ref, ones_ref, y_ref, bonus_ref,
                 at_s, bt_s, kt_s, rt_s, bd_s, kd_s, v_s, pc_s, st_s, *, n_blk):
    d = pl.program_id(1)
    step = pl.program_id(2)
    t_blk = jnp.where(d == 0, step, n_blk - 1 - step)
    nc = RWKV_TB // CHUNK

    @pl.when(step == 0)
    def _():
        st_s[...] = jnp.zeros_like(st_s)

    prev_row, next_row = _halo_rows(zp_ref, zn_ref, t_blk, n_blk)
    zs = _shifted(z_ref[...], prev_row, next_row, mup_ref[...], mun_ref[...])
    r = zs[:, 0:RWKV_WIDTH]
    k = zs[:, RWKV_WIDTH:2 * RWKV_WIDTH]
    v = zs[:, 2 * RWKV_WIDTH:3 * RWKV_WIDTH]
    wd = zs[:, WD_COL:WD_COL + 128]
    ad = zs[:, AD_COL:AD_COL + 128]
    pre_w = w0_ref[...] + _dot(jnp.tanh(wd), w2_ref[...])
    u = -pre_w
    softplus = jnp.maximum(u, 0.0) + jnp.log(1.0 + jnp.exp(-jnp.abs(u)))
    logw = -jnp.exp(-softplus - 0.5)
    iclr = jax.nn.sigmoid(a0_ref[...] + _dot(ad, a2_ref[...]))
    ones = ones_ref[...]
    kkr = k * kk_ref[...]
    kk = kkr / jnp.maximum(jnp.sqrt(_dot_exact_rhs(kkr * kkr, ones)), 1e-12)
    kdir = k * (1.0 + (iclr - 1.0) * ka_ref[...])
    bonus_ref[...] = _dot_exact_rhs(r * kdir * rk_ref[...], ones) * v
    a = -kk
    b = kk * iclr
    v_s[...] = v.astype(BF16)

    ti = lax.broadcasted_iota(jnp.int32, (CHUNK, CHUNK), 0)
    si = lax.broadcasted_iota(jnp.int32, (CHUNK, CHUNK), 1)
    fwd = d == 0
    sgn = jnp.where(fwd, 1, -1)
    tri = jnp.where((ti - si) * sgn >= 0, 1.0, 0.0).astype(BF16)
    for c in range(nc):
        rows = slice(c * CHUNK, (c + 1) * CHUNK)
        lw = logw[rows]
        lp_incl = _dot_exact_lhs(tri, lw)
        total = jnp.sum(lw, axis=0, keepdims=True)
        at_s[rows, :] = (a[rows] * jnp.exp(lp_incl - lw)).astype(BF16)
        e_neg = jnp.exp(-lp_incl)
        bt_s[rows, :] = (b[rows] * e_neg).astype(BF16)
        kt_s[rows, :] = (kdir[rows] * e_neg).astype(BF16)
        rt_s[rows, :] = r[rows] * jnp.exp(lp_incl)
        e_rest = jnp.exp(total - lp_incl)
        bd_s[rows, :] = (b[rows] * e_rest).astype(BF16)
        kd_s[rows, :] = (kdir[rows] * e_rest).astype(BF16)
        pc_s[c] = jnp.broadcast_to(jnp.exp(total), (8, RWKV_WIDTH))

    n2 = 2 * CHUNK
    rr = lax.broadcasted_iota(jnp.int32, (n2, n2), 0)
    cc = lax.broadcasted_iota(jnp.int32, (n2, n2), 1)
    same_head = (rr // CHUNK) == (cc // CHUNK)
    tt = rr % CHUNK
    ss = cc % CHUNK
    strict = same_head & ((tt - ss) * sgn > 0)
    incl = same_head & ((tt - ss) * sgn >= 0)
    blk16 = strict & ((tt // 16) == (ss // 16))
    off32 = strict & ((tt // 32) == (ss // 32)) & ((tt // 16) != (ss // 16))
    off64 = strict & ((tt // 32) != (ss // 32))
    eye = rr == cc
    eye_f = jnp.where(eye, 1.0, 0.0)
    lane = lax.broadcasted_iota(jnp.int32, (CHUNK, LANES), 1)
    first = lane < HEAD_DIM

    def chunk_body(ci, carry):
        c = jnp.where(fwd, ci, nc - 1 - ci)
        rows = pl.ds(pl.multiple_of(c * CHUNK, CHUNK), CHUNK)
        for p in range(PAIRS):
            cols = slice(p * LANES, (p + 1) * LANES)
            at = _stack_pair(at_s[rows, cols], first)
            bt = _stack_pair(bt_s[rows, cols], first)
            kt = _stack_pair(kt_s[rows, cols], first)
            rt = _stack_pair(rt_s[rows, cols], first)
            bd = _stack_pair(bd_s[rows, cols], first)
            kd = _stack_pair(kd_s[rows, cols], first)
            vv = _stack_pair(v_s[rows, cols], first)

            prod = _dot_nt(jnp.concatenate([at, rt.astype(BF16)], axis=0),
                           jnp.concatenate([bt, kt], axis=0))
            a_ab = jnp.where(strict, prod[:n2, :n2], 0.0)
            a_ak = jnp.where(strict, prod[:n2, n2:], 0.0)
            q_b = jnp.where(incl, prod[n2:, :n2], 0.0)
            q_k = jnp.where(incl, prod[n2:, n2:], 0.0)

            d0 = jnp.where(blk16, a_ab, 0.0)
            t_inv = eye_f + d0
            pw = _dot(d0, d0)
            for _ in range(2):
                both = _dot(jnp.concatenate([t_inv, pw], axis=0), pw)
                t_inv = t_inv + both[:n2]
                pw = both[n2:]
            t_inv = t_inv + _dot(t_inv, pw)
            for off_mask in (off32, off64):
                off = jnp.where(off_mask, a_ab, 0.0)
                t_inv = t_inv + _dot(t_inv, _dot(off, t_inv))

            av = _dot(a_ak, vv)
            wu = _dot(t_inv, jnp.concatenate([at.astype(F32), av], axis=1))
            ru = _dot(q_b, wu)
            r2 = rt + ru[:, :LANES]
            y0 = ru[:, LANES:] + _dot(q_k, vv)
            rhs = jnp.concatenate(
                [wu, jnp.concatenate([jnp.zeros((n2, LANES), F32), vv.astype(F32)], axis=1)], axis=0)
            gh = _dot_tn(jnp.concatenate([bd, kd], axis=0), rhs)

            st = st_s[p]
            y_st = _dot(r2, st) + y0
            y_ref[rows, cols] = y_st[:CHUNK] + y_st[CHUNK:]
            pc_col = jnp.sum(jnp.where(eye, pc_s[c][0:1, cols], 0.0), axis=1, keepdims=True)
            st_s[p] = pc_col * st + _dot(gh[:, :LANES], st) + gh[:, LANES:]
        return carry

    lax.fori_loop(0, nc, chunk_body, 0)


def _rwkv(z, batch, seq, mu_prev, mu_next, w0, w2, a0, a2, k_k, k_a, r_k, ones_bd):
    n_blk = seq // RWKV_TB
    hb = RWKV_TB // 8

    def tmap(d, i):
        return jnp.where(d == 0, i, n_blk - 1 - i)

    vec = lambda: pl.BlockSpec((1, RWKV_WIDTH), lambda b, d, i: (0, 0))
    zvec = lambda: pl.BlockSpec((1, N_RWKV_COLS), lambda b, d, i: (0, 0))
    in_specs = [
        pl.BlockSpec((None, RWKV_TB, N_RWKV_COLS), lambda b, d, i: (b, tmap(d, i), 0)),
        pl.BlockSpec((None, 8, N_RWKV_COLS), lambda b, d, i: (b, jnp.maximum(tmap(d, i) * hb - 1, 0), 0)),
        pl.BlockSpec((None, 8, N_RWKV_COLS),
                     lambda b, d, i: (b, jnp.minimum((tmap(d, i) + 1) * hb, seq // 8 - 1), 0)),
        zvec(), zvec(),
        pl.BlockSpec((None, 1, RWKV_WIDTH), lambda b, d, i: (d, 0, 0)),
        pl.BlockSpec((None, 128, RWKV_WIDTH), lambda b, d, i: (d, 0, 0)),
        pl.BlockSpec((None, 1, RWKV_WIDTH), lambda b, d, i: (d, 0, 0)),
        pl.BlockSpec((None, 128, RWKV_WIDTH), lambda b, d, i: (d, 0, 0)),
        vec(), vec(), vec(),
        pl.BlockSpec((RWKV_WIDTH, RWKV_WIDTH), lambda b, d, i: (0, 0)),
    ]
    out_spec = lambda: pl.BlockSpec((None, None, RWKV_TB, RWKV_WIDTH), lambda b, d, i: (d, b, tmap(d, i), 0))
    tok = lambda dt: pltpu.VMEM((RWKV_TB, RWKV_WIDTH), dt)
    return pl.pallas_call(
        functools.partial(_rwkv_kernel, n_blk=n_blk),
        out_shape=[jax.ShapeDtypeStruct((2, batch, seq, RWKV_WIDTH), F32)] * 2,
        grid=(batch, 2, n_blk),
        in_specs=in_specs,
        out_specs=[out_spec(), out_spec()],
        scratch_shapes=[tok(BF16), tok(BF16), tok(BF16), tok(F32), tok(BF16), tok(BF16), tok(BF16),
                        pltpu.VMEM((RWKV_TB // CHUNK, 8, RWKV_WIDTH), F32),
                        pltpu.VMEM((PAIRS, LANES, LANES), F32)],
        compiler_params=_cparams(("parallel", "parallel", "arbitrary")),
        name="rwkv",
    )(z, z, z, mu_prev, mu_next, w0, w2, a0, a2, k_k, k_a, r_k, ones_bd)


MIX_TM = 256


def _mixout_kernel(x_ref, o1, o2, o3, l1, l2, l3, y_ref, bonus_ref, gd_ref, gdp_ref, gdn_ref,
                   mup_ref, mun_ref, g2_ref, lnw_ref, lnb_ref, avg_ref, wo_ref, out_ref, *, n_blk):
    i = pl.program_id(1)
    m = jnp.maximum(jnp.maximum(l1[...], l2[...]), l3[...])
    e1 = jnp.exp(l1[...] - m)
    e2 = jnp.exp(l2[...] - m)
    e3 = jnp.exp(l3[...] - m)
    att = (e1 * o1[...].astype(F32) + e2 * o2[...].astype(F32) + e3 * o3[...].astype(F32)) / (e1 + e2 + e3)

    y = y_ref[0] + y_ref[1]
    avg = avg_ref[...]
    yc = y - _dot_exact_rhs(y, avg)
    var = _dot_exact_rhs(yc * yc, avg)
    yn = yc * lax.rsqrt(var + RWKV_LN_EPS) * lnw_ref[...] + lnb_ref[...]
    prev_row, next_row = _halo_rows(gdp_ref, gdn_ref, i, n_blk)
    gd = _shifted(gd_ref[...], prev_row, next_row, mup_ref[...], mun_ref[...])
    gate = _dot(jax.nn.sigmoid(gd), g2_ref[...])
    rw = (yn + bonus_ref[0] + bonus_ref[1]) * gate

    mixed = jnp.dot(att.astype(BF16), wo_ref[0:ATT_WIDTH, :], preferred_element_type=F32)
    mixed = mixed + jnp.dot(rw.astype(BF16), wo_ref[ATT_WIDTH:, :], preferred_element_type=F32)
    out_ref[...] = x_ref[...] + mixed


def _mixout(x1, os, lses, y, bonus, z, batch, seq, mu_prev_g, mu_next_g, g2, ln_w, ln_b, avg_bd, w_out):
    n_blk = seq // MIX_TM
    hb = MIX_TM // 8
    gcol = GD_COL // 128
    tok = lambda w: pl.BlockSpec((None, MIX_TM, w), lambda b, i: (b, i, 0))
    const = lambda shape: pl.BlockSpec(shape, lambda b, i: (0,) * len(shape))
    dir_spec = lambda: pl.BlockSpec((2, None, MIX_TM, RWKV_WIDTH), lambda b, i: (0, b, i, 0))
    in_specs = [tok(D_MODEL)] + [tok(ATT_WIDTH)] * 6 + [dir_spec(), dir_spec()] + [
        pl.BlockSpec((None, MIX_TM, 128), lambda b, i: (b, i, gcol)),
        pl.BlockSpec((None, 8, 128), lambda b, i: (b, jnp.maximum(i * hb - 1, 0), gcol)),
        pl.BlockSpec((None, 8, 128), lambda b, i: (b, jnp.minimum((i + 1) * hb, seq // 8 - 1), gcol)),
        const((1, 128)), const((1, 128)), const((128, RWKV_WIDTH)),
        const((1, RWKV_WIDTH)), const((1, RWKV_WIDTH)), const((RWKV_WIDTH, RWKV_WIDTH)),
        const((D_MODEL, D_MODEL)),
    ]
    return pl.pallas_call(
        functools.partial(_mixout_kernel, n_blk=n_blk),
        out_shape=jax.ShapeDtypeStruct((batch, seq, D_MODEL), F32),
        grid=(batch, n_blk),
        in_specs=in_specs,
        out_specs=tok(D_MODEL),
        compiler_params=_cparams(("parallel", "arbitrary")),
        name="mixout",
    )(x1, *os, *lses, y, bonus, z, z, z, mu_prev_g, mu_next_g, g2, ln_w, ln_b, avg_bd, w_out)


def _block_diag(width, block, value):
    idx = jnp.arange(width) // block
    return jnp.where(idx[:, None] == idx[None, :], value, 0.0).astype(BF16)


def _trunk(x, p):
    batch, seq, _ = x.shape
    xf = x.reshape(batch * seq, D_MODEL)
    x1 = _ffn(xf, p["ffn1_norm"], p["ffn1_gate"], p["ffn1_up"], p["ffn1_down"], p["final_norm"],
              final_norm=False)
    qkv = _inproj(x1, p["mix_norm"], p["w_in_att"], BF16, tn=768).reshape(batch, seq, N_ATT_COLS)
    z = _inproj(x1, p["mix_norm"], p["w_in_rwkv"], F32, tn=1152).reshape(batch, seq, N_RWKV_COLS)
    os, lses = [], []
    for dil in DILATIONS:
        o, lse = _attention(qkv, p["slopes"], batch, seq, dil)
        os.append(o)
        lses.append(lse)
    y, bonus = _rwkv(z, batch, seq, p["mu_prev"], p["mu_next"], p["w0"], p["w2"], p["a0"], p["a2"],
                     p["k_k"], p["k_a"], p["r_k"], p["ones_bd"])
    x2 = _mixout(x1.reshape(batch, seq, D_MODEL), os, lses, y, bonus, z, batch, seq,
                 p["mu_prev_g"], p["mu_next_g"], p["g2"], p["ln_x_w"], p["ln_x_b"], p["avg_bd"], p["w_out"])
    out = _ffn(x2.reshape(batch * seq, D_MODEL), p["ffn2_norm"], p["ffn2_gate"], p["ffn2_up"],
               p["ffn2_down"], p["final_norm"], final_norm=True)
    return out.reshape(batch, seq, D_MODEL)


def kernel(x_prompt, x_sample, ffn1_norm, ffn1_gate, ffn1_up, ffn1_down, mix_norm, w_in, w_out, mu_prev, mu_next, w0_f, w2_f, w0_b, w2_b, a0_f, a2_f, a0_b, a2_b, g2, k_k, k_a, r_k, ln_x_w, ln_x_b, ffn2_norm, ffn2_gate, ffn2_up, ffn2_down, final_norm):
    assert ffn1_norm.shape[0] == 1, "single layer"
    row = lambda t: t.reshape(1, -1)
    zero64 = jnp.zeros((64, RWKV_WIDTH), F32)
    w_in_b = w_in[0].astype(BF16)
    p = {
        "ffn1_norm": row(ffn1_norm[0]), "ffn1_gate": ffn1_gate[0].astype(BF16),
        "ffn1_up": ffn1_up[0].astype(BF16), "ffn1_down": ffn1_down[0].astype(BF16),
        "mix_norm": row(mix_norm[0]),
        "w_in_att": w_in_b[:, :N_ATT_COLS], "w_in_rwkv": w_in_b[:, N_ATT_COLS:],
        "w_out": w_out[0].astype(BF16),
        "mu_prev": row(mu_prev[0]), "mu_next": row(mu_next[0]),
        "mu_prev_g": row(mu_prev[0, GD_COL:]), "mu_next_g": row(mu_next[0, GD_COL:]),
        "w0": jnp.stack([row(w0_f[0]), row(w0_b[0])]),
        "w2": jnp.stack([jnp.concatenate([w2_f[0], zero64]), jnp.concatenate([zero64, w2_b[0]])]).astype(BF16),
        "a0": jnp.stack([row(a0_f[0]), row(a0_b[0])]),
        "a2": jnp.stack([jnp.concatenate([a2_f[0], zero64]), jnp.concatenate([zero64, a2_b[0]])]).astype(BF16),
        "g2": g2[0].astype(BF16),
        "k_k": row(k_k[0]), "k_a": row(k_a[0]), "r_k": row(r_k[0]),
        "ln_x_w": row(ln_x_w[0]), "ln_x_b": row(ln_x_b[0]),
        "ffn2_norm": row(ffn2_norm[0]), "ffn2_gate": ffn2_gate[0].astype(BF16),
        "ffn2_up": ffn2_up[0].astype(BF16), "ffn2_down": ffn2_down[0].astype(BF16),
        "final_norm": row(final_norm),
        "slopes": jnp.exp2(-8.0 * jnp.arange(1, ATT_HEADS + 1, dtype=F32) / ATT_HEADS),
        "ones_bd": _block_diag(RWKV_WIDTH, HEAD_DIM, 1.0),
        "avg_bd": _block_diag(RWKV_WIDTH, HEAD_DIM, 1.0 / HEAD_DIM),
    }
    return (_trunk(x_prompt, p), _trunk(x_sample, p))
```

```python
import functools

import jax
import jax.numpy as jnp
from jax import lax
from jax.experimental import pallas as pl
from jax.experimental.pallas import tpu as pltpu

F32 = jnp.float32
BF16 = jnp.bfloat16

D_MODEL = 2048
D_FF = 5632
HEAD_DIM = 64
ATT_WIDTH = 1024
ATT_HEADS = 16
RWKV_WIDTH = 1024
N_ATT_COLS = 3 * ATT_WIDTH
N_RWKV_COLS = 3 * RWKV_WIDTH + 64 * 4 + 128
WD_COL = 3 * RWKV_WIDTH
AD_COL = WD_COL + 128
GD_COL = AD_COL + 128
DILATIONS = (1, 4, 16)
BAND_HALF = 64
NORM_EPS = 1e-6
RWKV_LN_EPS = 64e-5
NEG_INF = -1e30

LANES = 128
HEAD_GROUP = 256
CHUNK = 64
VMEM_LIMIT = 56 * 1024 * 1024


def _cparams(sem):
    return pltpu.CompilerParams(dimension_semantics=sem, vmem_limit_bytes=VMEM_LIMIT)


def _dot(a, b):
    return jnp.dot(a.astype(BF16), b.astype(BF16), preferred_element_type=F32)


def _dot_nt(a, b):
    return lax.dot_general(a.astype(BF16), b.astype(BF16), (((1,), (1,)), ((), ())),
                           preferred_element_type=F32)


def _dot_tn(a, b):
    return lax.dot_general(a.astype(BF16), b.astype(BF16), (((0,), (0,)), ((), ())),
                           preferred_element_type=F32)


def _split2(x):
    hi = x.astype(BF16)
    lo = (x - hi.astype(F32)).astype(BF16)
    return hi, lo


def _head_sums(x, seg):
    hi, lo = _split2(x)
    outs = []
    for g in range(x.shape[1] // HEAD_GROUP):
        cols = slice(g * HEAD_GROUP, (g + 1) * HEAD_GROUP)
        outs.append(jnp.dot(hi[:, cols], seg, preferred_element_type=F32)
                    + jnp.dot(lo[:, cols], seg, preferred_element_type=F32))
    return jnp.concatenate(outs, axis=1)


def _cumsum_dot(tri_bf16, x):
    hi, lo = _split2(x)
    return (jnp.dot(tri_bf16, hi, preferred_element_type=F32)
            + jnp.dot(tri_bf16, lo, preferred_element_type=F32))


def _rmsnorm(x, g):
    return x * lax.rsqrt(jnp.mean(x * x, axis=-1, keepdims=True) + NORM_EPS) * g


def _ffn_kernel(x_ref, g_ref, wg_ref, wu_ref, wd_ref, fg_ref, o_ref, n_scr, acc_scr, *, final_norm):
    f = pl.program_id(1)

    @pl.when(f == 0)
    def _():
        n_scr[...] = _rmsnorm(x_ref[...], g_ref[...]).astype(BF16)
        acc_scr[...] = jnp.zeros_like(acc_scr)

    n = n_scr[...]
    hg = jnp.dot(n, wg_ref[...], preferred_element_type=F32)
    hu = jnp.dot(n, wu_ref[...], preferred_element_type=F32)
    h = (hg * jax.nn.sigmoid(hg) * hu).astype(BF16)
    acc_scr[...] += jnp.dot(h, wd_ref[...], preferred_element_type=F32)

    @pl.when(f == pl.num_programs(1) - 1)
    def _():
        out = x_ref[...] + 0.5 * acc_scr[...]
        if final_norm:
            out = _rmsnorm(out, fg_ref[...])
        o_ref[...] = out


def _ffn(x, g, wg, wu, wd, fg, *, final_norm, tm=512, tf=512):
    n_tok = x.shape[0]
    return pl.pallas_call(
        functools.partial(_ffn_kernel, final_norm=final_norm),
        out_shape=jax.ShapeDtypeStruct((n_tok, D_MODEL), F32),
        grid=(n_tok // tm, D_FF // tf),
        in_specs=[
            pl.BlockSpec((tm, D_MODEL), lambda i, f: (i, 0)),
            pl.BlockSpec((1, D_MODEL), lambda i, f: (0, 0)),
            pl.BlockSpec((D_MODEL, tf), lambda i, f: (0, f)),
            pl.BlockSpec((D_MODEL, tf), lambda i, f: (0, f)),
            pl.BlockSpec((tf, D_MODEL), lambda i, f: (f, 0)),
            pl.BlockSpec((1, D_MODEL), lambda i, f: (0, 0)),
        ],
        out_specs=pl.BlockSpec((tm, D_MODEL), lambda i, f: (i, 0)),
        scratch_shapes=[pltpu.VMEM((tm, D_MODEL), BF16), pltpu.VMEM((tm, D_MODEL), F32)],
        compiler_params=_cparams(("parallel", "arbitrary")),
        name="ffn_final" if final_norm else "ffn",
    )(x, g, wg, wu, wd, fg)


def _inproj_kernel(x_ref, g_ref, w_ref, o_ref, n_scr):
    @pl.when(pl.program_id(1) == 0)
    def _():
        n_scr[...] = _rmsnorm(x_ref[...], g_ref[...]).astype(BF16)

    o_ref[...] = jnp.dot(n_scr[...], w_ref[...], preferred_element_type=F32).astype(o_ref.dtype)


def _inproj(x, g, w, out_dtype, *, tn, tm=512):
    n_tok = x.shape[0]
    n_cols = w.shape[1]
    return pl.pallas_call(
        _inproj_kernel,
        out_shape=jax.ShapeDtypeStruct((n_tok, n_cols), out_dtype),
        grid=(n_tok // tm, n_cols // tn),
        in_specs=[
            pl.BlockSpec((tm, D_MODEL), lambda i, j: (i, 0)),
            pl.BlockSpec((1, D_MODEL), lambda i, j: (0, 0)),
            pl.BlockSpec((D_MODEL, tn), lambda i, j: (0, j)),
        ],
        out_specs=pl.BlockSpec((tm, tn), lambda i, j: (i, j)),
        scratch_shapes=[pltpu.VMEM((tm, D_MODEL), BF16)],
        compiler_params=_cparams(("parallel", "arbitrary")),
        name="inproj",
    )(x, g, w)


ATT_TQ = 128
ATT_TK = 64


def _attn_kernel(slopes_ref, q_ref, k0, k1, k2, k3, v0, v1, v2, v3, o_ref, lse_ref, kc_scr, vc_scr,
                 *, seq_len, dil):
    i = pl.program_id(2)
    for j, (kr, vr) in enumerate(((k0, v0), (k1, v1), (k2, v2), (k3, v3))):
        kc_scr[j * ATT_TK:(j + 1) * ATT_TK, :] = kr[...]
        vc_scr[j * ATT_TK:(j + 1) * ATT_TK, :] = vr[...]

    nk = 4 * ATT_TK
    row = lax.broadcasted_iota(jnp.int32, (ATT_TQ, nk), 0)
    col = lax.broadcasted_iota(jnp.int32, (ATT_TQ, nk), 1)
    rel = col - BAND_HALF - row
    kpos = i * ATT_TQ - BAND_HALF + col
    valid = (jnp.abs(rel) <= BAND_HALF) & (kpos >= 0) & (kpos < seq_len)
    dist = (jnp.abs(rel) * dil).astype(F32)
    lane = lax.broadcasted_iota(jnp.int32, (ATT_TQ, LANES), 1)
    first = lane < HEAD_DIM

    heads = range(ATT_HEADS)
    cols = [slice((h // 2) * LANES, (h // 2 + 1) * LANES) for h in heads]
    sel = [first if h % 2 == 0 else jnp.logical_not(first) for h in heads]
    qm = [jnp.where(sel[h], q_ref[:, cols[h]], jnp.zeros((ATT_TQ, LANES), BF16)) for h in heads]
    s = [_dot_nt(qm[h], kc_scr[:, cols[h]]) * (HEAD_DIM ** -0.5) - slopes_ref[h] * dist for h in heads]
    s = [jnp.where(valid, x, NEG_INF) for x in s]
    m = [jnp.max(x, axis=-1, keepdims=True) for x in s]
    p = [jnp.exp(x - mx) for x, mx in zip(s, m)]
    l = [jnp.sum(x, axis=-1, keepdims=True) for x in p]
    o = [jnp.dot(p[h].astype(BF16), vc_scr[:, cols[h]], preferred_element_type=F32) / l[h] for h in heads]
    lse = [mx + jnp.log(lx) for mx, lx in zip(m, l)]
    for h in range(0, ATT_HEADS, 2):
        o_ref[:, cols[h]] = jnp.where(first, o[h], o[h + 1]).astype(o_ref.dtype)
        lse_ref[:, cols[h]] = jnp.where(first, lse[h], lse[h + 1])


def _attention(qkv, slopes, batch, seq, dil):
    L = seq // dil
    view = qkv.reshape(batch, L, dil * N_ATT_COLS)
    nkb = L // ATT_TK
    ratio = ATT_TQ // ATT_TK

    def kv_spec(which, j):
        def imap(b, r, i, slopes_ref):
            blk = jnp.clip(i * ratio - 1 + j, 0, nkb - 1)
            return (b, blk, 3 * r + which)
        return pl.BlockSpec((None, ATT_TK, ATT_WIDTH), imap)

    grid_spec = pltpu.PrefetchScalarGridSpec(
        num_scalar_prefetch=1,
        grid=(batch, dil, L // ATT_TQ),
        in_specs=[pl.BlockSpec((None, ATT_TQ, ATT_WIDTH), lambda b, r, i, s: (b, i, 3 * r))]
        + [kv_spec(1, j) for j in range(4)] + [kv_spec(2, j) for j in range(4)],
        out_specs=[pl.BlockSpec((None, ATT_TQ, ATT_WIDTH), lambda b, r, i, s: (b, i, r)),
                   pl.BlockSpec((None, ATT_TQ, ATT_WIDTH), lambda b, r, i, s: (b, i, r))],
        scratch_shapes=[pltpu.VMEM((4 * ATT_TK, ATT_WIDTH), BF16),
                        pltpu.VMEM((4 * ATT_TK, ATT_WIDTH), BF16)],
    )
    o, lse = pl.pallas_call(
        functools.partial(_attn_kernel, seq_len=L, dil=dil),
        out_shape=[jax.ShapeDtypeStruct((batch, L, dil * ATT_WIDTH), BF16),
                   jax.ShapeDtypeStruct((batch, L, dil * ATT_WIDTH), F32)],
        grid_spec=grid_spec,
        compiler_params=_cparams(("parallel", "parallel", "arbitrary")),
        name=f"attn_d{dil}",
    )(slopes, view, *([view] * 8))
    return o.reshape(batch, seq, ATT_WIDTH), lse.reshape(batch, seq, ATT_WIDTH)


RWKV_TB = 256
PAIRS = RWKV_WIDTH // LANES


def _shifted(z, prev_row, next_row, mu_prev, mu_next):
    n = z.shape[0]
    ridx = lax.broadcasted_iota(jnp.int32, z.shape, 0)
    z_prev = jnp.where(ridx == 0, prev_row, pltpu.roll(z, 1, 0))
    z_next = jnp.where(ridx == n - 1, next_row, pltpu.roll(z, n - 1, 0))
    return z + mu_prev * (z_prev - z) + mu_next * (z_next - z)


def _halo_rows(zp_ref, zn_ref, t_blk, n_blk):
    prev_row = jnp.where(t_blk == 0, 0.0, zp_ref[7:8, :])
    next_row = jnp.where(t_blk == n_blk - 1, 0.0, zn_ref[0:1, :])
    return prev_row, next_row


def _stack_pair(x, first):
    zero = jnp.zeros_like(x)
    return jnp.concatenate([jnp.where(first, x, zero), jnp.where(first, zero, x)], axis=0)


def _rwkv_kernel(z_ref, zp_ref, zn_ref, mup_ref, mun_ref, w0_ref, w2_ref, a0_ref, a2_ref,
                 kk_ref, ka_ref, rk_ref, ones_ref, y_ref, bonus_ref,
                 at_s, bt_s, kt_s, rt_s, bd_s, kd_s, v_s, pc_s, st_s, *, n_blk):
    d = pl.program_id(1)
    step = pl.program_id(2)
    t_blk = jnp.where(d == 0, step, n_blk - 1 - step)
    nc = RWKV_TB // CHUNK

    @pl.when(step == 0)
    def _():
        st_s[...] = jnp.zeros_like(st_s)

    prev_row, next_row = _halo_rows(zp_ref, zn_ref, t_blk, n_blk)
    zs = _shifted(z_ref[...], prev_row, next_row, mup_ref[...], mun_ref[...])
    r = zs[:, 0:RWKV_WIDTH]
    k = zs[:, RWKV_WIDTH:2 * RWKV_WIDTH]
    v = zs[:, 2 * RWKV_WIDTH:3 * RWKV_WIDTH]
    wd = zs[:, WD_COL:WD_COL + 128]
    ad = zs[:, AD_COL:AD_COL + 128]
    pre_w = w0_ref[...] + _dot(jnp.tanh(wd), w2_ref[...])
    u = -pre_w
    softplus = jnp.maximum(u, 0.0) + jnp.log(1.0 + jnp.exp(-jnp.abs(u)))
    logw = -jnp.exp(-softplus - 0.5)
    iclr = jax.nn.sigmoid(a0_ref[...] + _dot(ad, a2_ref[...]))
    ones = ones_ref[...]
    kkr = k * kk_ref[...]
    kk = kkr / jnp.maximum(jnp.sqrt(_head_sums(kkr * kkr, ones)), 1e-12)
    kdir = k * (1.0 + (iclr - 1.0) * ka_ref[...])
    bonus_ref[...] = _head_sums(r * kdir * rk_ref[...], ones) * v
    a = -kk
    b = kk * iclr
    v_s[...] = v.astype(BF16)

    ti = lax.broadcasted_iota(jnp.int32, (CHUNK, CHUNK), 0)
    si = lax.broadcasted_iota(jnp.int32, (CHUNK, CHUNK), 1)
    fwd = d == 0
    sgn = jnp.where(fwd, 1, -1)
    tri = jnp.where((ti - si) * sgn >= 0, 1.0, 0.0).astype(BF16)
    for c in range(nc):
        rows = slice(c * CHUNK, (c + 1) * CHUNK)
        lw = logw[rows]
        lp_incl = _cumsum_dot(tri, lw)
        total = jnp.sum(lw, axis=0, keepdims=True)
        at_s[rows, :] = (a[rows] * jnp.exp(lp_incl - lw)).astype(BF16)
        e_neg = jnp.exp(-lp_incl)
        bt_s[rows, :] = (b[rows] * e_neg).astype(BF16)
        kt_s[rows, :] = (kdir[rows] * e_neg).astype(BF16)
        rt_s[rows, :] = r[rows] * jnp.exp(lp_incl)
        e_rest = jnp.exp(total - lp_incl)
        bd_s[rows, :] = (b[rows] * e_rest).astype(BF16)
        kd_s[rows, :] = (kdir[rows] * e_rest).astype(BF16)
        pc_s[c] = jnp.broadcast_to(jnp.exp(total), (8, RWKV_WIDTH))

    n2 = 2 * CHUNK
    rr = lax.broadcasted_iota(jnp.int32, (n2, n2), 0)
    cc = lax.broadcasted_iota(jnp.int32, (n2, n2), 1)
    same_head = (rr // CHUNK) == (cc // CHUNK)
    tt = rr % CHUNK
    ss = cc % CHUNK
    strict = same_head & ((tt - ss) * sgn > 0)
    incl = same_head & ((tt - ss) * sgn >= 0)
    blk16 = strict & ((tt // 16) == (ss // 16))
    off32 = strict & ((tt // 32) == (ss // 32)) & ((tt // 16) != (ss // 16))
    off64 = strict & ((tt // 32) != (ss // 32))
    eye = rr == cc
    eye_f = jnp.where(eye, 1.0, 0.0)
    lane = lax.broadcasted_iota(jnp.int32, (CHUNK, LANES), 1)
    first = lane < HEAD_DIM

    def chunk_body(ci, carry):
        c = jnp.where(fwd, ci, nc - 1 - ci)
        rows = pl.ds(pl.multiple_of(c * CHUNK, CHUNK), CHUNK)
        pairs = range(PAIRS)
        lanes = [slice(p * LANES, (p + 1) * LANES) for p in pairs]
        each = lambda fn, *lists: [fn(*xs) for xs in zip(*lists)]
        load = lambda ref: [_stack_pair(ref[rows, cols], first) for cols in lanes]
        at, bt, kt, rt, bd, kd, vv = (load(ref) for ref in (at_s, bt_s, kt_s, rt_s, bd_s, kd_s, v_s))

        prod = each(lambda a_, r_, b_, k_: _dot_nt(jnp.concatenate([a_, r_.astype(BF16)], axis=0),
                                                   jnp.concatenate([b_, k_], axis=0)), at, rt, bt, kt)
        a_ab = [jnp.where(strict, x[:n2, :n2], 0.0) for x in prod]
        a_ak = [jnp.where(strict, x[:n2, n2:], 0.0) for x in prod]
        q_b = [jnp.where(incl, x[n2:, :n2], 0.0) for x in prod]
        q_k = [jnp.where(incl, x[n2:, n2:], 0.0) for x in prod]

        d0 = [jnp.where(blk16, x, 0.0) for x in a_ab]
        t_inv = [eye_f + x for x in d0]
        pw = [_dot(x, x) for x in d0]
        for _ in range(2):
            both = each(lambda t_, p_: _dot(jnp.concatenate([t_, p_], axis=0), p_), t_inv, pw)
            t_inv = each(lambda t_, b_: t_ + b_[:n2], t_inv, both)
            pw = [x[n2:] for x in both]
        t_inv = each(lambda t_, p_: t_ + _dot(t_, p_), t_inv, pw)
        for off_mask in (off32, off64):
            inner = each(lambda a_, t_: _dot(jnp.where(off_mask, a_, 0.0), t_), a_ab, t_inv)
            t_inv = each(lambda t_, i_: t_ + _dot(t_, i_), t_inv, inner)

        av = each(_dot, a_ak, vv)
        wu = each(lambda t_, a_, v_: _dot(t_, jnp.concatenate([a_, v_.astype(BF16)], axis=1)),
                  t_inv, at, av)
        ru = each(_dot, q_b, wu)
        qv = each(_dot, q_k, vv)
        r2 = each(lambda r_, u_: r_ + u_[:, :LANES], rt, ru)
        y0 = each(lambda u_, q_: u_[:, LANES:] + q_, ru, qv)
        zero = jnp.zeros((n2, LANES), BF16)
        gh = each(lambda b_, k_, w_, v_: _dot_tn(
            jnp.concatenate([b_, k_], axis=0),
            jnp.concatenate([w_.astype(BF16), jnp.concatenate([zero, v_], axis=1)], axis=0)),
            bd, kd, wu, vv)

        st = [st_s[p] for p in pairs]
        y_st = each(lambda r_, s_, y_: _dot(r_, s_) + y_, r2, st, y0)
        gs = each(lambda g_, s_: _dot(g_[:, :LANES], s_), gh, st)
        pc_row = pc_s[c]
        for p in pairs:
            y_ref[rows, lanes[p]] = y_st[p][:CHUNK] + y_st[p][CHUNK:]
            pc_col = jnp.sum(jnp.where(eye, pc_row[0:1, lanes[p]], 0.0), axis=1, keepdims=True)
            st_s[p] = pc_col * st[p] + gs[p] + gh[p][:, LANES:]
        return carry

    lax.fori_loop(0, nc, chunk_body, 0)


def _rwkv(z, batch, seq, mu_prev, mu_next, w0, w2, a0, a2, k_k, k_a, r_k, ones_bd):
    n_blk = seq // RWKV_TB
    hb = RWKV_TB // 8

    def tmap(d, i):
        return jnp.where(d == 0, i, n_blk - 1 - i)

    vec = lambda: pl.BlockSpec((1, RWKV_WIDTH), lambda b, d, i: (0, 0))
    zvec = lambda: pl.BlockSpec((1, N_RWKV_COLS), lambda b, d, i: (0, 0))
    in_specs = [
        pl.BlockSpec((None, RWKV_TB, N_RWKV_COLS), lambda b, d, i: (b, tmap(d, i), 0)),
        pl.BlockSpec((None, 8, N_RWKV_COLS), lambda b, d, i: (b, jnp.maximum(tmap(d, i) * hb - 1, 0), 0)),
        pl.BlockSpec((None, 8, N_RWKV_COLS),
                     lambda b, d, i: (b, jnp.minimum((tmap(d, i) + 1) * hb, seq // 8 - 1), 0)),
        zvec(), zvec(),
        pl.BlockSpec((None, 1, RWKV_WIDTH), lambda b, d, i: (d, 0, 0)),
        pl.BlockSpec((None, 128, RWKV_WIDTH), lambda b, d, i: (d, 0, 0)),
        pl.BlockSpec((None, 1, RWKV_WIDTH), lambda b, d, i: (d, 0, 0)),
        pl.BlockSpec((None, 128, RWKV_WIDTH), lambda b, d, i: (d, 0, 0)),
        vec(), vec(), vec(),
        pl.BlockSpec((HEAD_GROUP, HEAD_GROUP), lambda b, d, i: (0, 0)),
    ]
    out_spec = lambda: pl.BlockSpec((None, None, RWKV_TB, RWKV_WIDTH), lambda b, d, i: (d, b, tmap(d, i), 0))
    tok = lambda dt: pltpu.VMEM((RWKV_TB, RWKV_WIDTH), dt)
    return pl.pallas_call(
        functools.partial(_rwkv_kernel, n_blk=n_blk),
        out_shape=[jax.ShapeDtypeStruct((2, batch, seq, RWKV_WIDTH), F32)] * 2,
        grid=(batch, 2, n_blk),
        in_specs=in_specs,
        out_specs=[out_spec(), out_spec()],
        scratch_shapes=[tok(BF16), tok(BF16), tok(BF16), tok(F32), tok(BF16), tok(BF16), tok(BF16),
                        pltpu.VMEM((RWKV_TB // CHUNK, 8, RWKV_WIDTH), F32),
                        pltpu.VMEM((PAIRS, LANES, LANES), F32)],
        compiler_params=_cparams(("parallel", "parallel", "arbitrary")),
        name="rwkv",
    )(z, z, z, mu_prev, mu_next, w0, w2, a0, a2, k_k, k_a, r_k, ones_bd)


MIX_TM = 256


def _mixout_kernel(x_ref, o1, o2, o3, l1, l2, l3, y_ref, bonus_ref, gd_ref, gdp_ref, gdn_ref,
                   mup_ref, mun_ref, g2_ref, lnw_ref, lnb_ref, avg_ref, wo_ref, out_ref, *, n_blk):
    i = pl.program_id(1)
    m = jnp.maximum(jnp.maximum(l1[...], l2[...]), l3[...])
    e1 = jnp.exp(l1[...] - m)
    e2 = jnp.exp(l2[...] - m)
    e3 = jnp.exp(l3[...] - m)
    att = (e1 * o1[...].astype(F32) + e2 * o2[...].astype(F32) + e3 * o3[...].astype(F32)) / (e1 + e2 + e3)

    y = y_ref[0] + y_ref[1]
    avg = avg_ref[...]
    yc = y - _head_sums(y, avg)
    var = _head_sums(yc * yc, avg)
    yn = yc * lax.rsqrt(var + RWKV_LN_EPS) * lnw_ref[...] + lnb_ref[...]
    prev_row, next_row = _halo_rows(gdp_ref, gdn_ref, i, n_blk)
    gd = _shifted(gd_ref[...], prev_row, next_row, mup_ref[...], mun_ref[...])
    gate = _dot(jax.nn.sigmoid(gd), g2_ref[...])
    rw = (yn + bonus_ref[0] + bonus_ref[1]) * gate

    mixed = jnp.dot(att.astype(BF16), wo_ref[0:ATT_WIDTH, :], preferred_element_type=F32)
    mixed = mixed + jnp.dot(rw.astype(BF16), wo_ref[ATT_WIDTH:, :], preferred_element_type=F32)
    out_ref[...] = x_ref[...] + mixed


def _mixout(x1, os, lses, y, bonus, z, batch, seq, mu_prev_g, mu_next_g, g2, ln_w, ln_b, avg_bd, w_out):
    n_blk = seq // MIX_TM
    hb = MIX_TM // 8
    gcol = GD_COL // 128
    tok = lambda w: pl.BlockSpec((None, MIX_TM, w), lambda b, i: (b, i, 0))
    const = lambda shape: pl.BlockSpec(shape, lambda b, i: (0,) * len(shape))
    dir_spec = lambda: pl.BlockSpec((2, None, MIX_TM, RWKV_WIDTH), lambda b, i: (0, b, i, 0))
    in_specs = [tok(D_MODEL)] + [tok(ATT_WIDTH)] * 6 + [dir_spec(), dir_spec()] + [
        pl.BlockSpec((None, MIX_TM, 128), lambda b, i: (b, i, gcol)),
        pl.BlockSpec((None, 8, 128), lambda b, i: (b, jnp.maximum(i * hb - 1, 0), gcol)),
        pl.BlockSpec((None, 8, 128), lambda b, i: (b, jnp.minimum((i + 1) * hb, seq // 8 - 1), gcol)),
        const((1, 128)), const((1, 128)), const((128, RWKV_WIDTH)),
        const((1, RWKV_WIDTH)), const((1, RWKV_WIDTH)), const((HEAD_GROUP, HEAD_GROUP)),
        const((D_MODEL, D_MODEL)),
    ]
    return pl.pallas_call(
        functools.partial(_mixout_kernel, n_blk=n_blk),
        out_shape=jax.ShapeDtypeStruct((batch, seq, D_MODEL), F32),
        grid=(batch, n_blk),
        in_specs=in_specs,
        out_specs=tok(D_MODEL),
        compiler_params=_cparams(("parallel", "arbitrary")),
        name="mixout",
    )(x1, *os, *lses, y, bonus, z, z, z, mu_prev_g, mu_next_g, g2, ln_w, ln_b, avg_bd, w_out)


def _block_diag(width, block, value):
    idx = jnp.arange(width) // block
    return jnp.where(idx[:, None] == idx[None, :], value, 0.0).astype(BF16)


def _trunk(x, p):
    batch, seq, _ = x.shape
    xf = x.reshape(batch * seq, D_MODEL)
    x1 = _ffn(xf, p["ffn1_norm"], p["ffn1_gate"], p["ffn1_up"], p["ffn1_down"], p["final_norm"],
              final_norm=False)
    qkv = _inproj(x1, p["mix_norm"], p["w_in_att"], BF16, tn=768).reshape(batch, seq, N_ATT_COLS)
    z = _inproj(x1, p["mix_norm"], p["w_in_rwkv"], F32, tn=1152).reshape(batch, seq, N_RWKV_COLS)
    os, lses = [], []
    for dil in DILATIONS:
        o, lse = _attention(qkv, p["slopes"], batch, seq, dil)
        os.append(o)
        lses.append(lse)
    y, bonus = _rwkv(z, batch, seq, p["mu_prev"], p["mu_next"], p["w0"], p["w2"], p["a0"], p["a2"],
                     p["k_k"], p["k_a"], p["r_k"], p["ones_bd"])
    x2 = _mixout(x1.reshape(batch, seq, D_MODEL), os, lses, y, bonus, z, batch, seq,
                 p["mu_prev_g"], p["mu_next_g"], p["g2"], p["ln_x_w"], p["ln_x_b"], p["avg_bd"], p["w_out"])
    out = _ffn(x2.reshape(batch * seq, D_MODEL), p["ffn2_norm"], p["ffn2_gate"], p["ffn2_up"],
               p["ffn2_down"], p["final_norm"], final_norm=True)
    return out.reshape(batch, seq, D_MODEL)


def kernel(x_prompt, x_sample, ffn1_norm, ffn1_gate, ffn1_up, ffn1_down, mix_norm, w_in, w_out, mu_prev, mu_next, w0_f, w2_f, w0_b, w2_b, a0_f, a2_f, a0_b, a2_b, g2, k_k, k_a, r_k, ln_x_w, ln_x_b, ffn2_norm, ffn2_gate, ffn2_up, ffn2_down, final_norm):
    assert ffn1_norm.shape[0] == 1, "single layer"
    row = lambda t: t.reshape(1, -1)
    zero64 = jnp.zeros((64, RWKV_WIDTH), F32)
    w_in_b = w_in[0].astype(BF16)
    p = {
        "ffn1_norm": row(ffn1_norm[0]), "ffn1_gate": ffn1_gate[0].astype(BF16),
        "ffn1_up": ffn1_up[0].astype(BF16), "ffn1_down": ffn1_down[0].astype(BF16),
        "mix_norm": row(mix_norm[0]),
        "w_in_att": w_in_b[:, :N_ATT_COLS], "w_in_rwkv": w_in_b[:, N_ATT_COLS:],
        "w_out": w_out[0].astype(BF16),
        "mu_prev": row(mu_prev[0]), "mu_next": row(mu_next[0]),
        "mu_prev_g": row(mu_prev[0, GD_COL:]), "mu_next_g": row(mu_next[0, GD_COL:]),
        "w0": jnp.stack([row(w0_f[0]), row(w0_b[0])]),
        "w2": jnp.stack([jnp.concatenate([w2_f[0], zero64]), jnp.concatenate([zero64, w2_b[0]])]).astype(BF16),
        "a0": jnp.stack([row(a0_f[0]), row(a0_b[0])]),
        "a2": jnp.stack([jnp.concatenate([a2_f[0], zero64]), jnp.concatenate([zero64, a2_b[0]])]).astype(BF16),
        "g2": g2[0].astype(BF16),
        "k_k": row(k_k[0]), "k_a": row(k_a[0]), "r_k": row(r_k[0]),
        "ln_x_w": row(ln_x_w[0]), "ln_x_b": row(ln_x_b[0]),
        "ffn2_norm": row(ffn2_norm[0]), "ffn2_gate": ffn2_gate[0].astype(BF16),
        "ffn2_up": ffn2_up[0].astype(BF16), "ffn2_down": ffn2_down[0].astype(BF16),
        "final_norm": row(final_norm),
        "slopes": jnp.exp2(-8.0 * jnp.arange(1, ATT_HEADS + 1, dtype=F32) / ATT_HEADS),
        "ones_bd": _block_diag(HEAD_GROUP, HEAD_DIM, 1.0),
        "avg_bd": _block_diag(HEAD_GROUP, HEAD_DIM, 1.0 / HEAD_DIM),
    }
    return (_trunk(x_prompt, p), _trunk(x_sample, p))
```

```python
import functools

import jax
import jax.numpy as jnp
from jax import lax
from jax.experimental import pallas as pl
from jax.experimental.pallas import tpu as pltpu

F32 = jnp.float32
BF16 = jnp.bfloat16

D_MODEL = 2048
D_FF = 5632
HEAD_DIM = 64
ATT_WIDTH = 1024
ATT_HEADS = 16
RWKV_WIDTH = 1024
N_ATT_COLS = 3 * ATT_WIDTH
N_RWKV_COLS = 3 * RWKV_WIDTH + 64 * 4 + 128
WD_COL = 3 * RWKV_WIDTH
AD_COL = WD_COL + 128
GD_COL = AD_COL + 128
DILATIONS = (1, 4, 16)
BAND_HALF = 64
NORM_EPS = 1e-6
RWKV_LN_EPS = 64e-5
NEG_INF = -1e30

LANES = 128
HEAD_GROUP = 256
CHUNK = 64
VMEM_LIMIT = 56 * 1024 * 1024


def _cparams(sem):
    return pltpu.CompilerParams(dimension_semantics=sem, vmem_limit_bytes=VMEM_LIMIT)


def _dot(a, b):
    return jnp.dot(a.astype(BF16), b.astype(BF16), preferred_element_type=F32)


def _dot_nt(a, b):
    return lax.dot_general(a.astype(BF16), b.astype(BF16), (((1,), (1,)), ((), ())),
                           preferred_element_type=F32)


def _dot_tn(a, b):
    return lax.dot_general(a.astype(BF16), b.astype(BF16), (((0,), (0,)), ((), ())),
                           preferred_element_type=F32)


def _split2(x):
    hi = x.astype(BF16)
    lo = (x - hi.astype(F32)).astype(BF16)
    return hi, lo


def _head_sums(x, seg):
    hi, lo = _split2(x)
    outs = []
    for g in range(x.shape[1] // HEAD_GROUP):
        cols = slice(g * HEAD_GROUP, (g + 1) * HEAD_GROUP)
        outs.append(jnp.dot(hi[:, cols], seg, preferred_element_type=F32)
                    + jnp.dot(lo[:, cols], seg, preferred_element_type=F32))
    return jnp.concatenate(outs, axis=1)


def _cumsum_dot(tri_bf16, x):
    hi, lo = _split2(x)
    return (jnp.dot(tri_bf16, hi, preferred_element_type=F32)
            + jnp.dot(tri_bf16, lo, preferred_element_type=F32))


def _rmsnorm(x, g):
    return x * lax.rsqrt(jnp.mean(x * x, axis=-1, keepdims=True) + NORM_EPS) * g


def _ffn_kernel(x_ref, g_ref, wg_ref, wu_ref, wd_ref, fg_ref, o_ref, n_scr, *, final_norm):
    f = pl.program_id(1)

    @pl.when(f == 0)
    def _():
        x = x_ref[...]
        n_scr[...] = _rmsnorm(x, g_ref[...]).astype(BF16)
        o_ref[...] = x

    n = n_scr[...]
    hg = jnp.dot(n, wg_ref[...], preferred_element_type=F32)
    hu = jnp.dot(n, wu_ref[...], preferred_element_type=F32)
    h = (hg * jax.nn.sigmoid(hg) * (0.5 * hu)).astype(BF16)
    o_ref[...] += jnp.dot(h, wd_ref[...], preferred_element_type=F32)

    if final_norm:
        @pl.when(f == pl.num_programs(1) - 1)
        def _():
            o_ref[...] = _rmsnorm(o_ref[...], fg_ref[...])


def _ffn(x, g, wg, wu, wd, fg, *, final_norm, tm=1024, tf=512):
    n_tok = x.shape[0]
    return pl.pallas_call(
        functools.partial(_ffn_kernel, final_norm=final_norm),
        out_shape=jax.ShapeDtypeStruct((n_tok, D_MODEL), F32),
        grid=(n_tok // tm, D_FF // tf),
        in_specs=[
            pl.BlockSpec((tm, D_MODEL), lambda i, f: (i, 0)),
            pl.BlockSpec((1, D_MODEL), lambda i, f: (0, 0)),
            pl.BlockSpec((D_MODEL, tf), lambda i, f: (0, f)),
            pl.BlockSpec((D_MODEL, tf), lambda i, f: (0, f)),
            pl.BlockSpec((tf, D_MODEL), lambda i, f: (f, 0)),
            pl.BlockSpec((1, D_MODEL), lambda i, f: (0, 0)),
        ],
        out_specs=pl.BlockSpec((tm, D_MODEL), lambda i, f: (i, 0)),
        scratch_shapes=[pltpu.VMEM((tm, D_MODEL), BF16)],
        compiler_params=_cparams(("parallel", "arbitrary")),
        name="ffn_final" if final_norm else "ffn",
    )(x, g, wg, wu, wd, fg)


def _inproj_kernel(x_ref, g_ref, w_ref, o_ref, n_scr):
    @pl.when(pl.program_id(1) == 0)
    def _():
        n_scr[...] = _rmsnorm(x_ref[...], g_ref[...]).astype(BF16)

    o_ref[...] = jnp.dot(n_scr[...], w_ref[...], preferred_element_type=F32).astype(o_ref.dtype)


def _inproj(x, g, w, out_dtype, *, tn, tm=1024):
    n_tok = x.shape[0]
    n_cols = w.shape[1]
    return pl.pallas_call(
        _inproj_kernel,
        out_shape=jax.ShapeDtypeStruct((n_tok, n_cols), out_dtype),
        grid=(n_tok // tm, n_cols // tn),
        in_specs=[
            pl.BlockSpec((tm, D_MODEL), lambda i, j: (i, 0)),
            pl.BlockSpec((1, D_MODEL), lambda i, j: (0, 0)),
            pl.BlockSpec((D_MODEL, tn), lambda i, j: (0, j)),
        ],
        out_specs=pl.BlockSpec((tm, tn), lambda i, j: (i, j)),
        scratch_shapes=[pltpu.VMEM((tm, D_MODEL), BF16)],
        compiler_params=_cparams(("parallel", "arbitrary")),
        name="inproj",
    )(x, g, w)


ATT_TQ = 128
ATT_TK = 64


def _attn_kernel(slopes_ref, q_ref, k0, k1, k2, k3, v0, v1, v2, v3, o_ref, lse_ref, kc_scr, vc_scr,
                 *, seq_len, dil):
    i = pl.program_id(2)
    for j, (kr, vr) in enumerate(((k0, v0), (k1, v1), (k2, v2), (k3, v3))):
        kc_scr[j * ATT_TK:(j + 1) * ATT_TK, :] = kr[...]
        vc_scr[j * ATT_TK:(j + 1) * ATT_TK, :] = vr[...]

    nk = 4 * ATT_TK
    row = lax.broadcasted_iota(jnp.int32, (ATT_TQ, nk), 0)
    col = lax.broadcasted_iota(jnp.int32, (ATT_TQ, nk), 1)
    rel = col - BAND_HALF - row
    kpos = i * ATT_TQ - BAND_HALF + col
    valid = (jnp.abs(rel) <= BAND_HALF) & (kpos >= 0) & (kpos < seq_len)
    dist = (jnp.abs(rel) * dil).astype(F32)
    lane = lax.broadcasted_iota(jnp.int32, (ATT_TQ, LANES), 1)
    first = lane < HEAD_DIM

    heads = range(ATT_HEADS)
    cols = [slice((h // 2) * LANES, (h // 2 + 1) * LANES) for h in heads]
    sel = [first if h % 2 == 0 else jnp.logical_not(first) for h in heads]
    qm = [jnp.where(sel[h], q_ref[:, cols[h]], jnp.zeros((ATT_TQ, LANES), BF16)) for h in heads]
    s = [_dot_nt(qm[h], kc_scr[:, cols[h]]) * (HEAD_DIM ** -0.5) - slopes_ref[h] * dist for h in heads]
    s = [jnp.where(valid, x, NEG_INF) for x in s]
    m = [jnp.max(x, axis=-1, keepdims=True) for x in s]
    p = [jnp.exp(x - mx) for x, mx in zip(s, m)]
    l = [jnp.sum(x, axis=-1, keepdims=True) for x in p]
    o = [jnp.dot(p[h].astype(BF16), vc_scr[:, cols[h]], preferred_element_type=F32) / l[h] for h in heads]
    lse = [mx + jnp.log(lx) for mx, lx in zip(m, l)]
    for h in range(0, ATT_HEADS, 2):
        o_ref[:, cols[h]] = jnp.where(first, o[h], o[h + 1]).astype(o_ref.dtype)
        lse_ref[:, cols[h]] = jnp.where(first, lse[h], lse[h + 1])


def _attention(qkv, slopes, batch, seq, dil):
    L = seq // dil
    view = qkv.reshape(batch, L, dil * N_ATT_COLS)
    nkb = L // ATT_TK
    ratio = ATT_TQ // ATT_TK

    def kv_spec(which, j):
        def imap(b, r, i, slopes_ref):
            blk = jnp.clip(i * ratio - 1 + j, 0, nkb - 1)
            return (b, blk, 3 * r + which)
        return pl.BlockSpec((None, ATT_TK, ATT_WIDTH), imap)

    grid_spec = pltpu.PrefetchScalarGridSpec(
        num_scalar_prefetch=1,
        grid=(batch, dil, L // ATT_TQ),
        in_specs=[pl.BlockSpec((None, ATT_TQ, ATT_WIDTH), lambda b, r, i, s: (b, i, 3 * r))]
        + [kv_spec(1, j) for j in range(4)] + [kv_spec(2, j) for j in range(4)],
        out_specs=[pl.BlockSpec((None, ATT_TQ, ATT_WIDTH), lambda b, r, i, s: (b, i, r)),
                   pl.BlockSpec((None, ATT_TQ, ATT_WIDTH), lambda b, r, i, s: (b, i, r))],
        scratch_shapes=[pltpu.VMEM((4 * ATT_TK, ATT_WIDTH), BF16),
                        pltpu.VMEM((4 * ATT_TK, ATT_WIDTH), BF16)],
    )
    o, lse = pl.pallas_call(
        functools.partial(_attn_kernel, seq_len=L, dil=dil),
        out_shape=[jax.ShapeDtypeStruct((batch, L, dil * ATT_WIDTH), BF16),
                   jax.ShapeDtypeStruct((batch, L, dil * ATT_WIDTH), F32)],
        grid_spec=grid_spec,
        compiler_params=_cparams(("parallel", "parallel", "arbitrary")),
        name=f"attn_d{dil}",
    )(slopes, view, *([view] * 8))
    return o.reshape(batch, seq, ATT_WIDTH), lse.reshape(batch, seq, ATT_WIDTH)


RWKV_TB = 256
PAIRS = RWKV_WIDTH // LANES


def _shifted(z, prev_row, next_row, mu_prev, mu_next):
    n = z.shape[0]
    ridx = lax.broadcasted_iota(jnp.int32, z.shape, 0)
    z_prev = jnp.where(ridx == 0, prev_row, pltpu.roll(z, 1, 0))
    z_next = jnp.where(ridx == n - 1, next_row, pltpu.roll(z, n - 1, 0))
    return z + mu_prev * (z_prev - z) + mu_next * (z_next - z)


def _halo_rows(zp_ref, zn_ref, t_blk, n_blk):
    prev_row = jnp.where(t_blk == 0, 0.0, zp_ref[7:8, :])
    next_row = jnp.where(t_blk == n_blk - 1, 0.0, zn_ref[0:1, :])
    return prev_row, next_row


def _stack_pair(x, first):
    zero = jnp.zeros_like(x)
    return jnp.concatenate([jnp.where(first, x, zero), jnp.where(first, zero, x)], axis=0)


def _rwkv_kernel(z_ref, zp_ref, zn_ref, mup_ref, mun_ref, w0_ref, w2_ref, a0_ref, a2_ref,
                 kk_ref, ka_ref, rk_ref, ones_ref, y_ref, bonus_ref,
                 at_s, bt_s, kt_s, rt_s, bd_s, kd_s, v_s, pc_s, st_s, *, n_blk):
    d = pl.program_id(1)
    step = pl.program_id(2)
    nc = RWKV_TB // CHUNK
    fwd = d == 0
    sgn = jnp.where(fwd, 1, -1)
    prep_step = jnp.minimum(step, n_blk - 1)
    t_blk = jnp.where(fwd, prep_step, n_blk - 1 - prep_step)
    wr = step % 2
    rd = 1 - wr

    @pl.when(step == 0)
    def _():
        st_s[...] = jnp.zeros_like(st_s)
        for ref in (at_s, bt_s, kt_s, rt_s, bd_s, kd_s, v_s, pc_s):
            ref[1] = jnp.zeros(ref.shape[1:], ref.dtype)

    halo_top = jnp.where(t_blk == 0, 0.0, zp_ref[...])
    halo_bot = jnp.where(t_blk == n_blk - 1, 0.0, zn_ref[...])
    mu_p = mup_ref[...]
    mu_n = mun_ref[...]
    mu_c = 1.0 - mu_p - mu_n
    ti = lax.broadcasted_iota(jnp.int32, (CHUNK, CHUNK), 0)
    si = lax.broadcasted_iota(jnp.int32, (CHUNK, CHUNK), 1)
    tri = jnp.where((ti - si) * sgn >= 0, 1.0, 0.0).astype(BF16)

    def prepare(cp):
        r0 = pl.multiple_of(cp * CHUNK, CHUNK)
        rows = pl.ds(r0, CHUNK)
        top_rows = pl.ds(pl.multiple_of(jnp.maximum(r0 - 8, 0), 8), 8)
        bot_rows = pl.ds(pl.multiple_of(jnp.minimum(r0 + CHUNK, RWKV_TB - 8), 8), 8)

        def shifted(cols):
            top = jnp.where(cp == 0, halo_top[:, cols], z_ref[top_rows, cols])
            bot = jnp.where(cp == nc - 1, halo_bot[:, cols], z_ref[bot_rows, cols])
            zc = z_ref[rows, cols]
            win = jnp.concatenate([top, zc, bot], axis=0)
            z_prev = pltpu.roll(win, 1, 0)[8:8 + CHUNK]
            z_next = pltpu.roll(win, CHUNK + 15, 0)[8:8 + CHUNK]
            return zc * mu_c[:, cols] + mu_p[:, cols] * z_prev + mu_n[:, cols] * z_next

        r = shifted(slice(0, RWKV_WIDTH))
        yield
        k = shifted(slice(RWKV_WIDTH, 2 * RWKV_WIDTH))
        yield
        v = shifted(slice(2 * RWKV_WIDTH, 3 * RWKV_WIDTH))
        v_s[wr, rows, :] = v.astype(BF16)
        yield
        low = shifted(slice(WD_COL, GD_COL))
        wd = low[:, 0:128]
        ad = low[:, 128:256]
        u = -(w0_ref[...] + _dot(jnp.tanh(wd), w2_ref[...]))
        softplus = jnp.maximum(u, 0.0) + jnp.log(1.0 + jnp.exp(-jnp.abs(u)))
        lw = -jnp.exp(-softplus - 0.5)
        yield
        iclr = jax.nn.sigmoid(a0_ref[...] + _dot(ad, a2_ref[...]))
        ones = ones_ref[...]
        kkr = k * kk_ref[...]
        kk = kkr / jnp.maximum(jnp.sqrt(_head_sums(kkr * kkr, ones)), 1e-12)
        yield
        kdir = k * (1.0 + (iclr - 1.0) * ka_ref[...])
        bonus_ref[rows, :] = _head_sums(r * kdir * rk_ref[...], ones) * v
        yield
        b = kk * iclr
        lp_incl = _cumsum_dot(tri, lw)
        total = jnp.sum(lw, axis=0, keepdims=True)
        pc_s[wr, cp] = jnp.broadcast_to(jnp.exp(total), (8, RWKV_WIDTH))
        yield
        e_neg = jnp.exp(-lp_incl)
        at_s[wr, rows, :] = (-kk * jnp.exp(lp_incl - lw)).astype(BF16)
        bt_s[wr, rows, :] = (b * e_neg).astype(BF16)
        yield
        kt_s[wr, rows, :] = (kdir * e_neg).astype(BF16)
        rt_s[wr, rows, :] = r * jnp.exp(lp_incl)
        yield
        e_rest = jnp.exp(total - lp_incl)
        bd_s[wr, rows, :] = (b * e_rest).astype(BF16)
        kd_s[wr, rows, :] = (kdir * e_rest).astype(BF16)

    n2 = 2 * CHUNK
    rr = lax.broadcasted_iota(jnp.int32, (n2, n2), 0)
    cc = lax.broadcasted_iota(jnp.int32, (n2, n2), 1)
    same_head = (rr // CHUNK) == (cc // CHUNK)
    tt = rr % CHUNK
    ss = cc % CHUNK
    strict = same_head & ((tt - ss) * sgn > 0)
    incl = same_head & ((tt - ss) * sgn >= 0)
    blk16 = strict & ((tt // 16) == (ss // 16))
    off32 = strict & ((tt // 32) == (ss // 32)) & ((tt // 16) != (ss // 16))
    off64 = strict & ((tt // 32) != (ss // 32))
    eye = rr == cc
    eye_f = jnp.where(eye, 1.0, 0.0)
    lane = lax.broadcasted_iota(jnp.int32, (CHUNK, LANES), 1)
    first = lane < HEAD_DIM

    def chunk_body(ci, carry):
        stages = prepare(ci)
        tick = lambda: next(stages, None)
        c = jnp.where(fwd, ci, nc - 1 - ci)
        rows = pl.ds(pl.multiple_of(c * CHUNK, CHUNK), CHUNK)
        pairs = range(PAIRS)
        lanes = [slice(p * LANES, (p + 1) * LANES) for p in pairs]
        each = lambda fn, *lists: [fn(*xs) for xs in zip(*lists)]
        load = lambda ref: [_stack_pair(ref[rd, rows, cols], first) for cols in lanes]
        at, bt, kt, rt, bd, kd, vv = (load(ref) for ref in (at_s, bt_s, kt_s, rt_s, bd_s, kd_s, v_s))
        tick()

        prod = each(lambda a_, r_, b_, k_: _dot_nt(jnp.concatenate([a_, r_.astype(BF16)], axis=0),
                                                   jnp.concatenate([b_, k_], axis=0)), at, rt, bt, kt)
        a_ab = [jnp.where(strict, x[:n2, :n2], 0.0) for x in prod]
        a_ak = [jnp.where(strict, x[:n2, n2:], 0.0) for x in prod]
        q_b = [jnp.where(incl, x[n2:, :n2], 0.0) for x in prod]
        q_k = [jnp.where(incl, x[n2:, n2:], 0.0) for x in prod]
        tick()

        d0 = [jnp.where(blk16, x, 0.0) for x in a_ab]
        t_inv = [eye_f + x for x in d0]
        pw = [_dot(x, x) for x in d0]
        tick()
        for _ in range(2):
            both = each(lambda t_, p_: _dot(jnp.concatenate([t_, p_], axis=0), p_), t_inv, pw)
            t_inv = each(lambda t_, b_: t_ + b_[:n2], t_inv, both)
            pw = [x[n2:] for x in both]
            tick()
        t_inv = each(lambda t_, p_: t_ + _dot(t_, p_), t_inv, pw)
        tick()
        for off_mask in (off32, off64):
            inner = each(lambda a_, t_: _dot(jnp.where(off_mask, a_, 0.0), t_), a_ab, t_inv)
            t_inv = each(lambda t_, i_: t_ + _dot(t_, i_), t_inv, inner)
            tick()

        av = each(_dot, a_ak, vv)
        tick()
        wu = each(lambda t_, a_, v_: _dot(t_, jnp.concatenate([a_, v_.astype(BF16)], axis=1)),
                  t_inv, at, av)
        tick()
        ru = each(_dot, q_b, wu)
        qv = each(_dot, q_k, vv)
        r2 = each(lambda r_, u_: r_ + u_[:, :LANES], rt, ru)
        y0 = each(lambda u_, q_: u_[:, LANES:] + q_, ru, qv)
        zero = jnp.zeros((n2, LANES), BF16)
        gh = each(lambda b_, k_, w_, v_: _dot_tn(
            jnp.concatenate([b_, k_], axis=0),
            jnp.concatenate([w_.astype(BF16), jnp.concatenate([zero, v_], axis=1)], axis=0)),
            bd, kd, wu, vv)
        for _ in stages:
            pass

        st = [st_s[p] for p in pairs]
        y_st = each(lambda r_, s_, y_: _dot(r_, s_) + y_, r2, st, y0)
        gs = each(lambda g_, s_: _dot(g_[:, :LANES], s_), gh, st)
        pc_row = pc_s[rd, c]
        for p in pairs:
            y_ref[rows, lanes[p]] = y_st[p][:CHUNK] + y_st[p][CHUNK:]
            pc_col = jnp.sum(jnp.where(eye, pc_row[0:1, lanes[p]], 0.0), axis=1, keepdims=True)
            st_s[p] = pc_col * st[p] + gs[p] + gh[p][:, LANES:]
        return carry

    lax.fori_loop(0, nc, chunk_body, 0)


def _rwkv(z, batch, seq, mu_prev, mu_next, w0, w2, a0, a2, k_k, k_a, r_k, ones_bd):
    n_blk = seq // RWKV_TB
    hb = RWKV_TB // 8

    def time_block(d, i):
        return jnp.where(d == 0, i, n_blk - 1 - i)

    def tmap(d, i):
        return time_block(d, jnp.minimum(i, n_blk - 1))

    def smap(d, i):
        return time_block(d, jnp.maximum(i - 1, 0))

    vec = lambda: pl.BlockSpec((1, RWKV_WIDTH), lambda b, d, i: (0, 0))
    zvec = lambda: pl.BlockSpec((1, N_RWKV_COLS), lambda b, d, i: (0, 0))
    in_specs = [
        pl.BlockSpec((None, RWKV_TB, N_RWKV_COLS), lambda b, d, i: (b, tmap(d, i), 0)),
        pl.BlockSpec((None, 8, N_RWKV_COLS), lambda b, d, i: (b, jnp.maximum(tmap(d, i) * hb - 1, 0), 0)),
        pl.BlockSpec((None, 8, N_RWKV_COLS),
                     lambda b, d, i: (b, jnp.minimum((tmap(d, i) + 1) * hb, seq // 8 - 1), 0)),
        zvec(), zvec(),
        pl.BlockSpec((None, 1, RWKV_WIDTH), lambda b, d, i: (d, 0, 0)),
        pl.BlockSpec((None, 128, RWKV_WIDTH), lambda b, d, i: (d, 0, 0)),
        pl.BlockSpec((None, 1, RWKV_WIDTH), lambda b, d, i: (d, 0, 0)),
        pl.BlockSpec((None, 128, RWKV_WIDTH), lambda b, d, i: (d, 0, 0)),
        vec(), vec(), vec(),
        pl.BlockSpec((HEAD_GROUP, HEAD_GROUP), lambda b, d, i: (0, 0)),
    ]
    out_spec = lambda m: pl.BlockSpec((None, None, RWKV_TB, RWKV_WIDTH), lambda b, d, i: (d, b, m(d, i), 0))
    tok = lambda dt: pltpu.VMEM((2, RWKV_TB, RWKV_WIDTH), dt)
    return pl.pallas_call(
        functools.partial(_rwkv_kernel, n_blk=n_blk),
        out_shape=[jax.ShapeDtypeStruct((2, batch, seq, RWKV_WIDTH), F32)] * 2,
        grid=(batch, 2, n_blk + 1),
        in_specs=in_specs,
        out_specs=[out_spec(smap), out_spec(tmap)],
        scratch_shapes=[tok(BF16), tok(BF16), tok(BF16), tok(F32), tok(BF16), tok(BF16), tok(BF16),
                        pltpu.VMEM((2, RWKV_TB // CHUNK, 8, RWKV_WIDTH), F32),
                        pltpu.VMEM((PAIRS, LANES, LANES), F32)],
        compiler_params=_cparams(("parallel", "parallel", "arbitrary")),
        name="rwkv",
    )(z, z, z, mu_prev, mu_next, w0, w2, a0, a2, k_k, k_a, r_k, ones_bd)


MIX_TM = 256


def _mixout_kernel(x_ref, o1, o2, o3, l1, l2, l3, y_ref, bonus_ref, gd_ref, gdp_ref, gdn_ref,
                   mup_ref, mun_ref, g2_ref, lnw_ref, lnb_ref, avg_ref, wo_ref, out_ref, *, n_blk):
    i = pl.program_id(1)
    m = jnp.maximum(jnp.maximum(l1[...], l2[...]), l3[...])
    e1 = jnp.exp(l1[...] - m)
    e2 = jnp.exp(l2[...] - m)
    e3 = jnp.exp(l3[...] - m)
    att = (e1 * o1[...].astype(F32) + e2 * o2[...].astype(F32) + e3 * o3[...].astype(F32)) / (e1 + e2 + e3)

    y = y_ref[0] + y_ref[1]
    avg = avg_ref[...]
    yc = y - _head_sums(y, avg)
    var = _head_sums(yc * yc, avg)
    yn = yc * lax.rsqrt(var + RWKV_LN_EPS) * lnw_ref[...] + lnb_ref[...]
    prev_row, next_row = _halo_rows(gdp_ref, gdn_ref, i, n_blk)
    gd = _shifted(gd_ref[...], prev_row, next_row, mup_ref[...], mun_ref[...])
    gate = _dot(jax.nn.sigmoid(gd), g2_ref[...])
    rw = (yn + bonus_ref[0] + bonus_ref[1]) * gate

    mixed = jnp.dot(att.astype(BF16), wo_ref[0:ATT_WIDTH, :], preferred_element_type=F32)
    mixed = mixed + jnp.dot(rw.astype(BF16), wo_ref[ATT_WIDTH:, :], preferred_element_type=F32)
    out_ref[...] = x_ref[...] + mixed


def _mixout(x1, os, lses, y, bonus, z, batch, seq, mu_prev_g, mu_next_g, g2, ln_w, ln_b, avg_bd, w_out):
    n_blk = seq // MIX_TM
    hb = MIX_TM // 8
    gcol = GD_COL // 128
    tok = lambda w: pl.BlockSpec((None, MIX_TM, w), lambda b, i: (b, i, 0))
    const = lambda shape: pl.BlockSpec(shape, lambda b, i: (0,) * len(shape))
    dir_spec = lambda: pl.BlockSpec((2, None, MIX_TM, RWKV_WIDTH), lambda b, i: (0, b, i, 0))
    in_specs = [tok(D_MODEL)] + [tok(ATT_WIDTH)] * 6 + [dir_spec(), dir_spec()] + [
        pl.BlockSpec((None, MIX_TM, 128), lambda b, i: (b, i, gcol)),
        pl.BlockSpec((None, 8, 128), lambda b, i: (b, jnp.maximum(i * hb - 1, 0), gcol)),
        pl.BlockSpec((None, 8, 128), lambda b, i: (b, jnp.minimum((i + 1) * hb, seq // 8 - 1), gcol)),
        const((1, 128)), const((1, 128)), const((128, RWKV_WIDTH)),
        const((1, RWKV_WIDTH)), const((1, RWKV_WIDTH)), const((HEAD_GROUP, HEAD_GROUP)),
        const((D_MODEL, D_MODEL)),
    ]
    return pl.pallas_call(
        functools.partial(_mixout_kernel, n_blk=n_blk),
        out_shape=jax.ShapeDtypeStruct((batch, seq, D_MODEL), F32),
        grid=(batch, n_blk),
        in_specs=in_specs,
        out_specs=tok(D_MODEL),
        compiler_params=_cparams(("parallel", "arbitrary")),
        name="mixout",
    )(x1, *os, *lses, y, bonus, z, z, z, mu_prev_g, mu_next_g, g2, ln_w, ln_b, avg_bd, w_out)


def _block_diag(width, block, value):
    idx = jnp.arange(width) // block
    return jnp.where(idx[:, None] == idx[None, :], value, 0.0).astype(BF16)


def _trunk(x, p):
    batch, seq, _ = x.shape
    xf = x.reshape(batch * seq, D_MODEL)
    x1 = _ffn(xf, p["ffn1_norm"], p["ffn1_gate"], p["ffn1_up"], p["ffn1_down"], p["final_norm"],
              final_norm=False)
    qkv = _inproj(x1, p["mix_norm"], p["w_in_att"], BF16, tn=768).reshape(batch, seq, N_ATT_COLS)
    z = _inproj(x1, p["mix_norm"], p["w_in_rwkv"], F32, tn=1152).reshape(batch, seq, N_RWKV_COLS)
    os, lses = [], []
    for dil in DILATIONS:
        o, lse = _attention(qkv, p["slopes"], batch, seq, dil)
        os.append(o)
        lses.append(lse)
    y, bonus = _rwkv(z, batch, seq, p["mu_prev"], p["mu_next"], p["w0"], p["w2"], p["a0"], p["a2"],
                     p["k_k"], p["k_a"], p["r_k"], p["ones_bd"])
    x2 = _mixout(x1.reshape(batch, seq, D_MODEL), os, lses, y, bonus, z, batch, seq,
                 p["mu_prev_g"], p["mu_next_g"], p["g2"], p["ln_x_w"], p["ln_x_b"], p["avg_bd"], p["w_out"])
    out = _ffn(x2.reshape(batch * seq, D_MODEL), p["ffn2_norm"], p["ffn2_gate"], p["ffn2_up"],
               p["ffn2_down"], p["final_norm"], final_norm=True)
    return out.reshape(batch, seq, D_MODEL)


def kernel(x_prompt, x_sample, ffn1_norm, ffn1_gate, ffn1_up, ffn1_down, mix_norm, w_in, w_out, mu_prev, mu_next, w0_f, w2_f, w0_b, w2_b, a0_f, a2_f, a0_b, a2_b, g2, k_k, k_a, r_k, ln_x_w, ln_x_b, ffn2_norm, ffn2_gate, ffn2_up, ffn2_down, final_norm):
    assert ffn1_norm.shape[0] == 1, "single layer"
    row = lambda t: t.reshape(1, -1)
    zero64 = jnp.zeros((64, RWKV_WIDTH), F32)
    w_in_b = w_in[0].astype(BF16)
    p = {
        "ffn1_norm": row(ffn1_norm[0]), "ffn1_gate": ffn1_gate[0].astype(BF16),
        "ffn1_up": ffn1_up[0].astype(BF16), "ffn1_down": ffn1_down[0].astype(BF16),
        "mix_norm": row(mix_norm[0]),
        "w_in_att": w_in_b[:, :N_ATT_COLS], "w_in_rwkv": w_in_b[:, N_ATT_COLS:],
        "w_out": w_out[0].astype(BF16),
        "mu_prev": row(mu_prev[0]), "mu_next": row(mu_next[0]),
        "mu_prev_g": row(mu_prev[0, GD_COL:]), "mu_next_g": row(mu_next[0, GD_COL:]),
        "w0": jnp.stack([row(w0_f[0]), row(w0_b[0])]),
        "w2": jnp.stack([jnp.concatenate([w2_f[0], zero64]), jnp.concatenate([zero64, w2_b[0]])]).astype(BF16),
        "a0": jnp.stack([row(a0_f[0]), row(a0_b[0])]),
        "a2": jnp.stack([jnp.concatenate([a2_f[0], zero64]), jnp.concatenate([zero64, a2_b[0]])]).astype(BF16),
        "g2": g2[0].astype(BF16),
        "k_k": row(k_k[0]), "k_a": row(k_a[0]), "r_k": row(r_k[0]),
        "ln_x_w": row(ln_x_w[0]), "ln_x_b": row(ln_x_b[0]),
        "ffn2_norm": row(ffn2_norm[0]), "ffn2_gate": ffn2_gate[0].astype(BF16),
        "ffn2_up": ffn2_up[0].astype(BF16), "ffn2_down": ffn2_down[0].astype(BF16),
        "final_norm": row(final_norm),
        "slopes": jnp.exp2(-8.0 * jnp.arange(1, ATT_HEADS + 1, dtype=F32) / ATT_HEADS),
        "ones_bd": _block_diag(HEAD_GROUP, HEAD_DIM, 1.0),
        "avg_bd": _block_diag(HEAD_GROUP, HEAD_DIM, 1.0 / HEAD_DIM),
    }
    return (_trunk(x_prompt, p), _trunk(x_sample, p))
```

```python
import functools

import jax
import jax.numpy as jnp
from jax import lax
from jax.experimental import pallas as pl
from jax.experimental.pallas import tpu as pltpu

F32 = jnp.float32
BF16 = jnp.bfloat16

D_MODEL = 2048
D_FF = 5632
HEAD_DIM = 64
ATT_WIDTH = 1024
ATT_HEADS = 16
RWKV_WIDTH = 1024
N_ATT_COLS = 3 * ATT_WIDTH
N_RWKV_COLS = 3 * RWKV_WIDTH + 64 * 4 + 128
WD_COL = 3 * RWKV_WIDTH
AD_COL = WD_COL + 128
GD_COL = AD_COL + 128
DILATIONS = (1, 4, 16)
BAND_HALF = 64
NORM_EPS = 1e-6
RWKV_LN_EPS = 64e-5
NEG_INF = -1e30

LANES = 128
HEAD_GROUP = 256
CHUNK = 64
VMEM_LIMIT = 56 * 1024 * 1024


def _cparams(sem):
    return pltpu.CompilerParams(dimension_semantics=sem, vmem_limit_bytes=VMEM_LIMIT)


def _dot(a, b):
    return jnp.dot(a.astype(BF16), b.astype(BF16), preferred_element_type=F32)


def _dot_nt(a, b):
    return lax.dot_general(a.astype(BF16), b.astype(BF16), (((1,), (1,)), ((), ())),
                           preferred_element_type=F32)


def _dot_tn(a, b):
    return lax.dot_general(a.astype(BF16), b.astype(BF16), (((0,), (0,)), ((), ())),
                           preferred_element_type=F32)


def _split2(x):
    hi = x.astype(BF16)
    lo = (x - hi.astype(F32)).astype(BF16)
    return hi, lo


def _head_sums(x, seg):
    hi, lo = _split2(x)
    outs = []
    for g in range(x.shape[1] // HEAD_GROUP):
        cols = slice(g * HEAD_GROUP, (g + 1) * HEAD_GROUP)
        outs.append(jnp.dot(hi[:, cols], seg, preferred_element_type=F32)
                    + jnp.dot(lo[:, cols], seg, preferred_element_type=F32))
    return jnp.concatenate(outs, axis=1)


def _cumsum_dot(tri_bf16, x):
    hi, lo = _split2(x)
    return (jnp.dot(tri_bf16, hi, preferred_element_type=F32)
            + jnp.dot(tri_bf16, lo, preferred_element_type=F32))


def _rmsnorm(x, g):
    return x * lax.rsqrt(jnp.mean(x * x, axis=-1, keepdims=True) + NORM_EPS) * g


def _ffn_kernel(x_ref, g_ref, wg_ref, wu_ref, wd_ref, fg_ref, o_ref, n_scr, *, final_norm):
    f = pl.program_id(1)

    @pl.when(f == 0)
    def _():
        x = x_ref[...]
        n_scr[...] = _rmsnorm(x, g_ref[...]).astype(BF16)
        o_ref[...] = x

    n = n_scr[...]
    hg = jnp.dot(n, wg_ref[...], preferred_element_type=F32)
    hu = jnp.dot(n, wu_ref[...], preferred_element_type=F32)
    h = (hg * jax.nn.sigmoid(hg) * (0.5 * hu)).astype(BF16)
    o_ref[...] += jnp.dot(h, wd_ref[...], preferred_element_type=F32)

    if final_norm:
        @pl.when(f == pl.num_programs(1) - 1)
        def _():
            o_ref[...] = _rmsnorm(o_ref[...], fg_ref[...])


def _ffn(x, g, wg, wu, wd, fg, *, final_norm, tm=1024, tf=512):
    n_tok = x.shape[0]
    return pl.pallas_call(
        functools.partial(_ffn_kernel, final_norm=final_norm),
        out_shape=jax.ShapeDtypeStruct((n_tok, D_MODEL), F32),
        grid=(n_tok // tm, D_FF // tf),
        in_specs=[
            pl.BlockSpec((tm, D_MODEL), lambda i, f: (i, 0)),
            pl.BlockSpec((1, D_MODEL), lambda i, f: (0, 0)),
            pl.BlockSpec((D_MODEL, tf), lambda i, f: (0, f)),
            pl.BlockSpec((D_MODEL, tf), lambda i, f: (0, f)),
            pl.BlockSpec((tf, D_MODEL), lambda i, f: (f, 0)),
            pl.BlockSpec((1, D_MODEL), lambda i, f: (0, 0)),
        ],
        out_specs=pl.BlockSpec((tm, D_MODEL), lambda i, f: (i, 0)),
        scratch_shapes=[pltpu.VMEM((tm, D_MODEL), BF16)],
        compiler_params=_cparams(("parallel", "arbitrary")),
        name="ffn_final" if final_norm else "ffn",
    )(x, g, wg, wu, wd, fg)


def _inproj_kernel(x_ref, g_ref, w_ref, o_ref, n_scr):
    @pl.when(pl.program_id(1) == 0)
    def _():
        n_scr[...] = _rmsnorm(x_ref[...], g_ref[...]).astype(BF16)

    o_ref[...] = jnp.dot(n_scr[...], w_ref[...], preferred_element_type=F32).astype(o_ref.dtype)


def _inproj(x, g, w, out_dtype, *, tn, tm=1024):
    n_tok = x.shape[0]
    n_cols = w.shape[1]
    return pl.pallas_call(
        _inproj_kernel,
        out_shape=jax.ShapeDtypeStruct((n_tok, n_cols), out_dtype),
        grid=(n_tok // tm, n_cols // tn),
        in_specs=[
            pl.BlockSpec((tm, D_MODEL), lambda i, j: (i, 0)),
            pl.BlockSpec((1, D_MODEL), lambda i, j: (0, 0)),
            pl.BlockSpec((D_MODEL, tn), lambda i, j: (0, j)),
        ],
        out_specs=pl.BlockSpec((tm, tn), lambda i, j: (i, j)),
        scratch_shapes=[pltpu.VMEM((tm, D_MODEL), BF16)],
        compiler_params=_cparams(("parallel", "arbitrary")),
        name="inproj",
    )(x, g, w)


ATT_TQ = 128
ATT_TK = 64


def _attn_kernel(slopes_ref, q_ref, k0, k1, k2, k3, v0, v1, v2, v3, o_ref, lse_ref, kc_scr, vc_scr,
                 *, seq_len, dil):
    i = pl.program_id(2)
    for j, (kr, vr) in enumerate(((k0, v0), (k1, v1), (k2, v2), (k3, v3))):
        kc_scr[j * ATT_TK:(j + 1) * ATT_TK, :] = kr[...]
        vc_scr[j * ATT_TK:(j + 1) * ATT_TK, :] = vr[...]

    nk = 4 * ATT_TK
    row = lax.broadcasted_iota(jnp.int32, (ATT_TQ, nk), 0)
    col = lax.broadcasted_iota(jnp.int32, (ATT_TQ, nk), 1)
    rel = col - BAND_HALF - row
    kpos = i * ATT_TQ - BAND_HALF + col
    valid = (jnp.abs(rel) <= BAND_HALF) & (kpos >= 0) & (kpos < seq_len)
    dist = (jnp.abs(rel) * dil).astype(F32)
    lane = lax.broadcasted_iota(jnp.int32, (ATT_TQ, LANES), 1)
    first = lane < HEAD_DIM

    heads = range(ATT_HEADS)
    cols = [slice((h // 2) * LANES, (h // 2 + 1) * LANES) for h in heads]
    sel = [first if h % 2 == 0 else jnp.logical_not(first) for h in heads]
    qm = [jnp.where(sel[h], q_ref[:, cols[h]], jnp.zeros((ATT_TQ, LANES), BF16)) for h in heads]
    s = [_dot_nt(qm[h], kc_scr[:, cols[h]]) * (HEAD_DIM ** -0.5) - slopes_ref[h] * dist for h in heads]
    s = [jnp.where(valid, x, NEG_INF) for x in s]
    m = [jnp.max(x, axis=-1, keepdims=True) for x in s]
    p = [jnp.exp(x - mx) for x, mx in zip(s, m)]
    l = [jnp.sum(x, axis=-1, keepdims=True) for x in p]
    o = [jnp.dot(p[h].astype(BF16), vc_scr[:, cols[h]], preferred_element_type=F32) / l[h] for h in heads]
    lse = [mx + jnp.log(lx) for mx, lx in zip(m, l)]
    for h in range(0, ATT_HEADS, 2):
        o_ref[:, cols[h]] = jnp.where(first, o[h], o[h + 1]).astype(o_ref.dtype)
    lse_tile = jnp.zeros((ATT_TQ, LANES), F32)
    for h in heads:
        lse_tile = jnp.where(lane == h, lse[h], lse_tile)
    lse_ref[...] = lse_tile


def _attention(qkv, slopes, batch, seq, dil):
    L = seq // dil
    view = qkv.reshape(batch, L, dil * N_ATT_COLS)
    nkb = L // ATT_TK
    ratio = ATT_TQ // ATT_TK

    def kv_spec(which, j):
        def imap(b, r, i, slopes_ref):
            blk = jnp.clip(i * ratio - 1 + j, 0, nkb - 1)
            return (b, blk, 3 * r + which)
        return pl.BlockSpec((None, ATT_TK, ATT_WIDTH), imap)

    grid_spec = pltpu.PrefetchScalarGridSpec(
        num_scalar_prefetch=1,
        grid=(batch, dil, L // ATT_TQ),
        in_specs=[pl.BlockSpec((None, ATT_TQ, ATT_WIDTH), lambda b, r, i, s: (b, i, 3 * r))]
        + [kv_spec(1, j) for j in range(4)] + [kv_spec(2, j) for j in range(4)],
        out_specs=[pl.BlockSpec((None, ATT_TQ, ATT_WIDTH), lambda b, r, i, s: (b, i, r)),
                   pl.BlockSpec((None, None, ATT_TQ, LANES), lambda b, r, i, s: (b, r, i, 0))],
        scratch_shapes=[pltpu.VMEM((4 * ATT_TK, ATT_WIDTH), BF16),
                        pltpu.VMEM((4 * ATT_TK, ATT_WIDTH), BF16)],
    )
    o, lse = pl.pallas_call(
        functools.partial(_attn_kernel, seq_len=L, dil=dil),
        out_shape=[jax.ShapeDtypeStruct((batch, L, dil * ATT_WIDTH), BF16),
                   jax.ShapeDtypeStruct((batch, dil, L, LANES), F32)],
        grid_spec=grid_spec,
        compiler_params=_cparams(("parallel", "parallel", "arbitrary")),
        name=f"attn_d{dil}",
    )(slopes, view, *([view] * 8))
    return o.reshape(batch, seq, ATT_WIDTH), lse.transpose(0, 2, 1, 3).reshape(batch, seq, LANES)


RWKV_TB = 256
PAIRS = RWKV_WIDTH // LANES
SOLVE_GROUP = 8


def _shifted(z, prev_row, next_row, mu_prev, mu_next):
    n = z.shape[0]
    ridx = lax.broadcasted_iota(jnp.int32, z.shape, 0)
    z_prev = jnp.where(ridx == 0, prev_row, pltpu.roll(z, 1, 0))
    z_next = jnp.where(ridx == n - 1, next_row, pltpu.roll(z, n - 1, 0))
    return z + mu_prev * (z_prev - z) + mu_next * (z_next - z)


def _halo_rows(zp_ref, zn_ref, t_blk, n_blk):
    prev_row = jnp.where(t_blk == 0, 0.0, zp_ref[7:8, :])
    next_row = jnp.where(t_blk == n_blk - 1, 0.0, zn_ref[0:1, :])
    return prev_row, next_row


def _stack_pair(x, first):
    zero = jnp.zeros_like(x)
    return jnp.concatenate([jnp.where(first, x, zero), jnp.where(first, zero, x)], axis=0)


def _rwkv_kernel(z_ref, zp_ref, zn_ref, mup_ref, mun_ref, w0_ref, w2_ref, a0_ref, a2_ref,
                 kk_ref, ka_ref, rk_ref, ones_ref, y_ref, bonus_ref,
                 at_s, bt_s, kt_s, rt_s, bd_s, kd_s, v_s, pc_s, st_s, *, n_blk):
    d = pl.program_id(1)
    step = pl.program_id(2)
    nc = RWKV_TB // CHUNK
    fwd = d == 0
    sgn = jnp.where(fwd, 1, -1)
    prep_step = jnp.minimum(step, n_blk - 1)
    t_blk = jnp.where(fwd, prep_step, n_blk - 1 - prep_step)
    wr = step % 2
    rd = 1 - wr

    @pl.when(step == 0)
    def _():
        st_s[...] = jnp.zeros_like(st_s)
        for ref in (at_s, bt_s, kt_s, rt_s, bd_s, kd_s, v_s, pc_s):
            ref[1] = jnp.zeros(ref.shape[1:], ref.dtype)

    halo_top = jnp.where(t_blk == 0, 0.0, zp_ref[...])
    halo_bot = jnp.where(t_blk == n_blk - 1, 0.0, zn_ref[...])
    mu_p = mup_ref[...]
    mu_n = mun_ref[...]
    mu_c = 1.0 - mu_p - mu_n
    ti = lax.broadcasted_iota(jnp.int32, (CHUNK, CHUNK), 0)
    si = lax.broadcasted_iota(jnp.int32, (CHUNK, CHUNK), 1)
    tri = jnp.where((ti - si) * sgn >= 0, 1.0, 0.0).astype(BF16)

    def prepare(cp):
        r0 = pl.multiple_of(cp * CHUNK, CHUNK)
        rows = pl.ds(r0, CHUNK)
        top_rows = pl.ds(pl.multiple_of(jnp.maximum(r0 - 8, 0), 8), 8)
        bot_rows = pl.ds(pl.multiple_of(jnp.minimum(r0 + CHUNK, RWKV_TB - 8), 8), 8)

        def shifted(cols):
            top = jnp.where(cp == 0, halo_top[:, cols], z_ref[top_rows, cols])
            bot = jnp.where(cp == nc - 1, halo_bot[:, cols], z_ref[bot_rows, cols])
            zc = z_ref[rows, cols]
            win = jnp.concatenate([top, zc, bot], axis=0)
            z_prev = pltpu.roll(win, 1, 0)[8:8 + CHUNK]
            z_next = pltpu.roll(win, CHUNK + 15, 0)[8:8 + CHUNK]
            return zc * mu_c[:, cols] + mu_p[:, cols] * z_prev + mu_n[:, cols] * z_next

        r = shifted(slice(0, RWKV_WIDTH))
        yield
        k = shifted(slice(RWKV_WIDTH, 2 * RWKV_WIDTH))
        yield
        v = shifted(slice(2 * RWKV_WIDTH, 3 * RWKV_WIDTH))
        v_s[wr, rows, :] = v.astype(BF16)
        yield
        low = shifted(slice(WD_COL, GD_COL))
        wd = low[:, 0:128]
        ad = low[:, 128:256]
        u = -(w0_ref[...] + _dot(jnp.tanh(wd), w2_ref[...]))
        softplus = jnp.maximum(u, 0.0) + jnp.log(1.0 + jnp.exp(-jnp.abs(u)))
        lw = -jnp.exp(-softplus - 0.5)
        yield
        iclr = jax.nn.sigmoid(a0_ref[...] + _dot(ad, a2_ref[...]))
        ones = ones_ref[...]
        kkr = k * kk_ref[...]
        kk = kkr / jnp.maximum(jnp.sqrt(_head_sums(kkr * kkr, ones)), 1e-12)
        yield
        kdir = k * (1.0 + (iclr - 1.0) * ka_ref[...])
        bonus_ref[rows, :] = _head_sums(r * kdir * rk_ref[...], ones) * v
        yield
        b = kk * iclr
        lp_incl = _cumsum_dot(tri, lw)
        total = jnp.sum(lw, axis=0, keepdims=True)
        pc_s[wr, cp] = jnp.broadcast_to(jnp.exp(total), (8, RWKV_WIDTH))
        yield
        e_neg = jnp.exp(-lp_incl)
        at_s[wr, rows, :] = (-kk * jnp.exp(lp_incl - lw)).astype(BF16)
        bt_s[wr, rows, :] = (b * e_neg).astype(BF16)
        yield
        kt_s[wr, rows, :] = (kdir * e_neg).astype(BF16)
        rt_s[wr, rows, :] = r * jnp.exp(lp_incl)
        yield
        e_rest = jnp.exp(total - lp_incl)
        bd_s[wr, rows, :] = (b * e_rest).astype(BF16)
        kd_s[wr, rows, :] = (kdir * e_rest).astype(BF16)

    n2 = 2 * CHUNK
    rr = lax.broadcasted_iota(jnp.int32, (n2, n2), 0)
    cc = lax.broadcasted_iota(jnp.int32, (n2, n2), 1)
    same_head = (rr // CHUNK) == (cc // CHUNK)
    tt = rr % CHUNK
    ss = cc % CHUNK
    strict = same_head & ((tt - ss) * sgn > 0)
    incl = same_head & ((tt - ss) * sgn >= 0)
    incl2 = jnp.concatenate([incl, incl], axis=1)
    blk16 = strict & ((tt // 16) == (ss // 16))
    off32 = strict & ((tt // 32) == (ss // 32)) & ((tt // 16) != (ss // 16))
    off64 = strict & ((tt // 32) != (ss // 32))
    eye = rr == cc
    eye_f = jnp.where(eye, 1.0, 0.0)
    lane = lax.broadcasted_iota(jnp.int32, (CHUNK, LANES), 1)
    first = lane < HEAD_DIM

    def solve(c, pairs, tick):
        rows = pl.ds(pl.multiple_of(c * CHUNK, CHUNK), CHUNK)
        lanes = [slice(p * LANES, (p + 1) * LANES) for p in pairs]
        each = lambda fn, *lists: [fn(*xs) for xs in zip(*lists)]
        load = lambda ref: [_stack_pair(ref[rd, rows, cols], first) for cols in lanes]
        at, bt, kt, rt, bd, kd, vv = (load(ref) for ref in (at_s, bt_s, kt_s, rt_s, bd_s, kd_s, v_s))
        tick()

        prod = each(lambda a_, r_, b_, k_: _dot_nt(jnp.concatenate([a_, r_.astype(BF16)], axis=0),
                                                   jnp.concatenate([b_, k_], axis=0)), at, rt, bt, kt)
        a_ab = [jnp.where(strict, x[:n2, :n2], 0.0) for x in prod]
        a_ak = [jnp.where(strict, x[:n2, n2:], 0.0) for x in prod]
        q_bk = [jnp.where(incl2, x[n2:, :], 0.0) for x in prod]
        tick()

        d0 = [jnp.where(blk16, x, 0.0) for x in a_ab]
        t_inv = [eye_f + x for x in d0]
        pw = [_dot(x, x) for x in d0]
        tick()
        for _ in range(2):
            both = each(lambda t_, p_: _dot(jnp.concatenate([t_, p_], axis=0), p_), t_inv, pw)
            t_inv = each(lambda t_, b_: t_ + b_[:n2], t_inv, both)
            pw = [x[n2:] for x in both]
            tick()
        t_inv = each(lambda t_, p_: t_ + _dot(t_, p_), t_inv, pw)
        tick()
        for off_mask in (off32, off64):
            inner = each(lambda a_, t_: _dot(jnp.where(off_mask, a_, 0.0), t_), a_ab, t_inv)
            t_inv = each(lambda t_, i_: t_ + _dot(t_, i_), t_inv, inner)
            tick()

        av = each(_dot, a_ak, vv)
        tick()
        wu = each(lambda t_, a_, v_: _dot(t_, jnp.concatenate([a_, v_.astype(BF16)], axis=1)),
                  t_inv, at, av)
        tick()
        zero = jnp.zeros((n2, LANES), BF16)
        rhs = each(lambda w_, v_: jnp.concatenate(
            [w_.astype(BF16), jnp.concatenate([zero, v_], axis=1)], axis=0), wu, vv)
        ry = each(_dot, q_bk, rhs)
        gh = each(lambda b_, k_, x_: _dot_tn(jnp.concatenate([b_, k_], axis=0), x_),
                  bd, kd, rhs)
        tick()

        st = [st_s[p] for p in pairs]
        y_st = each(lambda r_, x_, s_: _dot(r_ + x_[:, :LANES], s_) + x_[:, LANES:], rt, ry, st)
        gs = each(lambda g_, s_: _dot(g_[:, :LANES], s_), gh, st)
        pc_row = pc_s[rd, c]
        for i, p in enumerate(pairs):
            y_ref[rows, lanes[i]] = y_st[i][:CHUNK] + y_st[i][CHUNK:]
            pc_col = jnp.sum(jnp.where(eye, pc_row[0:1, lanes[i]], 0.0), axis=1, keepdims=True)
            st_s[p] = pc_col * st[i] + gs[i] + gh[i][:, LANES:]

    def chunk_body(ci, carry):
        stages = prepare(ci)
        tick = lambda: next(stages, None)
        c = jnp.where(fwd, ci, nc - 1 - ci)
        for g in range(0, PAIRS, SOLVE_GROUP):
            solve(c, range(g, g + SOLVE_GROUP), tick)
        for _ in stages:
            pass
        return carry

    lax.fori_loop(0, nc, chunk_body, 0)


def _rwkv(z, batch, seq, mu_prev, mu_next, w0, w2, a0, a2, k_k, k_a, r_k, ones_bd):
    n_blk = seq // RWKV_TB
    hb = RWKV_TB // 8

    def time_block(d, i):
        return jnp.where(d == 0, i, n_blk - 1 - i)

    def tmap(d, i):
        return time_block(d, jnp.minimum(i, n_blk - 1))

    def smap(d, i):
        return time_block(d, jnp.maximum(i - 1, 0))

    vec = lambda: pl.BlockSpec((1, RWKV_WIDTH), lambda b, d, i: (0, 0))
    zvec = lambda: pl.BlockSpec((1, N_RWKV_COLS), lambda b, d, i: (0, 0))
    in_specs = [
        pl.BlockSpec((None, RWKV_TB, N_RWKV_COLS), lambda b, d, i: (b, tmap(d, i), 0)),
        pl.BlockSpec((None, 8, N_RWKV_COLS), lambda b, d, i: (b, jnp.maximum(tmap(d, i) * hb - 1, 0), 0)),
        pl.BlockSpec((None, 8, N_RWKV_COLS),
                     lambda b, d, i: (b, jnp.minimum((tmap(d, i) + 1) * hb, seq // 8 - 1), 0)),
        zvec(), zvec(),
        pl.BlockSpec((None, 1, RWKV_WIDTH), lambda b, d, i: (d, 0, 0)),
        pl.BlockSpec((None, 128, RWKV_WIDTH), lambda b, d, i: (d, 0, 0)),
        pl.BlockSpec((None, 1, RWKV_WIDTH), lambda b, d, i: (d, 0, 0)),
        pl.BlockSpec((None, 128, RWKV_WIDTH), lambda b, d, i: (d, 0, 0)),
        vec(), vec(), vec(),
        pl.BlockSpec((HEAD_GROUP, HEAD_GROUP), lambda b, d, i: (0, 0)),
    ]
    out_spec = lambda m: pl.BlockSpec((None, None, RWKV_TB, RWKV_WIDTH), lambda b, d, i: (d, b, m(d, i), 0))
    tok = lambda dt: pltpu.VMEM((2, RWKV_TB, RWKV_WIDTH), dt)
    return pl.pallas_call(
        functools.partial(_rwkv_kernel, n_blk=n_blk),
        out_shape=[jax.ShapeDtypeStruct((2, batch, seq, RWKV_WIDTH), F32)] * 2,
        grid=(batch, 2, n_blk + 1),
        in_specs=in_specs,
        out_specs=[out_spec(smap), out_spec(tmap)],
        scratch_shapes=[tok(BF16), tok(BF16), tok(BF16), tok(F32), tok(BF16), tok(BF16), tok(BF16),
                        pltpu.VMEM((2, RWKV_TB // CHUNK, 8, RWKV_WIDTH), F32),
                        pltpu.VMEM((PAIRS, LANES, LANES), F32)],
        compiler_params=_cparams(("parallel", "parallel", "arbitrary")),
        name="rwkv",
    )(z, z, z, mu_prev, mu_next, w0, w2, a0, a2, k_k, k_a, r_k, ones_bd)


MIX_TM = 256


def _mixout_kernel(x_ref, o1, o2, o3, l1, l2, l3, y_ref, bonus_ref, gd_ref, gdp_ref, gdn_ref,
                   mup_ref, mun_ref, g2_ref, lnw_ref, lnb_ref, avg_ref, spread_ref, wo_ref, out_ref, *, n_blk):
    i = pl.program_id(1)
    m = jnp.maximum(jnp.maximum(l1[...], l2[...]), l3[...])
    e = [jnp.exp(l[...] - m) for l in (l1, l2, l3)]
    inv = 1.0 / (e[0] + e[1] + e[2])
    spread = spread_ref[...]
    att = jnp.zeros((MIX_TM, ATT_WIDTH), F32)
    for ep, op in zip(e, (o1, o2, o3)):
        hi, lo = _split2(ep * inv)
        wts = jnp.dot(hi, spread, preferred_element_type=F32) + jnp.dot(lo, spread, preferred_element_type=F32)
        att = att + wts * op[...].astype(F32)

    y = y_ref[0] + y_ref[1]
    avg = avg_ref[...]
    yc = y - _head_sums(y, avg)
    var = _head_sums(yc * yc, avg)
    yn = yc * lax.rsqrt(var + RWKV_LN_EPS) * lnw_ref[...] + lnb_ref[...]
    prev_row, next_row = _halo_rows(gdp_ref, gdn_ref, i, n_blk)
    gd = _shifted(gd_ref[...], prev_row, next_row, mup_ref[...], mun_ref[...])
    gate = _dot(jax.nn.sigmoid(gd), g2_ref[...])
    rw = (yn + bonus_ref[0] + bonus_ref[1]) * gate

    mixed = jnp.dot(att.astype(BF16), wo_ref[0:ATT_WIDTH, :], preferred_element_type=F32)
    mixed = mixed + jnp.dot(rw.astype(BF16), wo_ref[ATT_WIDTH:, :], preferred_element_type=F32)
    out_ref[...] = x_ref[...] + mixed


def _mixout(x1, os, lses, y, bonus, z, batch, seq, mu_prev_g, mu_next_g, g2, ln_w, ln_b, avg_bd, spread, w_out):
    n_blk = seq // MIX_TM
    hb = MIX_TM // 8
    gcol = GD_COL // 128
    tok = lambda w: pl.BlockSpec((None, MIX_TM, w), lambda b, i: (b, i, 0))
    const = lambda shape: pl.BlockSpec(shape, lambda b, i: (0,) * len(shape))
    dir_spec = lambda: pl.BlockSpec((2, None, MIX_TM, RWKV_WIDTH), lambda b, i: (0, b, i, 0))
    in_specs = [tok(D_MODEL)] + [tok(ATT_WIDTH)] * 3 + [tok(LANES)] * 3 + [dir_spec(), dir_spec()] + [
        pl.BlockSpec((None, MIX_TM, 128), lambda b, i: (b, i, gcol)),
        pl.BlockSpec((None, 8, 128), lambda b, i: (b, jnp.maximum(i * hb - 1, 0), gcol)),
        pl.BlockSpec((None, 8, 128), lambda b, i: (b, jnp.minimum((i + 1) * hb, seq // 8 - 1), gcol)),
        const((1, 128)), const((1, 128)), const((128, RWKV_WIDTH)),
        const((1, RWKV_WIDTH)), const((1, RWKV_WIDTH)), const((HEAD_GROUP, HEAD_GROUP)),
        const((LANES, ATT_WIDTH)), const((D_MODEL, D_MODEL)),
    ]
    return pl.pallas_call(
        functools.partial(_mixout_kernel, n_blk=n_blk),
        out_shape=jax.ShapeDtypeStruct((batch, seq, D_MODEL), F32),
        grid=(batch, n_blk),
        in_specs=in_specs,
        out_specs=tok(D_MODEL),
        compiler_params=_cparams(("parallel", "arbitrary")),
        name="mixout",
    )(x1, *os, *lses, y, bonus, z, z, z, mu_prev_g, mu_next_g, g2, ln_w, ln_b, avg_bd, spread, w_out)


def _block_diag(width, block, value):
    idx = jnp.arange(width) // block
    return jnp.where(idx[:, None] == idx[None, :], value, 0.0).astype(BF16)


def _trunk(x, p):
    batch, seq, _ = x.shape
    xf = x.reshape(batch * seq, D_MODEL)
    x1 = _ffn(xf, p["ffn1_norm"], p["ffn1_gate"], p["ffn1_up"], p["ffn1_down"], p["final_norm"],
              final_norm=False)
    qkv = _inproj(x1, p["mix_norm"], p["w_in_att"], BF16, tn=768).reshape(batch, seq, N_ATT_COLS)
    z = _inproj(x1, p["mix_norm"], p["w_in_rwkv"], F32, tn=1152).reshape(batch, seq, N_RWKV_COLS)
    os, lses = [], []
    for dil in DILATIONS:
        o, lse = _attention(qkv, p["slopes"], batch, seq, dil)
        os.append(o)
        lses.append(lse)
    y, bonus = _rwkv(z, batch, seq, p["mu_prev"], p["mu_next"], p["w0"], p["w2"], p["a0"], p["a2"],
                     p["k_k"], p["k_a"], p["r_k"], p["ones_bd"])
    x2 = _mixout(x1.reshape(batch, seq, D_MODEL), os, lses, y, bonus, z, batch, seq,
                 p["mu_prev_g"], p["mu_next_g"], p["g2"], p["ln_x_w"], p["ln_x_b"], p["avg_bd"], p["spread"], p["w_out"])
    out = _ffn(x2.reshape(batch * seq, D_MODEL), p["ffn2_norm"], p["ffn2_gate"], p["ffn2_up"],
               p["ffn2_down"], p["final_norm"], final_norm=True)
    return out.reshape(batch, seq, D_MODEL)


def kernel(x_prompt, x_sample, ffn1_norm, ffn1_gate, ffn1_up, ffn1_down, mix_norm, w_in, w_out, mu_prev, mu_next, w0_f, w2_f, w0_b, w2_b, a0_f, a2_f, a0_b, a2_b, g2, k_k, k_a, r_k, ln_x_w, ln_x_b, ffn2_norm, ffn2_gate, ffn2_up, ffn2_down, final_norm):
    assert ffn1_norm.shape[0] == 1, "single layer"
    row = lambda t: t.reshape(1, -1)
    zero64 = jnp.zeros((64, RWKV_WIDTH), F32)
    w_in_b = w_in[0].astype(BF16)
    p = {
        "ffn1_norm": row(ffn1_norm[0]), "ffn1_gate": ffn1_gate[0].astype(BF16),
        "ffn1_up": ffn1_up[0].astype(BF16), "ffn1_down": ffn1_down[0].astype(BF16),
        "mix_norm": row(mix_norm[0]),
        "w_in_att": w_in_b[:, :N_ATT_COLS], "w_in_rwkv": w_in_b[:, N_ATT_COLS:],
        "w_out": w_out[0].astype(BF16),
        "mu_prev": row(mu_prev[0]), "mu_next": row(mu_next[0]),
        "mu_prev_g": row(mu_prev[0, GD_COL:]), "mu_next_g": row(mu_next[0, GD_COL:]),
        "w0": jnp.stack([row(w0_f[0]), row(w0_b[0])]),
        "w2": jnp.stack([jnp.concatenate([w2_f[0], zero64]), jnp.concatenate([zero64, w2_b[0]])]).astype(BF16),
        "a0": jnp.stack([row(a0_f[0]), row(a0_b[0])]),
        "a2": jnp.stack([jnp.concatenate([a2_f[0], zero64]), jnp.concatenate([zero64, a2_b[0]])]).astype(BF16),
        "g2": g2[0].astype(BF16),
        "k_k": row(k_k[0]), "k_a": row(k_a[0]), "r_k": row(r_k[0]),
        "ln_x_w": row(ln_x_w[0]), "ln_x_b": row(ln_x_b[0]),
        "ffn2_norm": row(ffn2_norm[0]), "ffn2_gate": ffn2_gate[0].astype(BF16),
        "ffn2_up": ffn2_up[0].astype(BF16), "ffn2_down": ffn2_down[0].astype(BF16),
        "final_norm": row(final_norm),
        "slopes": jnp.exp2(-8.0 * jnp.arange(1, ATT_HEADS + 1, dtype=F32) / ATT_HEADS),
        "ones_bd": _block_diag(HEAD_GROUP, HEAD_DIM, 1.0),
        "avg_bd": _block_diag(HEAD_GROUP, HEAD_DIM, 1.0 / HEAD_DIM),
        "spread": (jnp.arange(LANES)[:, None] == jnp.arange(ATT_WIDTH)[None, :] // HEAD_DIM).astype(BF16),
    }
    return (_trunk(x_prompt, p), _trunk(x_sample, p))
```

```python
import functools
import itertools

import jax
import jax.numpy as jnp
from jax import lax
from jax.experimental import pallas as pl
from jax.experimental.pallas import tpu as pltpu

F32 = jnp.float32
BF16 = jnp.bfloat16

D_MODEL = 2048
D_FF = 5632
HEAD_DIM = 64
ATT_WIDTH = 1024
ATT_HEADS = 16
RWKV_WIDTH = 1024
N_ATT_COLS = 3 * ATT_WIDTH
N_RWKV_COLS = 3 * RWKV_WIDTH + 64 * 4 + 128
WD_COL = 3 * RWKV_WIDTH
AD_COL = WD_COL + 128
GD_COL = AD_COL + 128
DILATIONS = (1, 4, 16)
BAND_HALF = 64
NORM_EPS = 1e-6
RWKV_LN_EPS = 64e-5
NEG_INF = -1e30

LANES = 128
HEAD_GROUP = 256
CHUNK = 64
VMEM_LIMIT = 56 * 1024 * 1024


def _cparams(sem):
    return pltpu.CompilerParams(dimension_semantics=sem, vmem_limit_bytes=VMEM_LIMIT)


def _dot(a, b):
    return jnp.dot(a.astype(BF16), b.astype(BF16), preferred_element_type=F32)


def _dot_nt(a, b):
    return lax.dot_general(a.astype(BF16), b.astype(BF16), (((1,), (1,)), ((), ())),
                           preferred_element_type=F32)


def _dot_tn(a, b):
    return lax.dot_general(a.astype(BF16), b.astype(BF16), (((0,), (0,)), ((), ())),
                           preferred_element_type=F32)


def _split2(x):
    hi = x.astype(BF16)
    lo = (x - hi.astype(F32)).astype(BF16)
    return hi, lo


def _head_sums(x, seg):
    hi, lo = _split2(x)
    outs = []
    for g in range(x.shape[1] // HEAD_GROUP):
        cols = slice(g * HEAD_GROUP, (g + 1) * HEAD_GROUP)
        outs.append(jnp.dot(hi[:, cols], seg, preferred_element_type=F32)
                    + jnp.dot(lo[:, cols], seg, preferred_element_type=F32))
    return jnp.concatenate(outs, axis=1)


def _cumsum_dot(tri_bf16, x):
    hi, lo = _split2(x)
    return (jnp.dot(tri_bf16, hi, preferred_element_type=F32)
            + jnp.dot(tri_bf16, lo, preferred_element_type=F32))


def _rmsnorm(x, g):
    return x * lax.rsqrt(jnp.mean(x * x, axis=-1, keepdims=True) + NORM_EPS) * g


def _ffn_kernel(x_ref, g_ref, wg_ref, wu_ref, wd_ref, fg_ref, o_ref, n_scr, *, final_norm):
    f = pl.program_id(1)

    @pl.when(f == 0)
    def _():
        x = x_ref[...]
        n_scr[...] = _rmsnorm(x, g_ref[...]).astype(BF16)
        o_ref[...] = x

    n = n_scr[...]
    hg = jnp.dot(n, wg_ref[...], preferred_element_type=F32)
    hu = jnp.dot(n, wu_ref[...], preferred_element_type=F32)
    h = (hg * jax.nn.sigmoid(hg) * (0.5 * hu)).astype(BF16)
    o_ref[...] += jnp.dot(h, wd_ref[...], preferred_element_type=F32)

    if final_norm:
        @pl.when(f == pl.num_programs(1) - 1)
        def _():
            o_ref[...] = _rmsnorm(o_ref[...], fg_ref[...])


def _ffn(x, g, wg, wu, wd, fg, *, final_norm, tm=1024, tf=512):
    n_tok = x.shape[0]
    return pl.pallas_call(
        functools.partial(_ffn_kernel, final_norm=final_norm),
        out_shape=jax.ShapeDtypeStruct((n_tok, D_MODEL), F32),
        grid=(n_tok // tm, D_FF // tf),
        in_specs=[
            pl.BlockSpec((tm, D_MODEL), lambda i, f: (i, 0)),
            pl.BlockSpec((1, D_MODEL), lambda i, f: (0, 0)),
            pl.BlockSpec((D_MODEL, tf), lambda i, f: (0, f)),
            pl.BlockSpec((D_MODEL, tf), lambda i, f: (0, f)),
            pl.BlockSpec((tf, D_MODEL), lambda i, f: (f, 0)),
            pl.BlockSpec((1, D_MODEL), lambda i, f: (0, 0)),
        ],
        out_specs=pl.BlockSpec((tm, D_MODEL), lambda i, f: (i, 0)),
        scratch_shapes=[pltpu.VMEM((tm, D_MODEL), BF16)],
        compiler_params=_cparams(("parallel", "arbitrary")),
        name="ffn_final" if final_norm else "ffn",
    )(x, g, wg, wu, wd, fg)


def _inproj_kernel(x_ref, g_ref, w_ref, o_ref, n_scr):
    @pl.when(pl.program_id(1) == 0)
    def _():
        n_scr[...] = _rmsnorm(x_ref[...], g_ref[...]).astype(BF16)

    o_ref[...] = jnp.dot(n_scr[...], w_ref[...], preferred_element_type=F32).astype(o_ref.dtype)


def _inproj(x, g, w, out_dtype, *, tn, tm=1024):
    n_tok = x.shape[0]
    n_cols = w.shape[1]
    return pl.pallas_call(
        _inproj_kernel,
        out_shape=jax.ShapeDtypeStruct((n_tok, n_cols), out_dtype),
        grid=(n_tok // tm, n_cols // tn),
        in_specs=[
            pl.BlockSpec((tm, D_MODEL), lambda i, j: (i, 0)),
            pl.BlockSpec((1, D_MODEL), lambda i, j: (0, 0)),
            pl.BlockSpec((D_MODEL, tn), lambda i, j: (0, j)),
        ],
        out_specs=pl.BlockSpec((tm, tn), lambda i, j: (i, j)),
        scratch_shapes=[pltpu.VMEM((tm, D_MODEL), BF16)],
        compiler_params=_cparams(("parallel", "arbitrary")),
        name="inproj",
    )(x, g, w)


ATT_TQ = 128
ATT_TK = 64


def _attn_kernel(q_ref, k0, k1, k2, k3, v0, v1, v2, v3, bias_ref, o_ref, lse_ref, kc_scr, vc_scr):
    i = pl.program_id(2)
    for j, (kr, vr) in enumerate(((k0, v0), (k1, v1), (k2, v2), (k3, v3))):
        kc_scr[j * ATT_TK:(j + 1) * ATT_TK, :] = kr[...]
        vc_scr[j * ATT_TK:(j + 1) * ATT_TK, :] = vr[...]

    case = jnp.where(i == 0, 1, jnp.where(i == pl.num_programs(2) - 1, 2, 0))
    lane = lax.broadcasted_iota(jnp.int32, (ATT_TQ, LANES), 1)
    first = lane < HEAD_DIM

    heads = range(ATT_HEADS)
    cols = [slice((h // 2) * LANES, (h // 2 + 1) * LANES) for h in heads]
    sel = [first if h % 2 == 0 else jnp.logical_not(first) for h in heads]
    scale = jnp.asarray(HEAD_DIM ** -0.5, BF16)
    qm = [jnp.where(sel[h], q_ref[:, cols[h]] * scale, jnp.zeros((ATT_TQ, LANES), BF16)) for h in heads]
    s = [_dot_nt(qm[h], kc_scr[:, cols[h]]) + bias_ref[case, h] for h in heads]
    m = [jnp.max(x, axis=-1, keepdims=True) for x in s]
    p = [jnp.exp(x - mx) for x, mx in zip(s, m)]
    l = [jnp.sum(x, axis=-1, keepdims=True) for x in p]
    o = [jnp.dot(p[h].astype(BF16), vc_scr[:, cols[h]], preferred_element_type=F32) * (1.0 / l[h])
         for h in heads]
    lse = [mx + jnp.log(lx) for mx, lx in zip(m, l)]
    for h in range(0, ATT_HEADS, 2):
        o_ref[:, cols[h]] = jnp.where(first, o[h], o[h + 1]).astype(o_ref.dtype)
    lse_tile = jnp.zeros((ATT_TQ, LANES), F32)
    for h in heads:
        lse_tile = jnp.where(lane == h, lse[h], lse_tile)
    lse_ref[...] = lse_tile


def _attention_bias(slopes, dil):
    nk = 4 * ATT_TK
    row = jnp.arange(ATT_TQ)[:, None]
    col = jnp.arange(nk)[None, :]
    rel = col - BAND_HALF - row
    in_band = jnp.abs(rel) <= BAND_HALF
    dist = (jnp.abs(rel) * dil).astype(F32)
    inside = jnp.stack([in_band, in_band & (col >= BAND_HALF), in_band & (col < BAND_HALF + ATT_TQ)])
    return jnp.where(inside[:, None], -slopes[None, :, None, None] * dist, NEG_INF)


def _attention(qkv, slopes, batch, seq, dil):
    L = seq // dil
    assert L // ATT_TQ >= 2, "first and last query block must differ"
    view = qkv.reshape(batch, L, dil * N_ATT_COLS)
    nkb = L // ATT_TK
    ratio = ATT_TQ // ATT_TK
    bias = _attention_bias(slopes, dil)

    def kv_spec(which, j):
        def imap(b, r, i):
            blk = jnp.clip(i * ratio - 1 + j, 0, nkb - 1)
            return (b, blk, 3 * r + which)
        return pl.BlockSpec((None, ATT_TK, ATT_WIDTH), imap)

    o, lse = pl.pallas_call(
        _attn_kernel,
        out_shape=[jax.ShapeDtypeStruct((batch, L, dil * ATT_WIDTH), BF16),
                   jax.ShapeDtypeStruct((batch, dil, L, LANES), F32)],
        grid=(batch, dil, L // ATT_TQ),
        in_specs=[pl.BlockSpec((None, ATT_TQ, ATT_WIDTH), lambda b, r, i: (b, i, 3 * r))]
        + [kv_spec(1, j) for j in range(4)] + [kv_spec(2, j) for j in range(4)]
        + [pl.BlockSpec(bias.shape, lambda b, r, i: (0, 0, 0, 0))],
        out_specs=[pl.BlockSpec((None, ATT_TQ, ATT_WIDTH), lambda b, r, i: (b, i, r)),
                   pl.BlockSpec((None, None, ATT_TQ, LANES), lambda b, r, i: (b, r, i, 0))],
        scratch_shapes=[pltpu.VMEM((4 * ATT_TK, ATT_WIDTH), BF16),
                        pltpu.VMEM((4 * ATT_TK, ATT_WIDTH), BF16)],
        compiler_params=_cparams(("parallel", "parallel", "arbitrary")),
        name=f"attn_d{dil}",
    )(view, *([view] * 8), bias)
    return o.reshape(batch, seq, ATT_WIDTH), lse.transpose(0, 2, 1, 3).reshape(batch, seq, LANES)


RWKV_TB = 256
PAIRS = RWKV_WIDTH // LANES
SOLVE_CHUNKS = 2


def _shifted(z, prev_row, next_row, mu_prev, mu_next):
    n = z.shape[0]
    ridx = lax.broadcasted_iota(jnp.int32, z.shape, 0)
    z_prev = jnp.where(ridx == 0, prev_row, pltpu.roll(z, 1, 0))
    z_next = jnp.where(ridx == n - 1, next_row, pltpu.roll(z, n - 1, 0))
    return z + mu_prev * (z_prev - z) + mu_next * (z_next - z)


def _halo_rows(zp_ref, zn_ref, t_blk, n_blk):
    prev_row = jnp.where(t_blk == 0, 0.0, zp_ref[7:8, :])
    next_row = jnp.where(t_blk == n_blk - 1, 0.0, zn_ref[0:1, :])
    return prev_row, next_row


def _stack_pair(x, first):
    zero = jnp.zeros_like(x)
    return jnp.concatenate([jnp.where(first, x, zero), jnp.where(first, zero, x)], axis=0)


def _rwkv_kernel(z_ref, zp_ref, zn_ref, mup_ref, mun_ref, w0_ref, w2_ref, a0_ref, a2_ref,
                 kk_ref, ka_ref, rk_ref, ones_ref, y_ref, bonus_ref,
                 at_s, bt_s, kt_s, rt_s, bd_s, kd_s, v_s, pc_s, st_s, *, n_blk):
    d = pl.program_id(1)
    step = pl.program_id(2)
    nc = RWKV_TB // CHUNK
    fwd = d == 0
    sgn = jnp.where(fwd, 1, -1)
    prep_step = jnp.minimum(step, n_blk - 1)
    t_blk = jnp.where(fwd, prep_step, n_blk - 1 - prep_step)
    wr = step % 2
    rd = 1 - wr

    @pl.when(step == 0)
    def _():
        st_s[...] = jnp.zeros_like(st_s)
        for ref in (at_s, bt_s, kt_s, rt_s, bd_s, kd_s, v_s, pc_s):
            ref[1] = jnp.zeros(ref.shape[1:], ref.dtype)

    halo_top = jnp.where(t_blk == 0, 0.0, zp_ref[...])
    halo_bot = jnp.where(t_blk == n_blk - 1, 0.0, zn_ref[...])
    mu_p = mup_ref[...]
    mu_n = mun_ref[...]
    mu_c = 1.0 - mu_p - mu_n
    ti = lax.broadcasted_iota(jnp.int32, (CHUNK, CHUNK), 0)
    si = lax.broadcasted_iota(jnp.int32, (CHUNK, CHUNK), 1)
    tri = jnp.where((ti - si) * sgn >= 0, 1.0, 0.0).astype(BF16)

    def prepare(cp):
        r0 = pl.multiple_of(cp * CHUNK, CHUNK)
        rows = pl.ds(r0, CHUNK)
        top_rows = pl.ds(pl.multiple_of(jnp.maximum(r0 - 8, 0), 8), 8)
        bot_rows = pl.ds(pl.multiple_of(jnp.minimum(r0 + CHUNK, RWKV_TB - 8), 8), 8)

        def shifted(cols):
            top = jnp.where(cp == 0, halo_top[:, cols], z_ref[top_rows, cols])
            bot = jnp.where(cp == nc - 1, halo_bot[:, cols], z_ref[bot_rows, cols])
            zc = z_ref[rows, cols]
            win = jnp.concatenate([top, zc, bot], axis=0)
            z_prev = pltpu.roll(win, 1, 0)[8:8 + CHUNK]
            z_next = pltpu.roll(win, CHUNK + 15, 0)[8:8 + CHUNK]
            return zc * mu_c[:, cols] + mu_p[:, cols] * z_prev + mu_n[:, cols] * z_next

        r = shifted(slice(0, RWKV_WIDTH))
        yield
        k = shifted(slice(RWKV_WIDTH, 2 * RWKV_WIDTH))
        yield
        v = shifted(slice(2 * RWKV_WIDTH, 3 * RWKV_WIDTH))
        v_s[wr, rows, :] = v.astype(BF16)
        yield
        low = shifted(slice(WD_COL, GD_COL))
        wd = low[:, 0:128]
        ad = low[:, 128:256]
        u = -(w0_ref[...] + _dot(jnp.tanh(wd), w2_ref[...]))
        softplus = jnp.maximum(u, 0.0) + jnp.log(1.0 + jnp.exp(-jnp.abs(u)))
        lw = -jnp.exp(-softplus - 0.5)
        yield
        iclr = jax.nn.sigmoid(a0_ref[...] + _dot(ad, a2_ref[...]))
        ones = ones_ref[...]
        kkr = k * kk_ref[...]
        kk = kkr * lax.rsqrt(jnp.maximum(_head_sums(kkr * kkr, ones), 1e-24))
        yield
        kdir = k * (1.0 + (iclr - 1.0) * ka_ref[...])
        bonus_ref[rows, :] = _head_sums(r * kdir * rk_ref[...], ones) * v
        yield
        b = kk * iclr
        lp_incl = _cumsum_dot(tri, lw)
        total = jnp.sum(lw, axis=0, keepdims=True)
        pc_s[wr, cp] = jnp.broadcast_to(jnp.exp(total), (8, RWKV_WIDTH))
        yield
        e_neg = jnp.exp(-lp_incl)
        at_s[wr, rows, :] = (-kk * jnp.exp(lp_incl - lw)).astype(BF16)
        bt_s[wr, rows, :] = (b * e_neg).astype(BF16)
        yield
        kt_s[wr, rows, :] = (kdir * e_neg).astype(BF16)
        rt_s[wr, rows, :] = r * jnp.exp(lp_incl)
        yield
        e_rest = jnp.exp(total - lp_incl)
        bd_s[wr, rows, :] = (b * e_rest).astype(BF16)
        kd_s[wr, rows, :] = (kdir * e_rest).astype(BF16)

    n2 = 2 * CHUNK
    rr = lax.broadcasted_iota(jnp.int32, (n2, n2), 0)
    cc = lax.broadcasted_iota(jnp.int32, (n2, n2), 1)
    same_head = (rr // CHUNK) == (cc // CHUNK)
    tt = rr % CHUNK
    ss = cc % CHUNK
    strict = same_head & ((tt - ss) * sgn > 0)
    incl = same_head & ((tt - ss) * sgn >= 0)
    incl2 = jnp.concatenate([incl, incl], axis=1)
    blk16 = strict & ((tt // 16) == (ss // 16))
    off32 = strict & ((tt // 32) == (ss // 32)) & ((tt // 16) != (ss // 16))
    off64 = strict & ((tt // 32) != (ss // 32))
    eye = rr == cc
    eye_f = jnp.where(eye, 1.0, 0.0)
    lane = lax.broadcasted_iota(jnp.int32, (CHUNK, LANES), 1)
    first = lane < HEAD_DIM

    def solve(chunks, tick):
        items = [(c, p) for c in chunks for p in range(PAIRS)]
        rows = [pl.ds(pl.multiple_of(c * CHUNK, CHUNK), CHUNK) for c, _ in items]
        lanes = [slice(p * LANES, (p + 1) * LANES) for _, p in items]
        each = lambda fn, *lists: [fn(*xs) for xs in zip(*lists)]
        load = lambda ref: [_stack_pair(ref[rd, r_, l_], first) for r_, l_ in zip(rows, lanes)]
        at, bt, kt, rt, bd, kd, vv = (load(ref) for ref in (at_s, bt_s, kt_s, rt_s, bd_s, kd_s, v_s))
        tick()

        prod = each(lambda a_, r_, b_, k_: _dot_nt(jnp.concatenate([a_, r_.astype(BF16)], axis=0),
                                                   jnp.concatenate([b_, k_], axis=0)), at, rt, bt, kt)
        a_ab = [jnp.where(strict, x[:n2, :n2], 0.0) for x in prod]
        a_ak = [jnp.where(strict, x[:n2, n2:], 0.0) for x in prod]
        q_bk = [jnp.where(incl2, x[n2:, :], 0.0) for x in prod]
        tick()

        d0 = [jnp.where(blk16, x, 0.0) for x in a_ab]
        t_inv = [eye_f + x for x in d0]
        pw = [_dot(x, x) for x in d0]
        tick()
        for _ in range(2):
            both = each(lambda t_, p_: _dot(jnp.concatenate([t_, p_], axis=0), p_), t_inv, pw)
            t_inv = each(lambda t_, b_: t_ + b_[:n2], t_inv, both)
            pw = [x[n2:] for x in both]
            tick()
        t_inv = each(lambda t_, p_: t_ + _dot(t_, p_), t_inv, pw)
        tick()
        for off_mask in (off32, off64):
            inner = each(lambda a_, t_: _dot(jnp.where(off_mask, a_, 0.0), t_), a_ab, t_inv)
            t_inv = each(lambda t_, i_: t_ + _dot(t_, i_), t_inv, inner)
            tick()

        av = each(_dot, a_ak, vv)
        tick()
        wu = each(lambda t_, a_, v_: _dot(t_, jnp.concatenate([a_, v_.astype(BF16)], axis=1)),
                  t_inv, at, av)
        tick()
        zero = jnp.zeros((n2, LANES), BF16)
        rhs = each(lambda w_, v_: jnp.concatenate(
            [w_.astype(BF16), jnp.concatenate([zero, v_], axis=1)], axis=0), wu, vv)
        ry = each(_dot, q_bk, rhs)
        gh = each(lambda b_, k_, x_: _dot_tn(jnp.concatenate([b_, k_], axis=0), x_),
                  bd, kd, rhs)
        tick()

        st = [st_s[p] for p in range(PAIRS)]
        for j, c in enumerate(chunks):
            sl = slice(j * PAIRS, (j + 1) * PAIRS)
            y_st = each(lambda r_, x_, s_: _dot(r_ + x_[:, :LANES], s_) + x_[:, LANES:], rt[sl], ry[sl], st)
            gs = each(lambda g_, s_: _dot(g_[:, :LANES], s_), gh[sl], st)
            pc_row = pc_s[rd, c]
            for p in range(PAIRS):
                i = j * PAIRS + p
                y_ref[rows[i], lanes[i]] = y_st[p][:CHUNK] + y_st[p][CHUNK:]
                pc_col = jnp.sum(jnp.where(eye, pc_row[0:1, lanes[i]], 0.0), axis=1, keepdims=True)
                st[p] = pc_col * st[p] + gs[p] + gh[i][:, LANES:]
        for p in range(PAIRS):
            st_s[p] = st[p]

    def chunk_body(ci, carry):
        first_chunk = ci * SOLVE_CHUNKS
        stages = itertools.chain(*[prepare(first_chunk + j) for j in range(SOLVE_CHUNKS)])

        def tick():
            for _ in range(SOLVE_CHUNKS):
                next(stages, None)

        solve([jnp.where(fwd, first_chunk + j, nc - 1 - first_chunk - j) for j in range(SOLVE_CHUNKS)], tick)
        for _ in stages:
            pass
        return carry

    lax.fori_loop(0, nc // SOLVE_CHUNKS, chunk_body, 0)


def _rwkv(z, batch, seq, mu_prev, mu_next, w0, w2, a0, a2, k_k, k_a, r_k, ones_bd):
    n_blk = seq // RWKV_TB
    hb = RWKV_TB // 8

    def time_block(d, i):
        return jnp.where(d == 0, i, n_blk - 1 - i)

    def tmap(d, i):
        return time_block(d, jnp.minimum(i, n_blk - 1))

    def smap(d, i):
        return time_block(d, jnp.maximum(i - 1, 0))

    vec = lambda: pl.BlockSpec((1, RWKV_WIDTH), lambda b, d, i: (0, 0))
    zvec = lambda: pl.BlockSpec((1, N_RWKV_COLS), lambda b, d, i: (0, 0))
    in_specs = [
        pl.BlockSpec((None, RWKV_TB, N_RWKV_COLS), lambda b, d, i: (b, tmap(d, i), 0)),
        pl.BlockSpec((None, 8, N_RWKV_COLS), lambda b, d, i: (b, jnp.maximum(tmap(d, i) * hb - 1, 0), 0)),
        pl.BlockSpec((None, 8, N_RWKV_COLS),
                     lambda b, d, i: (b, jnp.minimum((tmap(d, i) + 1) * hb, seq // 8 - 1), 0)),
        zvec(), zvec(),
        pl.BlockSpec((None, 1, RWKV_WIDTH), lambda b, d, i: (d, 0, 0)),
        pl.BlockSpec((None, 128, RWKV_WIDTH), lambda b, d, i: (d, 0, 0)),
        pl.BlockSpec((None, 1, RWKV_WIDTH), lambda b, d, i: (d, 0, 0)),
        pl.BlockSpec((None, 128, RWKV_WIDTH), lambda b, d, i: (d, 0, 0)),
        vec(), vec(), vec(),
        pl.BlockSpec((HEAD_GROUP, HEAD_GROUP), lambda b, d, i: (0, 0)),
    ]
    out_spec = lambda m: pl.BlockSpec((None, None, RWKV_TB, RWKV_WIDTH), lambda b, d, i: (d, b, m(d, i), 0))
    tok = lambda dt: pltpu.VMEM((2, RWKV_TB, RWKV_WIDTH), dt)
    return pl.pallas_call(
        functools.partial(_rwkv_kernel, n_blk=n_blk),
        out_shape=[jax.ShapeDtypeStruct((2, batch, seq, RWKV_WIDTH), F32)] * 2,
        grid=(batch, 2, n_blk + 1),
        in_specs=in_specs,
        out_specs=[out_spec(smap), out_spec(tmap)],
        scratch_shapes=[tok(BF16), tok(BF16), tok(BF16), tok(F32), tok(BF16), tok(BF16), tok(BF16),
                        pltpu.VMEM((2, RWKV_TB // CHUNK, 8, RWKV_WIDTH), F32),
                        pltpu.VMEM((PAIRS, LANES, LANES), F32)],
        compiler_params=_cparams(("parallel", "parallel", "arbitrary")),
        name="rwkv",
    )(z, z, z, mu_prev, mu_next, w0, w2, a0, a2, k_k, k_a, r_k, ones_bd)


MIX_TM = 256


def _mixout_kernel(x_ref, o1, o2, o3, l1, l2, l3, y_ref, bonus_ref, gd_ref, gdp_ref, gdn_ref,
                   mup_ref, mun_ref, g2_ref, lnw_ref, lnb_ref, avg_ref, spread_ref, wo_ref, out_ref, *, n_blk):
    i = pl.program_id(1)
    m = jnp.maximum(jnp.maximum(l1[...], l2[...]), l3[...])
    e = [jnp.exp(l[...] - m) for l in (l1, l2, l3)]
    inv = 1.0 / (e[0] + e[1] + e[2])
    spread = spread_ref[...]
    att = jnp.zeros((MIX_TM, ATT_WIDTH), F32)
    for ep, op in zip(e, (o1, o2, o3)):
        hi, lo = _split2(ep * inv)
        wts = jnp.dot(hi, spread, preferred_element_type=F32) + jnp.dot(lo, spread, preferred_element_type=F32)
        att = att + wts * op[...].astype(F32)

    y = y_ref[0] + y_ref[1]
    avg = avg_ref[...]
    yc = y - _head_sums(y, avg)
    var = _head_sums(yc * yc, avg)
    yn = yc * lax.rsqrt(var + RWKV_LN_EPS) * lnw_ref[...] + lnb_ref[...]
    prev_row, next_row = _halo_rows(gdp_ref, gdn_ref, i, n_blk)
    gd = _shifted(gd_ref[...], prev_row, next_row, mup_ref[...], mun_ref[...])
    gate = _dot(jax.nn.sigmoid(gd), g2_ref[...])
    rw = (yn + bonus_ref[0] + bonus_ref[1]) * gate

    mixed = jnp.dot(att.astype(BF16), wo_ref[0:ATT_WIDTH, :], preferred_element_type=F32)
    mixed = mixed + jnp.dot(rw.astype(BF16), wo_ref[ATT_WIDTH:, :], preferred_element_type=F32)
    out_ref[...] = x_ref[...] + mixed


def _mixout(x1, os, lses, y, bonus, z, batch, seq, mu_prev_g, mu_next_g, g2, ln_w, ln_b, avg_bd, spread, w_out):
    n_blk = seq // MIX_TM
    hb = MIX_TM // 8
    gcol = GD_COL // 128
    tok = lambda w: pl.BlockSpec((None, MIX_TM, w), lambda b, i: (b, i, 0))
    const = lambda shape: pl.BlockSpec(shape, lambda b, i: (0,) * len(shape))
    dir_spec = lambda: pl.BlockSpec((2, None, MIX_TM, RWKV_WIDTH), lambda b, i: (0, b, i, 0))
    in_specs = [tok(D_MODEL)] + [tok(ATT_WIDTH)] * 3 + [tok(LANES)] * 3 + [dir_spec(), dir_spec()] + [
        pl.BlockSpec((None, MIX_TM, 128), lambda b, i: (b, i, gcol)),
        pl.BlockSpec((None, 8, 128), lambda b, i: (b, jnp.maximum(i * hb - 1, 0), gcol)),
        pl.BlockSpec((None, 8, 128), lambda b, i: (b, jnp.minimum((i + 1) * hb, seq // 8 - 1), gcol)),
        const((1, 128)), const((1, 128)), const((128, RWKV_WIDTH)),
        const((1, RWKV_WIDTH)), const((1, RWKV_WIDTH)), const((HEAD_GROUP, HEAD_GROUP)),
        const((LANES, ATT_WIDTH)), const((D_MODEL, D_MODEL)),
    ]
    return pl.pallas_call(
        functools.partial(_mixout_kernel, n_blk=n_blk),
        out_shape=jax.ShapeDtypeStruct((batch, seq, D_MODEL), F32),
        grid=(batch, n_blk),
        in_specs=in_specs,
        out_specs=tok(D_MODEL),
        compiler_params=_cparams(("parallel", "arbitrary")),
        name="mixout",
    )(x1, *os, *lses, y, bonus, z, z, z, mu_prev_g, mu_next_g, g2, ln_w, ln_b, avg_bd, spread, w_out)


def _block_diag(width, block, value):
    idx = jnp.arange(width) // block
    return jnp.where(idx[:, None] == idx[None, :], value, 0.0).astype(BF16)


def _trunk(x, p):
    batch, seq, _ = x.shape
    xf = x.reshape(batch * seq, D_MODEL)
    x1 = _ffn(xf, p["ffn1_norm"], p["ffn1_gate"], p["ffn1_up"], p["ffn1_down"], p["final_norm"],
              final_norm=False)
    qkv = _inproj(x1, p["mix_norm"], p["w_in_att"], BF16, tn=768).reshape(batch, seq, N_ATT_COLS)
    z = _inproj(x1, p["mix_norm"], p["w_in_rwkv"], F32, tn=1152).reshape(batch, seq, N_RWKV_COLS)
    os, lses = [], []
    for dil in DILATIONS:
        o, lse = _attention(qkv, p["slopes"], batch, seq, dil)
        os.append(o)
        lses.append(lse)
    y, bonus = _rwkv(z, batch, seq, p["mu_prev"], p["mu_next"], p["w0"], p["w2"], p["a0"], p["a2"],
                     p["k_k"], p["k_a"], p["r_k"], p["ones_bd"])
    x2 = _mixout(x1.reshape(batch, seq, D_MODEL), os, lses, y, bonus, z, batch, seq,
                 p["mu_prev_g"], p["mu_next_g"], p["g2"], p["ln_x_w"], p["ln_x_b"], p["avg_bd"], p["spread"], p["w_out"])
    out = _ffn(x2.reshape(batch * seq, D_MODEL), p["ffn2_norm"], p["ffn2_gate"], p["ffn2_up"],
               p["ffn2_down"], p["final_norm"], final_norm=True)
    return out.reshape(batch, seq, D_MODEL)


def kernel(x_prompt, x_sample, ffn1_norm, ffn1_gate, ffn1_up, ffn1_down, mix_norm, w_in, w_out, mu_prev, mu_next, w0_f, w2_f, w0_b, w2_b, a0_f, a2_f, a0_b, a2_b, g2, k_k, k_a, r_k, ln_x_w, ln_x_b, ffn2_norm, ffn2_gate, ffn2_up, ffn2_down, final_norm):
    assert ffn1_norm.shape[0] == 1, "single layer"
    row = lambda t: t.reshape(1, -1)
    zero64 = jnp.zeros((64, RWKV_WIDTH), F32)
    w_in_b = w_in[0].astype(BF16)
    p = {
        "ffn1_norm": row(ffn1_norm[0]), "ffn1_gate": ffn1_gate[0].astype(BF16),
        "ffn1_up": ffn1_up[0].astype(BF16), "ffn1_down": ffn1_down[0].astype(BF16),
        "mix_norm": row(mix_norm[0]),
        "w_in_att": w_in_b[:, :N_ATT_COLS], "w_in_rwkv": w_in_b[:, N_ATT_COLS:],
        "w_out": w_out[0].astype(BF16),
        "mu_prev": row(mu_prev[0]), "mu_next": row(mu_next[0]),
        "mu_prev_g": row(mu_prev[0, GD_COL:]), "mu_next_g": row(mu_next[0, GD_COL:]),
        "w0": jnp.stack([row(w0_f[0]), row(w0_b[0])]),
        "w2": jnp.stack([jnp.concatenate([w2_f[0], zero64]), jnp.concatenate([zero64, w2_b[0]])]).astype(BF16),
        "a0": jnp.stack([row(a0_f[0]), row(a0_b[0])]),
        "a2": jnp.stack([jnp.concatenate([a2_f[0], zero64]), jnp.concatenate([zero64, a2_b[0]])]).astype(BF16),
        "g2": g2[0].astype(BF16),
        "k_k": row(k_k[0]), "k_a": row(k_a[0]), "r_k": row(r_k[0]),
        "ln_x_w": row(ln_x_w[0]), "ln_x_b": row(ln_x_b[0]),
        "ffn2_norm": row(ffn2_norm[0]), "ffn2_gate": ffn2_gate[0].astype(BF16),
        "ffn2_up": ffn2_up[0].astype(BF16), "ffn2_down": ffn2_down[0].astype(BF16),
        "final_norm": row(final_norm),
        "slopes": jnp.exp2(-8.0 * jnp.arange(1, ATT_HEADS + 1, dtype=F32) / ATT_HEADS),
        "ones_bd": _block_diag(HEAD_GROUP, HEAD_DIM, 1.0),
        "avg_bd": _block_diag(HEAD_GROUP, HEAD_DIM, 1.0 / HEAD_DIM),
        "spread": (jnp.arange(LANES)[:, None] == jnp.arange(ATT_WIDTH)[None, :] // HEAD_DIM).astype(BF16),
    }
    return (_trunk(x_prompt, p), _trunk(x_sample, p))
```

```python
import functools
import itertools

import jax
import jax.numpy as jnp
from jax import lax
from jax.experimental import pallas as pl
from jax.experimental.pallas import tpu as pltpu

F32 = jnp.float32
BF16 = jnp.bfloat16

D_MODEL = 2048
D_FF = 5632
HEAD_DIM = 64
ATT_WIDTH = 1024
ATT_HEADS = 16
RWKV_WIDTH = 1024
N_ATT_COLS = 3 * ATT_WIDTH
N_RWKV_COLS = 3 * RWKV_WIDTH + 64 * 4 + 128
WD_COL = 3 * RWKV_WIDTH
AD_COL = WD_COL + 128
GD_COL = AD_COL + 128
DILATIONS = (1, 4, 16)
BAND_HALF = 64
NORM_EPS = 1e-6
RWKV_LN_EPS = 64e-5
NEG_INF = -1e30

LANES = 128
HEAD_GROUP = 256
CHUNK = 64
VMEM_LIMIT = 56 * 1024 * 1024


def _cparams(sem):
    return pltpu.CompilerParams(dimension_semantics=sem, vmem_limit_bytes=VMEM_LIMIT)


def _dot(a, b):
    return jnp.dot(a.astype(BF16), b.astype(BF16), preferred_element_type=F32)


def _dot_nt(a, b):
    return lax.dot_general(a.astype(BF16), b.astype(BF16), (((1,), (1,)), ((), ())),
                           preferred_element_type=F32)


def _dot_tn(a, b):
    return lax.dot_general(a.astype(BF16), b.astype(BF16), (((0,), (0,)), ((), ())),
                           preferred_element_type=F32)


def _split2(x):
    hi = x.astype(BF16)
    lo = (x - hi.astype(F32)).astype(BF16)
    return hi, lo


def _head_sums(x, seg):
    hi, lo = _split2(x)
    outs = []
    for g in range(x.shape[1] // HEAD_GROUP):
        cols = slice(g * HEAD_GROUP, (g + 1) * HEAD_GROUP)
        outs.append(jnp.dot(hi[:, cols], seg, preferred_element_type=F32)
                    + jnp.dot(lo[:, cols], seg, preferred_element_type=F32))
    return jnp.concatenate(outs, axis=1)


def _cumsum_dot(tri_bf16, x):
    hi, lo = _split2(x)
    return (jnp.dot(tri_bf16, hi, preferred_element_type=F32)
            + jnp.dot(tri_bf16, lo, preferred_element_type=F32))


def _rmsnorm(x, g):
    return x * lax.rsqrt(jnp.mean(x * x, axis=-1, keepdims=True) + NORM_EPS) * g


def _ffn_kernel(x_ref, g_ref, wg_ref, wu_ref, wd_ref, fg_ref, o_ref, n_scr, *, final_norm):
    f = pl.program_id(1)

    @pl.when(f == 0)
    def _():
        x = x_ref[...]
        n_scr[...] = _rmsnorm(x, g_ref[...]).astype(BF16)
        o_ref[...] = x

    n = n_scr[...]
    hg = jnp.dot(n, wg_ref[...], preferred_element_type=F32)
    hu = jnp.dot(n, wu_ref[...], preferred_element_type=F32)
    h = (hg * jax.nn.sigmoid(hg) * (0.5 * hu)).astype(BF16)
    o_ref[...] += jnp.dot(h, wd_ref[...], preferred_element_type=F32)

    if final_norm:
        @pl.when(f == pl.num_programs(1) - 1)
        def _():
            o_ref[...] = _rmsnorm(o_ref[...], fg_ref[...])


def _ffn(x, g, wg, wu, wd, fg, *, final_norm, tm=1024, tf=512):
    n_tok = x.shape[0]
    return pl.pallas_call(
        functools.partial(_ffn_kernel, final_norm=final_norm),
        out_shape=jax.ShapeDtypeStruct((n_tok, D_MODEL), F32),
        grid=(n_tok // tm, D_FF // tf),
        in_specs=[
            pl.BlockSpec((tm, D_MODEL), lambda i, f: (i, 0)),
            pl.BlockSpec((1, D_MODEL), lambda i, f: (0, 0)),
            pl.BlockSpec((D_MODEL, tf), lambda i, f: (0, f)),
            pl.BlockSpec((D_MODEL, tf), lambda i, f: (0, f)),
            pl.BlockSpec((tf, D_MODEL), lambda i, f: (f, 0)),
            pl.BlockSpec((1, D_MODEL), lambda i, f: (0, 0)),
        ],
        out_specs=pl.BlockSpec((tm, D_MODEL), lambda i, f: (i, 0)),
        scratch_shapes=[pltpu.VMEM((tm, D_MODEL), BF16)],
        compiler_params=_cparams(("parallel", "arbitrary")),
        name="ffn_final" if final_norm else "ffn",
    )(x, g, wg, wu, wd, fg)


def _inproj_kernel(x_ref, g_ref, w_ref, o_ref, n_scr):
    @pl.when(pl.program_id(1) == 0)
    def _():
        n_scr[...] = _rmsnorm(x_ref[...], g_ref[...]).astype(BF16)

    o_ref[...] = jnp.dot(n_scr[...], w_ref[...], preferred_element_type=F32).astype(o_ref.dtype)


def _inproj(x, g, w, out_dtype, *, tn, tm=1024):
    n_tok = x.shape[0]
    n_cols = w.shape[1]
    return pl.pallas_call(
        _inproj_kernel,
        out_shape=jax.ShapeDtypeStruct((n_tok, n_cols), out_dtype),
        grid=(n_tok // tm, n_cols // tn),
        in_specs=[
            pl.BlockSpec((tm, D_MODEL), lambda i, j: (i, 0)),
            pl.BlockSpec((1, D_MODEL), lambda i, j: (0, 0)),
            pl.BlockSpec((D_MODEL, tn), lambda i, j: (0, j)),
        ],
        out_specs=pl.BlockSpec((tm, tn), lambda i, j: (i, j)),
        scratch_shapes=[pltpu.VMEM((tm, D_MODEL), BF16)],
        compiler_params=_cparams(("parallel", "arbitrary")),
        name="inproj",
    )(x, g, w)


def _inproj_views_kernel(x_ref, g_ref, w_ref, *refs):
    out_refs, (n_scr, acc_scr) = refs[:len(DILATIONS)], refs[len(DILATIONS):]

    @pl.when(pl.program_id(1) == 0)
    def _():
        n_scr[...] = _rmsnorm(x_ref[...], g_ref[...]).astype(BF16)

    acc = jnp.dot(n_scr[...], w_ref[...], preferred_element_type=F32)
    out_refs[0][...] = acc.astype(BF16)
    n_tiles, tm, _ = acc_scr.shape
    for c in range(n_tiles):
        acc_scr[c] = acc[:, c * LANES:(c + 1) * LANES]
    for d, o_ref in zip(DILATIONS[1:], out_refs[1:]):
        for r in range(d):
            for c in range(n_tiles):
                col = r * ATT_WIDTH + c * LANES
                o_ref[:, col:col + LANES] = acc_scr[c, pl.ds(r, tm // d, stride=d), :].astype(BF16)


def _inproj_views(x, g, w, *, tm=1024):
    n_tok = x.shape[0]
    return pl.pallas_call(
        _inproj_views_kernel,
        out_shape=[jax.ShapeDtypeStruct((3, n_tok // d, d * ATT_WIDTH), BF16) for d in DILATIONS],
        grid=(n_tok // tm, 3),
        in_specs=[
            pl.BlockSpec((tm, D_MODEL), lambda i, j: (i, 0)),
            pl.BlockSpec((1, D_MODEL), lambda i, j: (0, 0)),
            pl.BlockSpec((D_MODEL, ATT_WIDTH), lambda i, j: (0, j)),
        ],
        out_specs=[pl.BlockSpec((None, tm // d, d * ATT_WIDTH), lambda i, j: (j, i, 0)) for d in DILATIONS],
        scratch_shapes=[pltpu.VMEM((tm, D_MODEL), BF16), pltpu.VMEM((ATT_WIDTH // LANES, tm, LANES), F32)],
        compiler_params=_cparams(("parallel", "arbitrary")),
        name="inproj_views",
    )(x, g, w)


ATT_TQ = 128
ATT_TK = 64


def _attn_kernel(q_ref, k0, k1, k2, k3, v0, v1, v2, v3, bias_ref, o_ref, lse_ref, kc_scr, vc_scr):
    i = pl.program_id(2)
    for j, (kr, vr) in enumerate(((k0, v0), (k1, v1), (k2, v2), (k3, v3))):
        kc_scr[j * ATT_TK:(j + 1) * ATT_TK, :] = kr[...]
        vc_scr[j * ATT_TK:(j + 1) * ATT_TK, :] = vr[...]

    case = jnp.where(i == 0, 1, jnp.where(i == pl.num_programs(2) - 1, 2, 0))
    lane = lax.broadcasted_iota(jnp.int32, (ATT_TQ, LANES), 1)
    first = lane < HEAD_DIM

    heads = range(ATT_HEADS)
    cols = [slice((h // 2) * LANES, (h // 2 + 1) * LANES) for h in heads]
    sel = [first if h % 2 == 0 else jnp.logical_not(first) for h in heads]
    scale = jnp.asarray(HEAD_DIM ** -0.5, BF16)
    qm = [jnp.where(sel[h], q_ref[:, cols[h]] * scale, jnp.zeros((ATT_TQ, LANES), BF16)) for h in heads]
    s = [_dot_nt(qm[h], kc_scr[:, cols[h]]) + bias_ref[case, h] for h in heads]
    m = [jnp.max(x, axis=-1, keepdims=True) for x in s]
    p = [jnp.exp(x - mx) for x, mx in zip(s, m)]
    l = [jnp.sum(x, axis=-1, keepdims=True) for x in p]
    o = [jnp.dot(p[h].astype(BF16), vc_scr[:, cols[h]], preferred_element_type=F32) * (1.0 / l[h])
         for h in heads]
    lse = [mx + jnp.log(lx) for mx, lx in zip(m, l)]
    for h in range(0, ATT_HEADS, 2):
        o_ref[:, cols[h]] = jnp.where(first, o[h], o[h + 1]).astype(o_ref.dtype)
    lse_tile = jnp.zeros((ATT_TQ, LANES), F32)
    for h in heads:
        lse_tile = jnp.where(lane == h, lse[h], lse_tile)
    lse_ref[...] = lse_tile


def _attention_bias(slopes, dil):
    nk = 4 * ATT_TK
    row = jnp.arange(ATT_TQ)[:, None]
    col = jnp.arange(nk)[None, :]
    rel = col - BAND_HALF - row
    in_band = jnp.abs(rel) <= BAND_HALF
    dist = (jnp.abs(rel) * dil).astype(F32)
    inside = jnp.stack([in_band, in_band & (col >= BAND_HALF), in_band & (col < BAND_HALF + ATT_TQ)])
    return jnp.where(inside[:, None], -slopes[None, :, None, None] * dist, NEG_INF)


def _attention(view, slopes, batch, seq, dil):
    L = seq // dil
    assert L // ATT_TQ >= 2, "first and last query block must differ"
    nkb = L // ATT_TK
    ratio = ATT_TQ // ATT_TK
    bias = _attention_bias(slopes, dil)

    def kv_spec(which, j):
        def imap(b, r, i):
            blk = jnp.clip(i * ratio - 1 + j, 0, nkb - 1)
            return (which, b, blk, r)
        return pl.BlockSpec((None, None, ATT_TK, ATT_WIDTH), imap)

    o, lse = pl.pallas_call(
        _attn_kernel,
        out_shape=[jax.ShapeDtypeStruct((batch, L, dil * ATT_WIDTH), BF16),
                   jax.ShapeDtypeStruct((batch, dil, L, LANES), F32)],
        grid=(batch, dil, L // ATT_TQ),
        in_specs=[pl.BlockSpec((None, None, ATT_TQ, ATT_WIDTH), lambda b, r, i: (0, b, i, r))]
        + [kv_spec(1, j) for j in range(4)] + [kv_spec(2, j) for j in range(4)]
        + [pl.BlockSpec(bias.shape, lambda b, r, i: (0, 0, 0, 0))],
        out_specs=[pl.BlockSpec((None, ATT_TQ, ATT_WIDTH), lambda b, r, i: (b, i, r)),
                   pl.BlockSpec((None, None, ATT_TQ, LANES), lambda b, r, i: (b, r, i, 0))],
        scratch_shapes=[pltpu.VMEM((4 * ATT_TK, ATT_WIDTH), BF16),
                        pltpu.VMEM((4 * ATT_TK, ATT_WIDTH), BF16)],
        compiler_params=_cparams(("parallel", "parallel", "arbitrary")),
        name=f"attn_d{dil}",
    )(view, *([view] * 8), bias)
    return o.reshape(batch, seq, ATT_WIDTH), lse.transpose(0, 2, 1, 3).reshape(batch, seq, LANES)


RWKV_TB = 256
PAIRS = RWKV_WIDTH // LANES
SOLVE_CHUNKS = 2


def _shifted(z, prev_row, next_row, mu_prev, mu_next):
    n = z.shape[0]
    ridx = lax.broadcasted_iota(jnp.int32, z.shape, 0)
    z_prev = jnp.where(ridx == 0, prev_row, pltpu.roll(z, 1, 0))
    z_next = jnp.where(ridx == n - 1, next_row, pltpu.roll(z, n - 1, 0))
    return z + mu_prev * (z_prev - z) + mu_next * (z_next - z)


def _halo_rows(zp_ref, zn_ref, t_blk, n_blk):
    prev_row = jnp.where(t_blk == 0, 0.0, zp_ref[7:8, :])
    next_row = jnp.where(t_blk == n_blk - 1, 0.0, zn_ref[0:1, :])
    return prev_row, next_row


def _stack_pair(x, first):
    zero = jnp.zeros_like(x)
    return jnp.concatenate([jnp.where(first, x, zero), jnp.where(first, zero, x)], axis=0)


def _rwkv_kernel(z_ref, zp_ref, zn_ref, mup_ref, mun_ref, w0_ref, w2_ref, a0_ref, a2_ref,
                 kk_ref, ka_ref, rk_ref, ones_ref, y_ref, bonus_ref,
                 at_s, bt_s, kt_s, rt_s, bd_s, kd_s, v_s, pc_s, st_s, *, n_blk):
    d = pl.program_id(1)
    step = pl.program_id(2)
    nc = RWKV_TB // CHUNK
    fwd = d == 0
    sgn = jnp.where(fwd, 1, -1)
    prep_step = jnp.minimum(step, n_blk - 1)
    t_blk = jnp.where(fwd, prep_step, n_blk - 1 - prep_step)
    wr = step % 2
    rd = 1 - wr

    @pl.when(step == 0)
    def _():
        st_s[...] = jnp.zeros_like(st_s)
        for ref in (at_s, bt_s, kt_s, rt_s, bd_s, kd_s, v_s, pc_s):
            ref[1] = jnp.zeros(ref.shape[1:], ref.dtype)

    halo_top = jnp.where(t_blk == 0, 0.0, zp_ref[...])
    halo_bot = jnp.where(t_blk == n_blk - 1, 0.0, zn_ref[...])
    mu_p = mup_ref[...]
    mu_n = mun_ref[...]
    mu_c = 1.0 - mu_p - mu_n
    ti = lax.broadcasted_iota(jnp.int32, (CHUNK, CHUNK), 0)
    si = lax.broadcasted_iota(jnp.int32, (CHUNK, CHUNK), 1)
    tri = jnp.where((ti - si) * sgn >= 0, 1.0, 0.0).astype(BF16)

    def prepare(cp):
        r0 = pl.multiple_of(cp * CHUNK, CHUNK)
        rows = pl.ds(r0, CHUNK)
        top_rows = pl.ds(pl.multiple_of(jnp.maximum(r0 - 8, 0), 8), 8)
        bot_rows = pl.ds(pl.multiple_of(jnp.minimum(r0 + CHUNK, RWKV_TB - 8), 8), 8)

        def shifted(cols):
            top = jnp.where(cp == 0, halo_top[:, cols], z_ref[top_rows, cols])
            bot = jnp.where(cp == nc - 1, halo_bot[:, cols], z_ref[bot_rows, cols])
            zc = z_ref[rows, cols]
            win = jnp.concatenate([top, zc, bot], axis=0)
            z_prev = pltpu.roll(win, 1, 0)[8:8 + CHUNK]
            z_next = pltpu.roll(win, CHUNK + 15, 0)[8:8 + CHUNK]
            return zc * mu_c[:, cols] + mu_p[:, cols] * z_prev + mu_n[:, cols] * z_next

        r = shifted(slice(0, RWKV_WIDTH))
        yield
        k = shifted(slice(RWKV_WIDTH, 2 * RWKV_WIDTH))
        yield
        v = shifted(slice(2 * RWKV_WIDTH, 3 * RWKV_WIDTH))
        v_s[wr, rows, :] = v.astype(BF16)
        yield
        low = shifted(slice(WD_COL, GD_COL))
        wd = low[:, 0:128]
        ad = low[:, 128:256]
        u = -(w0_ref[...] + _dot(jnp.tanh(wd), w2_ref[...]))
        softplus = jnp.maximum(u, 0.0) + jnp.log(1.0 + jnp.exp(-jnp.abs(u)))
        lw = -jnp.exp(-softplus - 0.5)
        yield
        iclr = jax.nn.sigmoid(a0_ref[...] + _dot(ad, a2_ref[...]))
        ones = ones_ref[...]
        kkr = k * kk_ref[...]
        kk = kkr * lax.rsqrt(jnp.maximum(_head_sums(kkr * kkr, ones), 1e-24))
        yield
        kdir = k * (1.0 + (iclr - 1.0) * ka_ref[...])
        bonus_ref[rows, :] = _head_sums(r * kdir * rk_ref[...], ones) * v
        yield
        b = kk * iclr
        lp_incl = _cumsum_dot(tri, lw)
        total = jnp.sum(lw, axis=0, keepdims=True)
        pc_s[wr, cp] = jnp.broadcast_to(jnp.exp(total), (8, RWKV_WIDTH))
        yield
        e_neg = jnp.exp(-lp_incl)
        at_s[wr, rows, :] = (-kk * jnp.exp(lp_incl - lw)).astype(BF16)
        bt_s[wr, rows, :] = (b * e_neg).astype(BF16)
        yield
        kt_s[wr, rows, :] = (kdir * e_neg).astype(BF16)
        rt_s[wr, rows, :] = r * jnp.exp(lp_incl)
        yield
        e_rest = jnp.exp(total - lp_incl)
        bd_s[wr, rows, :] = (b * e_rest).astype(BF16)
        kd_s[wr, rows, :] = (kdir * e_rest).astype(BF16)

    n2 = 2 * CHUNK
    rr = lax.broadcasted_iota(jnp.int32, (n2, n2), 0)
    cc = lax.broadcasted_iota(jnp.int32, (n2, n2), 1)
    same_head = (rr // CHUNK) == (cc // CHUNK)
    tt = rr % CHUNK
    ss = cc % CHUNK
    strict = same_head & ((tt - ss) * sgn > 0)
    incl = same_head & ((tt - ss) * sgn >= 0)
    incl2 = jnp.concatenate([incl, incl], axis=1)
    blk16 = strict & ((tt // 16) == (ss // 16))
    off32 = strict & ((tt // 32) == (ss // 32)) & ((tt // 16) != (ss // 16))
    off64 = strict & ((tt // 32) != (ss // 32))
    eye = rr == cc
    eye_f = jnp.where(eye, 1.0, 0.0)
    lane = lax.broadcasted_iota(jnp.int32, (CHUNK, LANES), 1)
    first = lane < HEAD_DIM

    def solve(chunks, tick):
        items = [(c, p) for c in chunks for p in range(PAIRS)]
        rows = [pl.ds(pl.multiple_of(c * CHUNK, CHUNK), CHUNK) for c, _ in items]
        lanes = [slice(p * LANES, (p + 1) * LANES) for _, p in items]
        each = lambda fn, *lists: [fn(*xs) for xs in zip(*lists)]
        load = lambda ref: [_stack_pair(ref[rd, r_, l_], first) for r_, l_ in zip(rows, lanes)]
        at, bt, kt, rt, bd, kd, vv = (load(ref) for ref in (at_s, bt_s, kt_s, rt_s, bd_s, kd_s, v_s))
        tick()

        prod = each(lambda a_, r_, b_, k_: _dot_nt(jnp.concatenate([a_, r_.astype(BF16)], axis=0),
                                                   jnp.concatenate([b_, k_], axis=0)), at, rt, bt, kt)
        a_ab = [jnp.where(strict, x[:n2, :n2], 0.0) for x in prod]
        a_ak = [jnp.where(strict, x[:n2, n2:], 0.0) for x in prod]
        q_bk = [jnp.where(incl2, x[n2:, :], 0.0) for x in prod]
        tick()

        d0 = [jnp.where(blk16, x, 0.0) for x in a_ab]
        t_inv = [eye_f + x for x in d0]
        pw = [_dot(x, x) for x in d0]
        tick()
        for _ in range(2):
            both = each(lambda t_, p_: _dot(jnp.concatenate([t_, p_], axis=0), p_), t_inv, pw)
            t_inv = each(lambda t_, b_: t_ + b_[:n2], t_inv, both)
            pw = [x[n2:] for x in both]
            tick()
        t_inv = each(lambda t_, p_: t_ + _dot(t_, p_), t_inv, pw)
        tick()
        for off_mask in (off32, off64):
            inner = each(lambda a_, t_: _dot(jnp.where(off_mask, a_, 0.0), t_), a_ab, t_inv)
            t_inv = each(lambda t_, i_: t_ + _dot(t_, i_), t_inv, inner)
            tick()

        av = each(_dot, a_ak, vv)
        tick()
        wu = each(lambda t_, a_, v_: _dot(t_, jnp.concatenate([a_, v_.astype(BF16)], axis=1)),
                  t_inv, at, av)
        tick()
        zero = jnp.zeros((n2, LANES), BF16)
        rhs = each(lambda w_, v_: jnp.concatenate(
            [w_.astype(BF16), jnp.concatenate([zero, v_], axis=1)], axis=0), wu, vv)
        ry = each(_dot, q_bk, rhs)
        gh = each(lambda b_, k_, x_: _dot_tn(jnp.concatenate([b_, k_], axis=0), x_),
                  bd, kd, rhs)
        tick()

        st = [st_s[p] for p in range(PAIRS)]
        for j, c in enumerate(chunks):
            sl = slice(j * PAIRS, (j + 1) * PAIRS)
            y_st = each(lambda r_, x_, s_: _dot(r_ + x_[:, :LANES], s_) + x_[:, LANES:], rt[sl], ry[sl], st)
            gs = each(lambda g_, s_: _dot(g_[:, :LANES], s_), gh[sl], st)
            pc_row = pc_s[rd, c]
            for p in range(PAIRS):
                i = j * PAIRS + p
                y_ref[rows[i], lanes[i]] = y_st[p][:CHUNK] + y_st[p][CHUNK:]
                pc_col = jnp.sum(jnp.where(eye, pc_row[0:1, lanes[i]], 0.0), axis=1, keepdims=True)
                st[p] = pc_col * st[p] + gs[p] + gh[i][:, LANES:]
        for p in range(PAIRS):
            st_s[p] = st[p]

    def chunk_body(ci, carry):
        first_chunk = ci * SOLVE_CHUNKS
        stages = itertools.chain(*[prepare(first_chunk + j) for j in range(SOLVE_CHUNKS)])

        def tick():
            for _ in range(SOLVE_CHUNKS):
                next(stages, None)

        solve([jnp.where(fwd, first_chunk + j, nc - 1 - first_chunk - j) for j in range(SOLVE_CHUNKS)], tick)
        for _ in stages:
            pass
        return carry

    lax.fori_loop(0, nc // SOLVE_CHUNKS, chunk_body, 0)


def _rwkv(z, batch, seq, mu_prev, mu_next, w0, w2, a0, a2, k_k, k_a, r_k, ones_bd):
    n_blk = seq // RWKV_TB
    hb = RWKV_TB // 8

    def time_block(d, i):
        return jnp.where(d == 0, i, n_blk - 1 - i)

    def tmap(d, i):
        return time_block(d, jnp.minimum(i, n_blk - 1))

    def smap(d, i):
        return time_block(d, jnp.maximum(i - 1, 0))

    vec = lambda: pl.BlockSpec((1, RWKV_WIDTH), lambda b, d, i: (0, 0))
    zvec = lambda: pl.BlockSpec((1, N_RWKV_COLS), lambda b, d, i: (0, 0))
    in_specs = [
        pl.BlockSpec((None, RWKV_TB, N_RWKV_COLS), lambda b, d, i: (b, tmap(d, i), 0)),
        pl.BlockSpec((None, 8, N_RWKV_COLS), lambda b, d, i: (b, jnp.maximum(tmap(d, i) * hb - 1, 0), 0)),
        pl.BlockSpec((None, 8, N_RWKV_COLS),
                     lambda b, d, i: (b, jnp.minimum((tmap(d, i) + 1) * hb, seq // 8 - 1), 0)),
        zvec(), zvec(),
        pl.BlockSpec((None, 1, RWKV_WIDTH), lambda b, d, i: (d, 0, 0)),
        pl.BlockSpec((None, 128, RWKV_WIDTH), lambda b, d, i: (d, 0, 0)),
        pl.BlockSpec((None, 1, RWKV_WIDTH), lambda b, d, i: (d, 0, 0)),
        pl.BlockSpec((None, 128, RWKV_WIDTH), lambda b, d, i: (d, 0, 0)),
        vec(), vec(), vec(),
        pl.BlockSpec((HEAD_GROUP, HEAD_GROUP), lambda b, d, i: (0, 0)),
    ]
    out_spec = lambda m: pl.BlockSpec((None, None, RWKV_TB, RWKV_WIDTH), lambda b, d, i: (d, b, m(d, i), 0))
    tok = lambda dt: pltpu.VMEM((2, RWKV_TB, RWKV_WIDTH), dt)
    return pl.pallas_call(
        functools.partial(_rwkv_kernel, n_blk=n_blk),
        out_shape=[jax.ShapeDtypeStruct((2, batch, seq, RWKV_WIDTH), F32)] * 2,
        grid=(batch, 2, n_blk + 1),
        in_specs=in_specs,
        out_specs=[out_spec(smap), out_spec(tmap)],
        scratch_shapes=[tok(BF16), tok(BF16), tok(BF16), tok(F32), tok(BF16), tok(BF16), tok(BF16),
                        pltpu.VMEM((2, RWKV_TB // CHUNK, 8, RWKV_WIDTH), F32),
                        pltpu.VMEM((PAIRS, LANES, LANES), F32)],
        compiler_params=_cparams(("parallel", "parallel", "arbitrary")),
        name="rwkv",
    )(z, z, z, mu_prev, mu_next, w0, w2, a0, a2, k_k, k_a, r_k, ones_bd)


MIX_TM = 256


def _mixout_kernel(x_ref, o1, o2, o3, l1, l2, l3, y_ref, bonus_ref, gd_ref, gdp_ref, gdn_ref,
                   mup_ref, mun_ref, g2_ref, lnw_ref, lnb_ref, avg_ref, spread_ref, wo_ref, out_ref, *, n_blk):
    i = pl.program_id(1)
    m = jnp.maximum(jnp.maximum(l1[...], l2[...]), l3[...])
    e = [jnp.exp(l[...] - m) for l in (l1, l2, l3)]
    inv = 1.0 / (e[0] + e[1] + e[2])
    spread = spread_ref[...]
    att = jnp.zeros((MIX_TM, ATT_WIDTH), F32)
    for ep, op in zip(e, (o1, o2, o3)):
        hi, lo = _split2(ep * inv)
        wts = jnp.dot(hi, spread, preferred_element_type=F32) + jnp.dot(lo, spread, preferred_element_type=F32)
        att = att + wts * op[...].astype(F32)

    y = y_ref[0] + y_ref[1]
    avg = avg_ref[...]
    yc = y - _head_sums(y, avg)
    var = _head_sums(yc * yc, avg)
    yn = yc * lax.rsqrt(var + RWKV_LN_EPS) * lnw_ref[...] + lnb_ref[...]
    prev_row, next_row = _halo_rows(gdp_ref, gdn_ref, i, n_blk)
    gd = _shifted(gd_ref[...], prev_row, next_row, mup_ref[...], mun_ref[...])
    gate = _dot(jax.nn.sigmoid(gd), g2_ref[...])
    rw = (yn + bonus_ref[0] + bonus_ref[1]) * gate

    mixed = jnp.dot(att.astype(BF16), wo_ref[0:ATT_WIDTH, :], preferred_element_type=F32)
    mixed = mixed + jnp.dot(rw.astype(BF16), wo_ref[ATT_WIDTH:, :], preferred_element_type=F32)
    out_ref[...] = x_ref[...] + mixed


def _mixout(x1, os, lses, y, bonus, z, batch, seq, mu_prev_g, mu_next_g, g2, ln_w, ln_b, avg_bd, spread, w_out):
    n_blk = seq // MIX_TM
    hb = MIX_TM // 8
    gcol = GD_COL // 128
    tok = lambda w: pl.BlockSpec((None, MIX_TM, w), lambda b, i: (b, i, 0))
    const = lambda shape: pl.BlockSpec(shape, lambda b, i: (0,) * len(shape))
    dir_spec = lambda: pl.BlockSpec((2, None, MIX_TM, RWKV_WIDTH), lambda b, i: (0, b, i, 0))
    in_specs = [tok(D_MODEL)] + [tok(ATT_WIDTH)] * 3 + [tok(LANES)] * 3 + [dir_spec(), dir_spec()] + [
        pl.BlockSpec((None, MIX_TM, 128), lambda b, i: (b, i, gcol)),
        pl.BlockSpec((None, 8, 128), lambda b, i: (b, jnp.maximum(i * hb - 1, 0), gcol)),
        pl.BlockSpec((None, 8, 128), lambda b, i: (b, jnp.minimum((i + 1) * hb, seq // 8 - 1), gcol)),
        const((1, 128)), const((1, 128)), const((128, RWKV_WIDTH)),
        const((1, RWKV_WIDTH)), const((1, RWKV_WIDTH)), const((HEAD_GROUP, HEAD_GROUP)),
        const((LANES, ATT_WIDTH)), const((D_MODEL, D_MODEL)),
    ]
    return pl.pallas_call(
        functools.partial(_mixout_kernel, n_blk=n_blk),
        out_shape=jax.ShapeDtypeStruct((batch, seq, D_MODEL), F32),
        grid=(batch, n_blk),
        in_specs=in_specs,
        out_specs=tok(D_MODEL),
        compiler_params=_cparams(("parallel", "arbitrary")),
        name="mixout",
    )(x1, *os, *lses, y, bonus, z, z, z, mu_prev_g, mu_next_g, g2, ln_w, ln_b, avg_bd, spread, w_out)


def _block_diag(width, block, value):
    idx = jnp.arange(width) // block
    return jnp.where(idx[:, None] == idx[None, :], value, 0.0).astype(BF16)


def _trunk(x, p):
    batch, seq, _ = x.shape
    xf = x.reshape(batch * seq, D_MODEL)
    x1 = _ffn(xf, p["ffn1_norm"], p["ffn1_gate"], p["ffn1_up"], p["ffn1_down"], p["final_norm"],
              final_norm=False)
    views = _inproj_views(x1, p["mix_norm"], p["w_in_att"])
    z = _inproj(x1, p["mix_norm"], p["w_in_rwkv"], F32, tn=1152).reshape(batch, seq, N_RWKV_COLS)
    os, lses = [], []
    for dil, view in zip(DILATIONS, views):
        o, lse = _attention(view.reshape(3, batch, seq // dil, dil * ATT_WIDTH), p["slopes"], batch, seq, dil)
        os.append(o)
        lses.append(lse)
    y, bonus = _rwkv(z, batch, seq, p["mu_prev"], p["mu_next"], p["w0"], p["w2"], p["a0"], p["a2"],
                     p["k_k"], p["k_a"], p["r_k"], p["ones_bd"])
    x2 = _mixout(x1.reshape(batch, seq, D_MODEL), os, lses, y, bonus, z, batch, seq,
                 p["mu_prev_g"], p["mu_next_g"], p["g2"], p["ln_x_w"], p["ln_x_b"], p["avg_bd"], p["spread"], p["w_out"])
    out = _ffn(x2.reshape(batch * seq, D_MODEL), p["ffn2_norm"], p["ffn2_gate"], p["ffn2_up"],
               p["ffn2_down"], p["final_norm"], final_norm=True)
    return out.reshape(batch, seq, D_MODEL)


def kernel(x_prompt, x_sample, ffn1_norm, ffn1_gate, ffn1_up, ffn1_down, mix_norm, w_in, w_out, mu_prev, mu_next, w0_f, w2_f, w0_b, w2_b, a0_f, a2_f, a0_b, a2_b, g2, k_k, k_a, r_k, ln_x_w, ln_x_b, ffn2_norm, ffn2_gate, ffn2_up, ffn2_down, final_norm):
    assert ffn1_norm.shape[0] == 1, "single layer"
    row = lambda t: t.reshape(1, -1)
    zero64 = jnp.zeros((64, RWKV_WIDTH), F32)
    w_in_b = w_in[0].astype(BF16)
    p = {
        "ffn1_norm": row(ffn1_norm[0]), "ffn1_gate": ffn1_gate[0].astype(BF16),
        "ffn1_up": ffn1_up[0].astype(BF16), "ffn1_down": ffn1_down[0].astype(BF16),
        "mix_norm": row(mix_norm[0]),
        "w_in_att": w_in_b[:, :N_ATT_COLS], "w_in_rwkv": w_in_b[:, N_ATT_COLS:],
        "w_out": w_out[0].astype(BF16),
        "mu_prev": row(mu_prev[0]), "mu_next": row(mu_next[0]),
        "mu_prev_g": row(mu_prev[0, GD_COL:]), "mu_next_g": row(mu_next[0, GD_COL:]),
        "w0": jnp.stack([row(w0_f[0]), row(w0_b[0])]),
        "w2": jnp.stack([jnp.concatenate([w2_f[0], zero64]), jnp.concatenate([zero64, w2_b[0]])]).astype(BF16),
        "a0": jnp.stack([row(a0_f[0]), row(a0_b[0])]),
        "a2": jnp.stack([jnp.concatenate([a2_f[0], zero64]), jnp.concatenate([zero64, a2_b[0]])]).astype(BF16),
        "g2": g2[0].astype(BF16),
        "k_k": row(k_k[0]), "k_a": row(k_a[0]), "r_k": row(r_k[0]),
        "ln_x_w": row(ln_x_w[0]), "ln_x_b": row(ln_x_b[0]),
        "ffn2_norm": row(ffn2_norm[0]), "ffn2_gate": ffn2_gate[0].astype(BF16),
        "ffn2_up": ffn2_up[0].astype(BF16), "ffn2_down": ffn2_down[0].astype(BF16),
        "final_norm": row(final_norm),
        "slopes": jnp.exp2(-8.0 * jnp.arange(1, ATT_HEADS + 1, dtype=F32) / ATT_HEADS),
        "ones_bd": _block_diag(HEAD_GROUP, HEAD_DIM, 1.0),
        "avg_bd": _block_diag(HEAD_GROUP, HEAD_DIM, 1.0 / HEAD_DIM),
        "spread": (jnp.arange(LANES)[:, None] == jnp.arange(ATT_WIDTH)[None, :] // HEAD_DIM).astype(BF16),
    }
    return (_trunk(x_prompt, p), _trunk(x_sample, p))
```

```python
import functools
import itertools

import jax
import jax.numpy as jnp
from jax import lax
from jax.experimental import pallas as pl
from jax.experimental.pallas import tpu as pltpu

F32 = jnp.float32
BF16 = jnp.bfloat16

D_MODEL = 2048
D_FF = 5632
HEAD_DIM = 64
ATT_WIDTH = 1024
ATT_HEADS = 16
RWKV_WIDTH = 1024
N_ATT_COLS = 3 * ATT_WIDTH
N_RWKV_COLS = 3 * RWKV_WIDTH + 64 * 4 + 128
WD_COL = 3 * RWKV_WIDTH
AD_COL = WD_COL + 128
GD_COL = AD_COL + 128
DILATIONS = (1, 4, 16)
BAND_HALF = 64
NORM_EPS = 1e-6
RWKV_LN_EPS = 64e-5
NEG_INF = -1e30

LANES = 128
HEAD_GROUP = 256
CHUNK = 64
VMEM_LIMIT = 56 * 1024 * 1024


def _cparams(sem):
    return pltpu.CompilerParams(dimension_semantics=sem, vmem_limit_bytes=VMEM_LIMIT)


def _dot(a, b):
    return jnp.dot(a.astype(BF16), b.astype(BF16), preferred_element_type=F32)


def _dot_nt(a, b):
    return lax.dot_general(a.astype(BF16), b.astype(BF16), (((1,), (1,)), ((), ())),
                           preferred_element_type=F32)


def _dot_tn(a, b):
    return lax.dot_general(a.astype(BF16), b.astype(BF16), (((0,), (0,)), ((), ())),
                           preferred_element_type=F32)


def _split2(x):
    hi = x.astype(BF16)
    lo = (x - hi.astype(F32)).astype(BF16)
    return hi, lo


def _head_sums(x, seg):
    hi, lo = _split2(x)
    outs = []
    for g in range(x.shape[1] // HEAD_GROUP):
        cols = slice(g * HEAD_GROUP, (g + 1) * HEAD_GROUP)
        outs.append(jnp.dot(hi[:, cols], seg, preferred_element_type=F32)
                    + jnp.dot(lo[:, cols], seg, preferred_element_type=F32))
    return jnp.concatenate(outs, axis=1)


def _cumsum_dot(tri_bf16, x):
    hi, lo = _split2(x)
    return (jnp.dot(tri_bf16, hi, preferred_element_type=F32)
            + jnp.dot(tri_bf16, lo, preferred_element_type=F32))


def _rmsnorm(x, g):
    return x * lax.rsqrt(jnp.mean(x * x, axis=-1, keepdims=True) + NORM_EPS) * g


def _ffn_kernel(x_ref, g_ref, wg_ref, wu_ref, wd_ref, fg_ref, o_ref, n_scr, *, final_norm):
    f = pl.program_id(1)

    @pl.when(f == 0)
    def _():
        x = x_ref[...]
        n_scr[...] = _rmsnorm(x, g_ref[...]).astype(BF16)
        o_ref[...] = x

    n = n_scr[...]
    hg = jnp.dot(n, wg_ref[...], preferred_element_type=F32)
    hu = jnp.dot(n, wu_ref[...], preferred_element_type=F32)
    h = (hg * jax.nn.sigmoid(hg) * (0.5 * hu)).astype(BF16)
    o_ref[...] += jnp.dot(h, wd_ref[...], preferred_element_type=F32)

    if final_norm:
        @pl.when(f == pl.num_programs(1) - 1)
        def _():
            o_ref[...] = _rmsnorm(o_ref[...], fg_ref[...])


def _ffn(x, g, wg, wu, wd, fg, *, final_norm, tm=1024, tf=512):
    n_tok = x.shape[0]
    return pl.pallas_call(
        functools.partial(_ffn_kernel, final_norm=final_norm),
        out_shape=jax.ShapeDtypeStruct((n_tok, D_MODEL), F32),
        grid=(n_tok // tm, D_FF // tf),
        in_specs=[
            pl.BlockSpec((tm, D_MODEL), lambda i, f: (i, 0)),
            pl.BlockSpec((1, D_MODEL), lambda i, f: (0, 0)),
            pl.BlockSpec((D_MODEL, tf), lambda i, f: (0, f)),
            pl.BlockSpec((D_MODEL, tf), lambda i, f: (0, f)),
            pl.BlockSpec((tf, D_MODEL), lambda i, f: (f, 0)),
            pl.BlockSpec((1, D_MODEL), lambda i, f: (0, 0)),
        ],
        out_specs=pl.BlockSpec((tm, D_MODEL), lambda i, f: (i, 0)),
        scratch_shapes=[pltpu.VMEM((tm, D_MODEL), BF16)],
        compiler_params=_cparams(("parallel", "arbitrary")),
        name="ffn_final" if final_norm else "ffn",
    )(x, g, wg, wu, wd, fg)


def _inproj_shift_kernel(x_ref, xp_ref, xn_ref, g_ref, w_ref, mup_ref, mun_ref, o_ref, n_scr, nh_scr, *, seq):
    i = pl.program_id(0)
    tm = x_ref.shape[0]

    @pl.when(pl.program_id(1) == 0)
    def _():
        g = g_ref[...]
        n_scr[...] = _rmsnorm(x_ref[...], g).astype(BF16)
        halo = jnp.concatenate([xp_ref[...], xn_ref[...]], axis=0)
        nh_scr[...] = _rmsnorm(halo, g).astype(BF16)

    w = w_ref[...]
    z = jnp.dot(n_scr[...], w, preferred_element_type=F32)
    z_halo = jnp.dot(nh_scr[...], w, preferred_element_type=F32)
    prev_row = jnp.where((i * tm) % seq == 0, 0.0, z_halo[7:8, :])
    next_row = jnp.where(((i + 1) * tm) % seq == 0, 0.0, z_halo[8:9, :])
    o_ref[...] = _shifted(z, prev_row, next_row, mup_ref[...], mun_ref[...])


def _inproj_shift(x, g, w, mu_prev, mu_next, seq, *, tn, tm=1024):
    n_tok = x.shape[0]
    n_cols = w.shape[1]
    hb = tm // 8
    return pl.pallas_call(
        functools.partial(_inproj_shift_kernel, seq=seq),
        out_shape=jax.ShapeDtypeStruct((n_tok, n_cols), F32),
        grid=(n_tok // tm, n_cols // tn),
        in_specs=[
            pl.BlockSpec((tm, D_MODEL), lambda i, j: (i, 0)),
            pl.BlockSpec((8, D_MODEL), lambda i, j: (jnp.maximum(i * hb - 1, 0), 0)),
            pl.BlockSpec((8, D_MODEL), lambda i, j: (jnp.minimum((i + 1) * hb, n_tok // 8 - 1), 0)),
            pl.BlockSpec((1, D_MODEL), lambda i, j: (0, 0)),
            pl.BlockSpec((D_MODEL, tn), lambda i, j: (0, j)),
            pl.BlockSpec((1, tn), lambda i, j: (0, j)),
            pl.BlockSpec((1, tn), lambda i, j: (0, j)),
        ],
        out_specs=pl.BlockSpec((tm, tn), lambda i, j: (i, j)),
        scratch_shapes=[pltpu.VMEM((tm, D_MODEL), BF16), pltpu.VMEM((16, D_MODEL), BF16)],
        compiler_params=_cparams(("parallel", "arbitrary")),
        name="inproj_shift",
    )(x, x, x, g, w, mu_prev, mu_next)


def _inproj_views_kernel(x_ref, g_ref, w_ref, *refs):
    out_refs, (n_scr, acc_scr) = refs[:len(DILATIONS)], refs[len(DILATIONS):]

    @pl.when(pl.program_id(1) == 0)
    def _():
        n_scr[...] = _rmsnorm(x_ref[...], g_ref[...]).astype(BF16)

    acc = jnp.dot(n_scr[...], w_ref[...], preferred_element_type=F32)
    out_refs[0][...] = acc.astype(BF16)
    n_tiles, tm, _ = acc_scr.shape
    for c in range(n_tiles):
        acc_scr[c] = acc[:, c * LANES:(c + 1) * LANES]
    for d, o_ref in zip(DILATIONS[1:], out_refs[1:]):
        for r in range(d):
            for c in range(n_tiles):
                col = r * ATT_WIDTH + c * LANES
                o_ref[:, col:col + LANES] = acc_scr[c, pl.ds(r, tm // d, stride=d), :].astype(BF16)


def _inproj_views(x, g, w, *, tm=1024):
    n_tok = x.shape[0]
    return pl.pallas_call(
        _inproj_views_kernel,
        out_shape=[jax.ShapeDtypeStruct((3, n_tok // d, d * ATT_WIDTH), BF16) for d in DILATIONS],
        grid=(n_tok // tm, 3),
        in_specs=[
            pl.BlockSpec((tm, D_MODEL), lambda i, j: (i, 0)),
            pl.BlockSpec((1, D_MODEL), lambda i, j: (0, 0)),
            pl.BlockSpec((D_MODEL, ATT_WIDTH), lambda i, j: (0, j)),
        ],
        out_specs=[pl.BlockSpec((None, tm // d, d * ATT_WIDTH), lambda i, j: (j, i, 0)) for d in DILATIONS],
        scratch_shapes=[pltpu.VMEM((tm, D_MODEL), BF16), pltpu.VMEM((ATT_WIDTH // LANES, tm, LANES), F32)],
        compiler_params=_cparams(("parallel", "arbitrary")),
        name="inproj_views",
    )(x, g, w)


ATT_TQ = 128
ATT_TK = 64


def _attn_kernel(q_ref, k0, k1, k2, k3, v0, v1, v2, v3, bias_ref, o_ref, lse_ref, kc_scr, vc_scr):
    i = pl.program_id(2)
    for j, (kr, vr) in enumerate(((k0, v0), (k1, v1), (k2, v2), (k3, v3))):
        kc_scr[j * ATT_TK:(j + 1) * ATT_TK, :] = kr[...]
        vc_scr[j * ATT_TK:(j + 1) * ATT_TK, :] = vr[...]

    case = jnp.where(i == 0, 1, jnp.where(i == pl.num_programs(2) - 1, 2, 0))
    lane = lax.broadcasted_iota(jnp.int32, (ATT_TQ, LANES), 1)
    first = lane < HEAD_DIM

    heads = range(ATT_HEADS)
    cols = [slice((h // 2) * LANES, (h // 2 + 1) * LANES) for h in heads]
    sel = [first if h % 2 == 0 else jnp.logical_not(first) for h in heads]
    scale = jnp.asarray(HEAD_DIM ** -0.5, BF16)
    qm = [jnp.where(sel[h], q_ref[:, cols[h]] * scale, jnp.zeros((ATT_TQ, LANES), BF16)) for h in heads]
    s = [_dot_nt(qm[h], kc_scr[:, cols[h]]) + bias_ref[case, h] for h in heads]
    m = [jnp.max(x, axis=-1, keepdims=True) for x in s]
    p = [jnp.exp(x - mx) for x, mx in zip(s, m)]
    l = [jnp.sum(x, axis=-1, keepdims=True) for x in p]
    o = [jnp.dot(p[h].astype(BF16), vc_scr[:, cols[h]], preferred_element_type=F32) * (1.0 / l[h])
         for h in heads]
    lse = [mx + jnp.log(lx) for mx, lx in zip(m, l)]
    for h in range(0, ATT_HEADS, 2):
        o_ref[:, cols[h]] = jnp.where(first, o[h], o[h + 1]).astype(o_ref.dtype)
    lse_tile = jnp.zeros((ATT_TQ, LANES), F32)
    for h in heads:
        lse_tile = jnp.where(lane == h, lse[h], lse_tile)
    lse_ref[...] = lse_tile


def _attention_bias(slopes, dil):
    nk = 4 * ATT_TK
    row = jnp.arange(ATT_TQ)[:, None]
    col = jnp.arange(nk)[None, :]
    rel = col - BAND_HALF - row
    in_band = jnp.abs(rel) <= BAND_HALF
    dist = (jnp.abs(rel) * dil).astype(F32)
    inside = jnp.stack([in_band, in_band & (col >= BAND_HALF), in_band & (col < BAND_HALF + ATT_TQ)])
    return jnp.where(inside[:, None], -slopes[None, :, None, None] * dist, NEG_INF)


def _attention(view, slopes, batch, seq, dil):
    L = seq // dil
    assert L // ATT_TQ >= 2, "first and last query block must differ"
    nkb = L // ATT_TK
    ratio = ATT_TQ // ATT_TK
    bias = _attention_bias(slopes, dil)

    def kv_spec(which, j):
        def imap(b, r, i):
            blk = jnp.clip(i * ratio - 1 + j, 0, nkb - 1)
            return (which, b, blk, r)
        return pl.BlockSpec((None, None, ATT_TK, ATT_WIDTH), imap)

    o, lse = pl.pallas_call(
        _attn_kernel,
        out_shape=[jax.ShapeDtypeStruct((batch, L, dil * ATT_WIDTH), BF16),
                   jax.ShapeDtypeStruct((batch, dil, L, LANES), F32)],
        grid=(batch, dil, L // ATT_TQ),
        in_specs=[pl.BlockSpec((None, None, ATT_TQ, ATT_WIDTH), lambda b, r, i: (0, b, i, r))]
        + [kv_spec(1, j) for j in range(4)] + [kv_spec(2, j) for j in range(4)]
        + [pl.BlockSpec(bias.shape, lambda b, r, i: (0, 0, 0, 0))],
        out_specs=[pl.BlockSpec((None, ATT_TQ, ATT_WIDTH), lambda b, r, i: (b, i, r)),
                   pl.BlockSpec((None, None, ATT_TQ, LANES), lambda b, r, i: (b, r, i, 0))],
        scratch_shapes=[pltpu.VMEM((4 * ATT_TK, ATT_WIDTH), BF16),
                        pltpu.VMEM((4 * ATT_TK, ATT_WIDTH), BF16)],
        compiler_params=_cparams(("parallel", "parallel", "arbitrary")),
        name=f"attn_d{dil}",
    )(view, *([view] * 8), bias)
    return o, lse.transpose(0, 2, 1, 3).reshape(batch, seq, LANES)


RWKV_TB = 256
PAIRS = RWKV_WIDTH // LANES
SOLVE_CHUNKS = 2


def _shifted(z, prev_row, next_row, mu_prev, mu_next):
    n = z.shape[0]
    ridx = lax.broadcasted_iota(jnp.int32, z.shape, 0)
    z_prev = jnp.where(ridx == 0, prev_row, pltpu.roll(z, 1, 0))
    z_next = jnp.where(ridx == n - 1, next_row, pltpu.roll(z, n - 1, 0))
    return z + mu_prev * (z_prev - z) + mu_next * (z_next - z)


def _stack_pair(x, first):
    zero = jnp.zeros_like(x)
    return jnp.concatenate([jnp.where(first, x, zero), jnp.where(first, zero, x)], axis=0)


def _rwkv_kernel(z_ref, w0_ref, w2_ref, a0_ref, a2_ref, kk_ref, ka_ref, rk_ref, ones_ref, y_ref, bonus_ref,
                 at_s, bt_s, kt_s, rt_s, bd_s, kd_s, v_s, pc_s, st_s, mask_s):
    d = pl.program_id(1)
    step = pl.program_id(2)
    nc = RWKV_TB // CHUNK
    fwd = d == 0
    sgn = jnp.where(fwd, 1, -1)
    wr = step % 2
    rd = 1 - wr
    n2 = 2 * CHUNK

    @pl.when(step == 0)
    def _():
        st_s[...] = jnp.zeros_like(st_s)
        for ref in (at_s, bt_s, kt_s, rt_s, bd_s, kd_s, v_s, pc_s):
            ref[1] = jnp.zeros(ref.shape[1:], ref.dtype)
        rr = lax.broadcasted_iota(jnp.int32, (n2, n2), 0)
        cc = lax.broadcasted_iota(jnp.int32, (n2, n2), 1)
        same_head = (rr // CHUNK) == (cc // CHUNK)
        tt = rr % CHUNK
        ss = cc % CHUNK
        strict = same_head & ((tt - ss) * sgn > 0)
        masks = (strict,
                 same_head & ((tt - ss) * sgn >= 0),
                 strict & ((tt // 16) == (ss // 16)),
                 strict & ((tt // 32) == (ss // 32)) & ((tt // 16) != (ss // 16)),
                 strict & ((tt // 32) != (ss // 32)),
                 rr == cc)
        for idx, msk in enumerate(masks):
            mask_s[idx] = jnp.where(msk, 1.0, 0.0)

    ti = lax.broadcasted_iota(jnp.int32, (CHUNK, CHUNK), 0)
    si = lax.broadcasted_iota(jnp.int32, (CHUNK, CHUNK), 1)
    tri = jnp.where((ti - si) * sgn >= 0, 1.0, 0.0).astype(BF16)

    def prepare(cp):
        rows = pl.ds(pl.multiple_of(cp * CHUNK, CHUNK), CHUNK)
        r = z_ref[rows, 0:RWKV_WIDTH]
        k = z_ref[rows, RWKV_WIDTH:2 * RWKV_WIDTH]
        v = z_ref[rows, 2 * RWKV_WIDTH:3 * RWKV_WIDTH]
        v_s[wr, rows, :] = v.astype(BF16)
        yield
        wd = z_ref[rows, WD_COL:WD_COL + 128]
        ad = z_ref[rows, AD_COL:AD_COL + 128]
        u = -(w0_ref[...] + _dot(jnp.tanh(wd), w2_ref[...]))
        softplus = jnp.maximum(u, 0.0) + jnp.log(1.0 + jnp.exp(-jnp.abs(u)))
        lw = -jnp.exp(-softplus - 0.5)
        yield
        iclr = jax.nn.sigmoid(a0_ref[...] + _dot(ad, a2_ref[...]))
        ones = ones_ref[...]
        kkr = k * kk_ref[...]
        kk = kkr * lax.rsqrt(jnp.maximum(_head_sums(kkr * kkr, ones), 1e-24))
        yield
        kdir = k * (1.0 + (iclr - 1.0) * ka_ref[...])
        bonus_ref[rows, :] = _head_sums(r * kdir * rk_ref[...], ones) * v
        yield
        b = kk * iclr
        lp_incl = _cumsum_dot(tri, lw)
        total = jnp.sum(lw, axis=0, keepdims=True)
        pc_s[wr, cp] = jnp.broadcast_to(jnp.exp(total), (8, RWKV_WIDTH))
        yield
        e_neg = jnp.exp(-lp_incl)
        at_s[wr, rows, :] = (-kk * jnp.exp(lp_incl - lw)).astype(BF16)
        bt_s[wr, rows, :] = (b * e_neg).astype(BF16)
        yield
        kt_s[wr, rows, :] = (kdir * e_neg).astype(BF16)
        rt_s[wr, rows, :] = r * jnp.exp(lp_incl)
        yield
        e_rest = jnp.exp(total - lp_incl)
        bd_s[wr, rows, :] = (b * e_rest).astype(BF16)
        kd_s[wr, rows, :] = (kdir * e_rest).astype(BF16)

    strict, incl, blk16, off32, off64, eye = range(6)
    lane = lax.broadcasted_iota(jnp.int32, (CHUNK, LANES), 1)
    first = lane < HEAD_DIM

    def solve(chunks, tick):
        items = [(c, p) for c in chunks for p in range(PAIRS)]
        rows = [pl.ds(pl.multiple_of(c * CHUNK, CHUNK), CHUNK) for c, _ in items]
        lanes = [slice(p * LANES, (p + 1) * LANES) for _, p in items]
        each = lambda fn, *lists: [fn(*xs) for xs in zip(*lists)]
        load = lambda ref: [_stack_pair(ref[rd, r_, l_], first) for r_, l_ in zip(rows, lanes)]
        at, bt, kt, bd, kd, vv = (load(ref) for ref in (at_s, bt_s, kt_s, bd_s, kd_s, v_s))
        rt = [_stack_pair(rt_s[rd, r_, l_].astype(BF16), first) for r_, l_ in zip(rows, lanes)]
        tick()

        prod = each(lambda a_, r_, b_, k_: _dot_nt(jnp.concatenate([a_, r_], axis=0),
                                                   jnp.concatenate([b_, k_], axis=0)), at, rt, bt, kt)
        a_ak = [(x[:n2, n2:] * mask_s[strict]).astype(BF16) for x in prod]
        q_bk = [jnp.concatenate([x[n2:, :n2] * mask_s[incl], x[n2:, n2:] * mask_s[incl]], axis=1).astype(BF16)
                for x in prod]
        offs = [[(x[:n2, :n2] * mask_s[m]).astype(BF16) for x in prod] for m in (off32, off64)]
        tick()

        d0 = [x[:n2, :n2] * mask_s[blk16] for x in prod]
        t_inv = [mask_s[eye] + x for x in d0]
        pw = [_dot(x, x).astype(BF16) for x in d0]
        tick()
        for _ in range(2):
            both = each(lambda t_, p_: _dot(jnp.concatenate([t_.astype(BF16), p_], axis=0), p_), t_inv, pw)
            t_inv = each(lambda t_, b_: t_ + b_[:n2], t_inv, both)
            pw = [x[n2:].astype(BF16) for x in both]
            tick()
        t_inv = each(lambda t_, p_: t_ + _dot(t_, p_), t_inv, pw)
        tick()
        for off in offs:
            inner = each(lambda o_, t_: _dot(o_, t_).astype(BF16), off, t_inv)
            t_inv = each(lambda t_, i_: t_ + _dot(t_, i_), t_inv, inner)
            tick()

        av = each(lambda a_, v_: _dot(a_, v_).astype(BF16), a_ak, vv)
        tick()
        wu = each(lambda t_, a_, v_: _dot(t_, jnp.concatenate([a_, v_], axis=1)).astype(BF16),
                  t_inv, at, av)
        tick()
        zero = jnp.zeros((n2, LANES), BF16)
        rhs = each(lambda w_, v_: jnp.concatenate(
            [w_, jnp.concatenate([zero, v_], axis=1)], axis=0), wu, vv)
        ry = each(_dot, q_bk, rhs)
        gh = each(lambda b_, k_, x_: _dot_tn(jnp.concatenate([b_, k_], axis=0), x_),
                  bd, kd, rhs)
        tick()

        st = [st_s[p] for p in range(PAIRS)]
        for j, c in enumerate(chunks):
            sl = slice(j * PAIRS, (j + 1) * PAIRS)
            rt_f = [_stack_pair(rt_s[rd, rows[i], lanes[i]], first) for i in range(sl.start, sl.stop)]
            y_st = each(lambda r_, x_, s_: _dot(r_ + x_[:, :LANES], s_) + x_[:, LANES:], rt_f, ry[sl], st)
            gs = each(lambda g_, s_: _dot(g_[:, :LANES], s_), gh[sl], st)
            pc_row = pc_s[rd, c]
            for p in range(PAIRS):
                i = j * PAIRS + p
                y_ref[rows[i], lanes[i]] = y_st[p][:CHUNK] + y_st[p][CHUNK:]
                pc_col = jnp.sum(mask_s[eye] * pc_row[0:1, lanes[i]], axis=1, keepdims=True)
                st[p] = pc_col * st[p] + gs[p] + gh[i][:, LANES:]
        for p in range(PAIRS):
            st_s[p] = st[p]

    def chunk_body(ci, carry):
        first_chunk = ci * SOLVE_CHUNKS
        stages = itertools.chain(*[prepare(first_chunk + j) for j in range(SOLVE_CHUNKS)])

        def tick():
            for _ in range(SOLVE_CHUNKS):
                next(stages, None)

        solve([jnp.where(fwd, first_chunk + j, nc - 1 - first_chunk - j) for j in range(SOLVE_CHUNKS)], tick)
        for _ in stages:
            pass
        return carry

    lax.fori_loop(0, nc // SOLVE_CHUNKS, chunk_body, 0)


def _rwkv(z, batch, seq, w0, w2, a0, a2, k_k, k_a, r_k, ones_bd):
    n_blk = seq // RWKV_TB

    def time_block(d, i):
        return jnp.where(d == 0, i, n_blk - 1 - i)

    def tmap(d, i):
        return time_block(d, jnp.minimum(i, n_blk - 1))

    def smap(d, i):
        return time_block(d, jnp.maximum(i - 1, 0))

    vec = lambda: pl.BlockSpec((1, RWKV_WIDTH), lambda b, d, i: (0, 0))
    in_specs = [
        pl.BlockSpec((None, RWKV_TB, N_RWKV_COLS), lambda b, d, i: (b, tmap(d, i), 0)),
        pl.BlockSpec((None, 1, RWKV_WIDTH), lambda b, d, i: (d, 0, 0)),
        pl.BlockSpec((None, 128, RWKV_WIDTH), lambda b, d, i: (d, 0, 0)),
        pl.BlockSpec((None, 1, RWKV_WIDTH), lambda b, d, i: (d, 0, 0)),
        pl.BlockSpec((None, 128, RWKV_WIDTH), lambda b, d, i: (d, 0, 0)),
        vec(), vec(), vec(),
        pl.BlockSpec((HEAD_GROUP, HEAD_GROUP), lambda b, d, i: (0, 0)),
    ]
    out_spec = lambda m: pl.BlockSpec((None, None, RWKV_TB, RWKV_WIDTH), lambda b, d, i: (d, b, m(d, i), 0))
    tok = lambda dt: pltpu.VMEM((2, RWKV_TB, RWKV_WIDTH), dt)
    return pl.pallas_call(
        _rwkv_kernel,
        out_shape=[jax.ShapeDtypeStruct((2, batch, seq, RWKV_WIDTH), F32)] * 2,
        grid=(batch, 2, n_blk + 1),
        in_specs=in_specs,
        out_specs=[out_spec(smap), out_spec(tmap)],
        scratch_shapes=[tok(BF16), tok(BF16), tok(BF16), tok(F32), tok(BF16), tok(BF16), tok(BF16),
                        pltpu.VMEM((2, RWKV_TB // CHUNK, 8, RWKV_WIDTH), F32),
                        pltpu.VMEM((PAIRS, LANES, LANES), F32),
                        pltpu.VMEM((6, 2 * CHUNK, 2 * CHUNK), F32)],
        compiler_params=_cparams(("parallel", "parallel", "arbitrary")),
        name="rwkv",
    )(z, w0, w2, a0, a2, k_k, k_a, r_k, ones_bd)


MIX_TM = 256


def _mixout_kernel(x_ref, o1, o2, o3, l1, l2, l3, y_ref, bonus_ref, gd_ref, g2_ref, lnw_ref, lnb_ref,
                   avg_ref, spread_ref, wo_ref, out_ref, o_scr):
    n_tiles = ATT_WIDTH // LANES
    o_refs = (o1, o2, o3)
    for k, d in enumerate(DILATIONS):
        for r in range(d if d > 1 else 0):
            for c in range(n_tiles):
                col = r * ATT_WIDTH + c * LANES
                o_scr[k, c, pl.ds(r, MIX_TM // d, stride=d), :] = o_refs[k][:, col:col + LANES].astype(F32)
    m = jnp.maximum(jnp.maximum(l1[...], l2[...]), l3[...])
    e = [jnp.exp(l[...] - m) for l in (l1, l2, l3)]
    inv = 1.0 / (e[0] + e[1] + e[2])
    spread = spread_ref[...]
    att = jnp.zeros((MIX_TM, ATT_WIDTH), F32)
    for k, (d, ep) in enumerate(zip(DILATIONS, e)):
        hi, lo = _split2(ep * inv)
        wts = jnp.dot(hi, spread, preferred_element_type=F32) + jnp.dot(lo, spread, preferred_element_type=F32)
        o_tok = (o_refs[k][...].astype(F32) if d == 1
                 else jnp.concatenate([o_scr[k, c] for c in range(n_tiles)], axis=1))
        att = att + wts * o_tok

    y = y_ref[0] + y_ref[1]
    avg = avg_ref[...]
    yc = y - _head_sums(y, avg)
    var = _head_sums(yc * yc, avg)
    yn = yc * lax.rsqrt(var + RWKV_LN_EPS) * lnw_ref[...] + lnb_ref[...]
    gate = _dot(jax.nn.sigmoid(gd_ref[...]), g2_ref[...])
    rw = (yn + bonus_ref[0] + bonus_ref[1]) * gate

    mixed = jnp.dot(att.astype(BF16), wo_ref[0:ATT_WIDTH, :], preferred_element_type=F32)
    mixed = mixed + jnp.dot(rw.astype(BF16), wo_ref[ATT_WIDTH:, :], preferred_element_type=F32)
    out_ref[...] = x_ref[...] + mixed


def _mixout(x1, os, lses, y, bonus, z, batch, seq, g2, ln_w, ln_b, avg_bd, spread, w_out):
    gcol = GD_COL // 128
    tok = lambda w: pl.BlockSpec((None, MIX_TM, w), lambda b, i: (b, i, 0))
    const = lambda shape: pl.BlockSpec(shape, lambda b, i: (0,) * len(shape))
    dir_spec = lambda: pl.BlockSpec((2, None, MIX_TM, RWKV_WIDTH), lambda b, i: (0, b, i, 0))
    view = lambda d: pl.BlockSpec((None, MIX_TM // d, d * ATT_WIDTH), lambda b, i: (b, i, 0))
    in_specs = [tok(D_MODEL)] + [view(d) for d in DILATIONS] + [tok(LANES)] * 3 + [dir_spec(), dir_spec()] + [
        pl.BlockSpec((None, MIX_TM, 128), lambda b, i: (b, i, gcol)),
        const((128, RWKV_WIDTH)),
        const((1, RWKV_WIDTH)), const((1, RWKV_WIDTH)), const((HEAD_GROUP, HEAD_GROUP)),
        const((LANES, ATT_WIDTH)), const((D_MODEL, D_MODEL)),
    ]
    return pl.pallas_call(
        _mixout_kernel,
        out_shape=jax.ShapeDtypeStruct((batch, seq, D_MODEL), F32),
        grid=(batch, seq // MIX_TM),
        in_specs=in_specs,
        out_specs=tok(D_MODEL),
        scratch_shapes=[pltpu.VMEM((len(DILATIONS), ATT_WIDTH // LANES, MIX_TM, LANES), F32)],
        compiler_params=_cparams(("parallel", "arbitrary")),
        name="mixout",
    )(x1, *os, *lses, y, bonus, z, g2, ln_w, ln_b, avg_bd, spread, w_out)


def _block_diag(width, block, value):
    idx = jnp.arange(width) // block
    return jnp.where(idx[:, None] == idx[None, :], value, 0.0).astype(BF16)


def _trunk(x, p):
    batch, seq, _ = x.shape
    xf = x.reshape(batch * seq, D_MODEL)
    x1 = _ffn(xf, p["ffn1_norm"], p["ffn1_gate"], p["ffn1_up"], p["ffn1_down"], p["final_norm"],
              final_norm=False)
    views = _inproj_views(x1, p["mix_norm"], p["w_in_att"])
    z = _inproj_shift(x1, p["mix_norm"], p["w_in_rwkv"], p["mu_prev"], p["mu_next"], seq,
                      tn=1152).reshape(batch, seq, N_RWKV_COLS)
    os, lses = [], []
    for dil, view in zip(DILATIONS, views):
        o, lse = _attention(view.reshape(3, batch, seq // dil, dil * ATT_WIDTH), p["slopes"], batch, seq, dil)
        os.append(o)
        lses.append(lse)
    y, bonus = _rwkv(z, batch, seq, p["w0"], p["w2"], p["a0"], p["a2"],
                     p["k_k"], p["k_a"], p["r_k"], p["ones_bd"])
    x2 = _mixout(x1.reshape(batch, seq, D_MODEL), os, lses, y, bonus, z, batch, seq,
                 p["g2"], p["ln_x_w"], p["ln_x_b"], p["avg_bd"], p["spread"], p["w_out"])
    out = _ffn(x2.reshape(batch * seq, D_MODEL), p["ffn2_norm"], p["ffn2_gate"], p["ffn2_up"],
               p["ffn2_down"], p["final_norm"], final_norm=True)
    return out.reshape(batch, seq, D_MODEL)


def kernel(x_prompt, x_sample, ffn1_norm, ffn1_gate, ffn1_up, ffn1_down, mix_norm, w_in, w_out, mu_prev, mu_next, w0_f, w2_f, w0_b, w2_b, a0_f, a2_f, a0_b, a2_b, g2, k_k, k_a, r_k, ln_x_w, ln_x_b, ffn2_norm, ffn2_gate, ffn2_up, ffn2_down, final_norm):
    assert ffn1_norm.shape[0] == 1, "single layer"
    row = lambda t: t.reshape(1, -1)
    zero64 = jnp.zeros((64, RWKV_WIDTH), F32)
    w_in_b = w_in[0].astype(BF16)
    p = {
        "ffn1_norm": row(ffn1_norm[0]), "ffn1_gate": ffn1_gate[0].astype(BF16),
        "ffn1_up": ffn1_up[0].astype(BF16), "ffn1_down": ffn1_down[0].astype(BF16),
        "mix_norm": row(mix_norm[0]),
        "w_in_att": w_in_b[:, :N_ATT_COLS], "w_in_rwkv": w_in_b[:, N_ATT_COLS:],
        "w_out": w_out[0].astype(BF16),
        "mu_prev": row(mu_prev[0]), "mu_next": row(mu_next[0]),
        "w0": jnp.stack([row(w0_f[0]), row(w0_b[0])]),
        "w2": jnp.stack([jnp.concatenate([w2_f[0], zero64]), jnp.concatenate([zero64, w2_b[0]])]).astype(BF16),
        "a0": jnp.stack([row(a0_f[0]), row(a0_b[0])]),
        "a2": jnp.stack([jnp.concatenate([a2_f[0], zero64]), jnp.concatenate([zero64, a2_b[0]])]).astype(BF16),
        "g2": g2[0].astype(BF16),
        "k_k": row(k_k[0]), "k_a": row(k_a[0]), "r_k": row(r_k[0]),
        "ln_x_w": row(ln_x_w[0]), "ln_x_b": row(ln_x_b[0]),
        "ffn2_norm": row(ffn2_norm[0]), "ffn2_gate": ffn2_gate[0].astype(BF16),
        "ffn2_up": ffn2_up[0].astype(BF16), "ffn2_down": ffn2_down[0].astype(BF16),
        "final_norm": row(final_norm),
        "slopes": jnp.exp2(-8.0 * jnp.arange(1, ATT_HEADS + 1, dtype=F32) / ATT_HEADS),
        "ones_bd": _block_diag(HEAD_GROUP, HEAD_DIM, 1.0),
        "avg_bd": _block_diag(HEAD_GROUP, HEAD_DIM, 1.0 / HEAD_DIM),
        "spread": (jnp.arange(LANES)[:, None] == jnp.arange(ATT_WIDTH)[None, :] // HEAD_DIM).astype(BF16),
    }
    return (_trunk(x_prompt, p), _trunk(x_sample, p))
```

```python
import functools
import itertools

import jax
import jax.numpy as jnp
from jax import lax
from jax.experimental import pallas as pl
from jax.experimental.pallas import tpu as pltpu

F32 = jnp.float32
BF16 = jnp.bfloat16

D_MODEL = 2048
D_FF = 5632
HEAD_DIM = 64
ATT_WIDTH = 1024
ATT_HEADS = 16
RWKV_WIDTH = 1024
N_ATT_COLS = 3 * ATT_WIDTH
N_RWKV_COLS = 3 * RWKV_WIDTH + 64 * 4 + 128
WD_COL = 3 * RWKV_WIDTH
AD_COL = WD_COL + 128
GD_COL = AD_COL + 128
DILATIONS = (1, 4, 16)
BAND_HALF = 64
NORM_EPS = 1e-6
RWKV_LN_EPS = 64e-5
NEG_INF = -1e30

LANES = 128
HEAD_GROUP = 256
PROJ_CHUNK = 256
CHUNK = 64
VMEM_LIMIT = 56 * 1024 * 1024


def _cparams(sem):
    return pltpu.CompilerParams(dimension_semantics=sem, vmem_limit_bytes=VMEM_LIMIT)


def _dot(a, b):
    return jnp.dot(a.astype(BF16), b.astype(BF16), preferred_element_type=F32)


def _dot_nt(a, b):
    return lax.dot_general(a.astype(BF16), b.astype(BF16), (((1,), (1,)), ((), ())),
                           preferred_element_type=F32)


def _dot_tn(a, b):
    return lax.dot_general(a.astype(BF16), b.astype(BF16), (((0,), (0,)), ((), ())),
                           preferred_element_type=F32)


def _split2(x):
    hi = x.astype(BF16)
    lo = (x - hi.astype(F32)).astype(BF16)
    return hi, lo


def _head_sums(x, seg):
    hi, lo = _split2(x)
    outs = []
    for g in range(x.shape[1] // HEAD_GROUP):
        cols = slice(g * HEAD_GROUP, (g + 1) * HEAD_GROUP)
        outs.append(jnp.dot(hi[:, cols], seg, preferred_element_type=F32)
                    + jnp.dot(lo[:, cols], seg, preferred_element_type=F32))
    return jnp.concatenate(outs, axis=1)


def _cumsum_dot(tri_bf16, x):
    hi, lo = _split2(x)
    return (jnp.dot(tri_bf16, hi, preferred_element_type=F32)
            + jnp.dot(tri_bf16, lo, preferred_element_type=F32))


def _rmsnorm(x, g):
    return x * lax.rsqrt(jnp.mean(x * x, axis=-1, keepdims=True) + NORM_EPS) * g


def _ffn_kernel(x_ref, g_ref, wg_ref, wu_ref, wd_ref, fg_ref, o_ref, n_scr, *, final_norm):
    f = pl.program_id(1)

    @pl.when(f == 0)
    def _():
        x = x_ref[...]
        n_scr[...] = _rmsnorm(x, g_ref[...]).astype(BF16)
        o_ref[...] = x

    n = n_scr[...]
    hg = jnp.dot(n, wg_ref[...], preferred_element_type=F32)
    hu = jnp.dot(n, wu_ref[...], preferred_element_type=F32)
    h = (hg * jax.nn.sigmoid(hg) * (0.5 * hu)).astype(BF16)
    o_ref[...] += jnp.dot(h, wd_ref[...], preferred_element_type=F32)

    if final_norm:
        @pl.when(f == pl.num_programs(1) - 1)
        def _():
            o_ref[...] = _rmsnorm(o_ref[...], fg_ref[...])


def _ffn(x, g, wg, wu, wd, fg, *, final_norm, tm=1024, tf=512):
    n_tok = x.shape[0]
    return pl.pallas_call(
        functools.partial(_ffn_kernel, final_norm=final_norm),
        out_shape=jax.ShapeDtypeStruct((n_tok, D_MODEL), F32),
        grid=(n_tok // tm, D_FF // tf),
        in_specs=[
            pl.BlockSpec((tm, D_MODEL), lambda i, f: (i, 0)),
            pl.BlockSpec((1, D_MODEL), lambda i, f: (0, 0)),
            pl.BlockSpec((D_MODEL, tf), lambda i, f: (0, f)),
            pl.BlockSpec((D_MODEL, tf), lambda i, f: (0, f)),
            pl.BlockSpec((tf, D_MODEL), lambda i, f: (f, 0)),
            pl.BlockSpec((1, D_MODEL), lambda i, f: (0, 0)),
        ],
        out_specs=pl.BlockSpec((tm, D_MODEL), lambda i, f: (i, 0)),
        scratch_shapes=[pltpu.VMEM((tm, D_MODEL), BF16)],
        compiler_params=_cparams(("parallel", "arbitrary")),
        name="ffn_final" if final_norm else "ffn",
    )(x, g, wg, wu, wd, fg)


def _inproj_shift_kernel(x_ref, xp_ref, xn_ref, g_ref, w_ref, mup_ref, mun_ref, o_ref, n_scr, *, seq):
    i = pl.program_id(0)
    tm = x_ref.shape[0]

    @pl.when(pl.program_id(1) == 0)
    def _():
        g = g_ref[...]
        n_scr[0:tm, :] = _rmsnorm(x_ref[...], g).astype(BF16)
        halo = jnp.concatenate([xp_ref[...], xn_ref[...]], axis=0)
        n_scr[tm:, :] = _rmsnorm(halo, g).astype(BF16)

    n = n_scr[...]
    first_of_seq = (i * tm) % seq == 0
    last_of_seq = ((i + 1) * tm) % seq == 0
    tn = o_ref.shape[1]
    chunks = [slice(c0, min(c0 + PROJ_CHUNK, tn)) for c0 in range(0, tn, PROJ_CHUNK)]

    def project(cols):
        z_all = jnp.dot(n, w_ref[:, cols], preferred_element_type=F32)
        return z_all[:tm], z_all[tm:]

    def finish(cols, z, z_halo):
        prev_row = jnp.where(first_of_seq, 0.0, z_halo[7:8, :])
        next_row = jnp.where(last_of_seq, 0.0, z_halo[8:9, :])
        o_ref[:, cols] = _shifted(z, prev_row, next_row, mup_ref[:, cols], mun_ref[:, cols])

    pending = None
    for cols in chunks:
        done = project(cols)
        if pending is not None:
            finish(*pending)
        pending = (cols,) + done
    finish(*pending)


def _inproj_shift(x, g, w, mu_prev, mu_next, seq, *, tn, tm=1024):
    n_tok = x.shape[0]
    n_cols = w.shape[1]
    hb = tm // 8
    return pl.pallas_call(
        functools.partial(_inproj_shift_kernel, seq=seq),
        out_shape=jax.ShapeDtypeStruct((n_tok, n_cols), F32),
        grid=(n_tok // tm, n_cols // tn),
        in_specs=[
            pl.BlockSpec((tm, D_MODEL), lambda i, j: (i, 0)),
            pl.BlockSpec((8, D_MODEL), lambda i, j: (jnp.maximum(i * hb - 1, 0), 0)),
            pl.BlockSpec((8, D_MODEL), lambda i, j: (jnp.minimum((i + 1) * hb, n_tok // 8 - 1), 0)),
            pl.BlockSpec((1, D_MODEL), lambda i, j: (0, 0)),
            pl.BlockSpec((D_MODEL, tn), lambda i, j: (0, j)),
            pl.BlockSpec((1, tn), lambda i, j: (0, j)),
            pl.BlockSpec((1, tn), lambda i, j: (0, j)),
        ],
        out_specs=pl.BlockSpec((tm, tn), lambda i, j: (i, j)),
        scratch_shapes=[pltpu.VMEM((tm + 16, D_MODEL), BF16)],
        compiler_params=_cparams(("parallel", "arbitrary")),
        name="inproj_shift",
    )(x, x, x, g, w, mu_prev, mu_next)


def _inproj_views_kernel(x_ref, g_ref, w_ref, *refs):
    out_refs, (n_scr, acc_scr) = refs[:len(DILATIONS)], refs[len(DILATIONS):]

    @pl.when(pl.program_id(1) == 0)
    def _():
        n_scr[...] = _rmsnorm(x_ref[...], g_ref[...]).astype(BF16)

    n = n_scr[...]
    n_tiles, tm, _ = acc_scr.shape

    def finish(c0, acc):
        out_refs[0][:, c0:c0 + PROJ_CHUNK] = acc.astype(BF16)
        tiles = range(c0 // LANES, (c0 + PROJ_CHUNK) // LANES)
        for c in tiles:
            acc_scr[c] = acc[:, c * LANES - c0:(c + 1) * LANES - c0]
        for d, o_ref in zip(DILATIONS[1:], out_refs[1:]):
            for r in range(d):
                for c in tiles:
                    col = r * ATT_WIDTH + c * LANES
                    o_ref[:, col:col + LANES] = acc_scr[c, pl.ds(r, tm // d, stride=d), :].astype(BF16)

    pending = None
    for c0 in range(0, ATT_WIDTH, PROJ_CHUNK):
        acc = jnp.dot(n, w_ref[:, c0:c0 + PROJ_CHUNK], preferred_element_type=F32)
        if pending is not None:
            finish(*pending)
        pending = (c0, acc)
    finish(*pending)


def _inproj_views(x, g, w, *, tm=1024):
    n_tok = x.shape[0]
    return pl.pallas_call(
        _inproj_views_kernel,
        out_shape=[jax.ShapeDtypeStruct((3, n_tok // d, d * ATT_WIDTH), BF16) for d in DILATIONS],
        grid=(n_tok // tm, 3),
        in_specs=[
            pl.BlockSpec((tm, D_MODEL), lambda i, j: (i, 0)),
            pl.BlockSpec((1, D_MODEL), lambda i, j: (0, 0)),
            pl.BlockSpec((D_MODEL, ATT_WIDTH), lambda i, j: (0, j)),
        ],
        out_specs=[pl.BlockSpec((None, tm // d, d * ATT_WIDTH), lambda i, j: (j, i, 0)) for d in DILATIONS],
        scratch_shapes=[pltpu.VMEM((tm, D_MODEL), BF16), pltpu.VMEM((ATT_WIDTH // LANES, tm, LANES), F32)],
        compiler_params=_cparams(("parallel", "arbitrary")),
        name="inproj_views",
    )(x, g, w)


ATT_TQ = 128
ATT_TK = 64
ATT_GROUP = 8


def _attn_kernel(q_ref, k0, k1, k2, k3, v0, v1, v2, v3, bias_ref, o_ref, lse_ref, kc_scr, vc_scr):
    i = pl.program_id(2)
    for j, (kr, vr) in enumerate(((k0, v0), (k1, v1), (k2, v2), (k3, v3))):
        kc_scr[j * ATT_TK:(j + 1) * ATT_TK, :] = kr[...]
        vc_scr[j * ATT_TK:(j + 1) * ATT_TK, :] = vr[...]

    case = jnp.where(i == 0, 1, jnp.where(i == pl.num_programs(2) - 1, 2, 0))
    lane = lax.broadcasted_iota(jnp.int32, (ATT_TQ, LANES), 1)
    first = lane < HEAD_DIM

    heads = range(ATT_HEADS)
    cols = [slice((h // 2) * LANES, (h // 2 + 1) * LANES) for h in heads]
    sel = [first if h % 2 == 0 else jnp.logical_not(first) for h in heads]
    scale = jnp.asarray(HEAD_DIM ** -0.5, BF16)
    o, lse = [], []
    for g in range(0, ATT_HEADS, ATT_GROUP):
        grp = range(g, g + ATT_GROUP)
        qm = [jnp.where(sel[h], q_ref[:, cols[h]] * scale, jnp.zeros((ATT_TQ, LANES), BF16)) for h in grp]
        s = [_dot_nt(q_, kc_scr[:, cols[h]]) + bias_ref[case, h] for q_, h in zip(qm, grp)]
        m = [jnp.max(x, axis=-1, keepdims=True) for x in s]
        p = [jnp.exp(x - mx) for x, mx in zip(s, m)]
        l = [jnp.sum(x, axis=-1, keepdims=True) for x in p]
        o += [jnp.dot(p_.astype(BF16), vc_scr[:, cols[h]], preferred_element_type=F32) * (1.0 / l_)
              for p_, l_, h in zip(p, l, grp)]
        lse += [mx + jnp.log(lx) for mx, lx in zip(m, l)]
    for h in range(0, ATT_HEADS, 2):
        o_ref[:, cols[h]] = jnp.where(first, o[h], o[h + 1]).astype(o_ref.dtype)
    lse_tile = jnp.zeros((ATT_TQ, LANES), F32)
    for h in heads:
        lse_tile = jnp.where(lane == h, lse[h], lse_tile)
    lse_ref[...] = lse_tile


def _attention_bias(slopes, dil):
    nk = 4 * ATT_TK
    row = jnp.arange(ATT_TQ)[:, None]
    col = jnp.arange(nk)[None, :]
    rel = col - BAND_HALF - row
    in_band = jnp.abs(rel) <= BAND_HALF
    dist = (jnp.abs(rel) * dil).astype(F32)
    inside = jnp.stack([in_band, in_band & (col >= BAND_HALF), in_band & (col < BAND_HALF + ATT_TQ)])
    return jnp.where(inside[:, None], -slopes[None, :, None, None] * dist, NEG_INF)


def _attention(view, slopes, batch, seq, dil):
    L = seq // dil
    assert L // ATT_TQ >= 2, "first and last query block must differ"
    nkb = L // ATT_TK
    ratio = ATT_TQ // ATT_TK
    bias = _attention_bias(slopes, dil)

    def kv_spec(which, j):
        def imap(b, r, i):
            blk = jnp.clip(i * ratio - 1 + j, 0, nkb - 1)
            return (which, b, blk, r)
        return pl.BlockSpec((None, None, ATT_TK, ATT_WIDTH), imap)

    o, lse = pl.pallas_call(
        _attn_kernel,
        out_shape=[jax.ShapeDtypeStruct((batch, L, dil * ATT_WIDTH), BF16),
                   jax.ShapeDtypeStruct((batch, dil, L, LANES), F32)],
        grid=(batch, dil, L // ATT_TQ),
        in_specs=[pl.BlockSpec((None, None, ATT_TQ, ATT_WIDTH), lambda b, r, i: (0, b, i, r))]
        + [kv_spec(1, j) for j in range(4)] + [kv_spec(2, j) for j in range(4)]
        + [pl.BlockSpec(bias.shape, lambda b, r, i: (0, 0, 0, 0))],
        out_specs=[pl.BlockSpec((None, ATT_TQ, ATT_WIDTH), lambda b, r, i: (b, i, r)),
                   pl.BlockSpec((None, None, ATT_TQ, LANES), lambda b, r, i: (b, r, i, 0))],
        scratch_shapes=[pltpu.VMEM((4 * ATT_TK, ATT_WIDTH), BF16),
                        pltpu.VMEM((4 * ATT_TK, ATT_WIDTH), BF16)],
        compiler_params=_cparams(("parallel", "parallel", "arbitrary")),
        name=f"attn_d{dil}",
    )(view, *([view] * 8), bias)
    return o, lse.transpose(0, 2, 1, 3).reshape(batch, seq, LANES)


RWKV_TB = 256
PAIRS = RWKV_WIDTH // LANES
SOLVE_CHUNKS = 2


def _shifted(z, prev_row, next_row, mu_prev, mu_next):
    n = z.shape[0]
    ridx = lax.broadcasted_iota(jnp.int32, z.shape, 0)
    z_prev = jnp.where(ridx == 0, prev_row, pltpu.roll(z, 1, 0))
    z_next = jnp.where(ridx == n - 1, next_row, pltpu.roll(z, n - 1, 0))
    return z + mu_prev * (z_prev - z) + mu_next * (z_next - z)


def _stack_pair(x, first):
    zero = jnp.zeros_like(x)
    return jnp.concatenate([jnp.where(first, x, zero), jnp.where(first, zero, x)], axis=0)


def _rwkv_kernel(z_ref, w0_ref, w2_ref, a0_ref, a2_ref, kk_ref, ka_ref, rk_ref, ones_ref, y_ref, bonus_ref,
                 at_s, bt_s, kt_s, rt_s, bd_s, kd_s, v_s, pc_s, st_s, mask_s):
    d = pl.program_id(1)
    step = pl.program_id(2)
    nc = RWKV_TB // CHUNK
    fwd = d == 0
    sgn = jnp.where(fwd, 1, -1)
    wr = step % 2
    rd = 1 - wr
    n2 = 2 * CHUNK

    @pl.when(step == 0)
    def _():
        st_s[...] = jnp.zeros_like(st_s)
        for ref in (at_s, bt_s, kt_s, rt_s, bd_s, kd_s, v_s, pc_s):
            ref[1] = jnp.zeros(ref.shape[1:], ref.dtype)
        rr = lax.broadcasted_iota(jnp.int32, (n2, n2), 0)
        cc = lax.broadcasted_iota(jnp.int32, (n2, n2), 1)
        same_head = (rr // CHUNK) == (cc // CHUNK)
        tt = rr % CHUNK
        ss = cc % CHUNK
        strict = same_head & ((tt - ss) * sgn > 0)
        masks = (strict,
                 same_head & ((tt - ss) * sgn >= 0),
                 strict & ((tt // 16) == (ss // 16)),
                 strict & ((tt // 32) == (ss // 32)) & ((tt // 16) != (ss // 16)),
                 strict & ((tt // 32) != (ss // 32)),
                 rr == cc)
        for idx, msk in enumerate(masks):
            mask_s[idx] = jnp.where(msk, 1.0, 0.0)

    ti = lax.broadcasted_iota(jnp.int32, (CHUNK, CHUNK), 0)
    si = lax.broadcasted_iota(jnp.int32, (CHUNK, CHUNK), 1)
    tri = jnp.where((ti - si) * sgn >= 0, 1.0, 0.0).astype(BF16)

    def prepare(cp):
        rows = pl.ds(pl.multiple_of(cp * CHUNK, CHUNK), CHUNK)
        r = z_ref[rows, 0:RWKV_WIDTH]
        k = z_ref[rows, RWKV_WIDTH:2 * RWKV_WIDTH]
        v = z_ref[rows, 2 * RWKV_WIDTH:3 * RWKV_WIDTH]
        v_s[wr, rows, :] = v.astype(BF16)
        yield
        wd = z_ref[rows, WD_COL:WD_COL + 128]
        ad = z_ref[rows, AD_COL:AD_COL + 128]
        u = -(w0_ref[...] + _dot(jnp.tanh(wd), w2_ref[...]))
        softplus = jnp.maximum(u, 0.0) + jnp.log(1.0 + jnp.exp(-jnp.abs(u)))
        lw = -jnp.exp(-softplus - 0.5)
        yield
        iclr = jax.nn.sigmoid(a0_ref[...] + _dot(ad, a2_ref[...]))
        ones = ones_ref[...]
        kkr = k * kk_ref[...]
        kk = kkr * lax.rsqrt(jnp.maximum(_head_sums(kkr * kkr, ones), 1e-24))
        yield
        kdir = k * (1.0 + (iclr - 1.0) * ka_ref[...])
        bonus_ref[rows, :] = _head_sums(r * kdir * rk_ref[...], ones) * v
        yield
        b = kk * iclr
        lp_incl = _cumsum_dot(tri, lw)
        total = jnp.sum(lw, axis=0, keepdims=True)
        pc_s[wr, cp] = jnp.broadcast_to(jnp.exp(total), (8, RWKV_WIDTH))
        yield
        e_neg = jnp.exp(-lp_incl)
        at_s[wr, rows, :] = (-kk * jnp.exp(lp_incl - lw)).astype(BF16)
        bt_s[wr, rows, :] = (b * e_neg).astype(BF16)
        yield
        kt_s[wr, rows, :] = (kdir * e_neg).astype(BF16)
        rt_s[wr, rows, :] = r * jnp.exp(lp_incl)
        yield
        e_rest = jnp.exp(total - lp_incl)
        bd_s[wr, rows, :] = (b * e_rest).astype(BF16)
        kd_s[wr, rows, :] = (kdir * e_rest).astype(BF16)

    strict, incl, blk16, off32, off64, eye = range(6)
    lane = lax.broadcasted_iota(jnp.int32, (CHUNK, LANES), 1)
    first = lane < HEAD_DIM

    def solve(chunks, tick):
        items = [(c, p) for c in chunks for p in range(PAIRS)]
        rows = [pl.ds(pl.multiple_of(c * CHUNK, CHUNK), CHUNK) for c, _ in items]
        lanes = [slice(p * LANES, (p + 1) * LANES) for _, p in items]
        each = lambda fn, *lists: [fn(*xs) for xs in zip(*lists)]
        load = lambda ref: [_stack_pair(ref[rd, r_, l_], first) for r_, l_ in zip(rows, lanes)]
        at, bt, kt, bd, kd, vv = (load(ref) for ref in (at_s, bt_s, kt_s, bd_s, kd_s, v_s))
        rt = [_stack_pair(rt_s[rd, r_, l_].astype(BF16), first) for r_, l_ in zip(rows, lanes)]
        tick()

        prod = each(lambda a_, r_, b_, k_: _dot_nt(jnp.concatenate([a_, r_], axis=0),
                                                   jnp.concatenate([b_, k_], axis=0)), at, rt, bt, kt)
        a_ak = [(x[:n2, n2:] * mask_s[strict]).astype(BF16) for x in prod]
        q_bk = [jnp.concatenate([x[n2:, :n2] * mask_s[incl], x[n2:, n2:] * mask_s[incl]], axis=1).astype(BF16)
                for x in prod]
        offs = [[(x[:n2, :n2] * mask_s[m]).astype(BF16) for x in prod] for m in (off32, off64)]
        tick()

        d0 = [x[:n2, :n2] * mask_s[blk16] for x in prod]
        t_inv = [mask_s[eye] + x for x in d0]
        pw = [_dot(x, x).astype(BF16) for x in d0]
        tick()
        for _ in range(2):
            both = each(lambda t_, p_: _dot(jnp.concatenate([t_.astype(BF16), p_], axis=0), p_), t_inv, pw)
            t_inv = each(lambda t_, b_: t_ + b_[:n2], t_inv, both)
            pw = [x[n2:].astype(BF16) for x in both]
            tick()
        t_inv = each(lambda t_, p_: t_ + _dot(t_, p_), t_inv, pw)
        tick()
        for off in offs:
            inner = each(lambda o_, t_: _dot(o_, t_).astype(BF16), off, t_inv)
            t_inv = each(lambda t_, i_: t_ + _dot(t_, i_), t_inv, inner)
            tick()

        av = each(lambda a_, v_: _dot(a_, v_).astype(BF16), a_ak, vv)
        tick()
        wu = each(lambda t_, a_, v_: _dot(t_, jnp.concatenate([a_, v_], axis=1)).astype(BF16),
                  t_inv, at, av)
        tick()
        zero = jnp.zeros((n2, LANES), BF16)
        rhs = each(lambda w_, v_: jnp.concatenate(
            [w_, jnp.concatenate([zero, v_], axis=1)], axis=0), wu, vv)
        ry = each(_dot, q_bk, rhs)
        gh = each(lambda b_, k_, x_: _dot_tn(jnp.concatenate([b_, k_], axis=0), x_),
                  bd, kd, rhs)
        tick()

        st = [st_s[p] for p in range(PAIRS)]
        for j, c in enumerate(chunks):
            sl = slice(j * PAIRS, (j + 1) * PAIRS)
            rt_f = [_stack_pair(rt_s[rd, rows[i], lanes[i]], first) for i in range(sl.start, sl.stop)]
            y_st = each(lambda r_, x_, s_: _dot(r_ + x_[:, :LANES], s_) + x_[:, LANES:], rt_f, ry[sl], st)
            gs = each(lambda g_, s_: _dot(g_[:, :LANES], s_), gh[sl], st)
            pc_row = pc_s[rd, c]
            for p in range(PAIRS):
                i = j * PAIRS + p
                y_ref[rows[i], lanes[i]] = y_st[p][:CHUNK] + y_st[p][CHUNK:]
                pc_col = jnp.sum(mask_s[eye] * pc_row[0:1, lanes[i]], axis=1, keepdims=True)
                st[p] = pc_col * st[p] + gs[p] + gh[i][:, LANES:]
        for p in range(PAIRS):
            st_s[p] = st[p]

    def chunk_body(ci, carry):
        first_chunk = ci * SOLVE_CHUNKS
        stages = itertools.chain(*[prepare(first_chunk + j) for j in range(SOLVE_CHUNKS)])

        def tick():
            for _ in range(SOLVE_CHUNKS):
                next(stages, None)

        solve([jnp.where(fwd, first_chunk + j, nc - 1 - first_chunk - j) for j in range(SOLVE_CHUNKS)], tick)
        for _ in stages:
            pass
        return carry

    lax.fori_loop(0, nc // SOLVE_CHUNKS, chunk_body, 0)


def _rwkv(z, batch, seq, w0, w2, a0, a2, k_k, k_a, r_k, ones_bd):
    n_blk = seq // RWKV_TB

    def time_block(d, i):
        return jnp.where(d == 0, i, n_blk - 1 - i)

    def tmap(d, i):
        return time_block(d, jnp.minimum(i, n_blk - 1))

    def smap(d, i):
        return time_block(d, jnp.maximum(i - 1, 0))

    vec = lambda: pl.BlockSpec((1, RWKV_WIDTH), lambda b, d, i: (0, 0))
    in_specs = [
        pl.BlockSpec((None, RWKV_TB, N_RWKV_COLS), lambda b, d, i: (b, tmap(d, i), 0)),
        pl.BlockSpec((None, 1, RWKV_WIDTH), lambda b, d, i: (d, 0, 0)),
        pl.BlockSpec((None, 128, RWKV_WIDTH), lambda b, d, i: (d, 0, 0)),
        pl.BlockSpec((None, 1, RWKV_WIDTH), lambda b, d, i: (d, 0, 0)),
        pl.BlockSpec((None, 128, RWKV_WIDTH), lambda b, d, i: (d, 0, 0)),
        vec(), vec(), vec(),
        pl.BlockSpec((HEAD_GROUP, HEAD_GROUP), lambda b, d, i: (0, 0)),
    ]
    out_spec = lambda m: pl.BlockSpec((None, None, RWKV_TB, RWKV_WIDTH), lambda b, d, i: (d, b, m(d, i), 0))
    tok = lambda dt: pltpu.VMEM((2, RWKV_TB, RWKV_WIDTH), dt)
    return pl.pallas_call(
        _rwkv_kernel,
        out_shape=[jax.ShapeDtypeStruct((2, batch, seq, RWKV_WIDTH), F32)] * 2,
        grid=(batch, 2, n_blk + 1),
        in_specs=in_specs,
        out_specs=[out_spec(smap), out_spec(tmap)],
        scratch_shapes=[tok(BF16), tok(BF16), tok(BF16), tok(F32), tok(BF16), tok(BF16), tok(BF16),
                        pltpu.VMEM((2, RWKV_TB // CHUNK, 8, RWKV_WIDTH), F32),
                        pltpu.VMEM((PAIRS, LANES, LANES), F32),
                        pltpu.VMEM((6, 2 * CHUNK, 2 * CHUNK), F32)],
        compiler_params=_cparams(("parallel", "parallel", "arbitrary")),
        name="rwkv",
    )(z, w0, w2, a0, a2, k_k, k_a, r_k, ones_bd)


MIX_TM = 256


def _mixout_kernel(x_ref, o1, o2, o3, l1, l2, l3, y_ref, bonus_ref, gd_ref, g2_ref, lnw_ref, lnb_ref,
                   avg_ref, spread_ref, wo_ref, out_ref, o_scr):
    n_tiles = ATT_WIDTH // LANES
    o_refs = (o1, o2, o3)
    for k, d in enumerate(DILATIONS):
        for r in range(d if d > 1 else 0):
            for c in range(n_tiles):
                col = r * ATT_WIDTH + c * LANES
                o_scr[k, c, pl.ds(r, MIX_TM // d, stride=d), :] = o_refs[k][:, col:col + LANES].astype(F32)
    m = jnp.maximum(jnp.maximum(l1[...], l2[...]), l3[...])
    e = [jnp.exp(l[...] - m) for l in (l1, l2, l3)]
    inv = 1.0 / (e[0] + e[1] + e[2])
    spread2 = jnp.concatenate([spread_ref[...]] * 2, axis=0)
    att = jnp.zeros((MIX_TM, ATT_WIDTH), F32)
    for k, (d, ep) in enumerate(zip(DILATIONS, e)):
        wts = jnp.dot(jnp.concatenate(_split2(ep * inv), axis=1), spread2, preferred_element_type=F32)
        o_tok = (o_refs[k][...].astype(F32) if d == 1
                 else jnp.concatenate([o_scr[k, c] for c in range(n_tiles)], axis=1))
        att = att + wts * o_tok

    y = y_ref[0] + y_ref[1]
    avg = avg_ref[...]
    yc = y - _head_sums(y, avg)
    var = _head_sums(yc * yc, avg)
    yn = yc * lax.rsqrt(var + RWKV_LN_EPS) * lnw_ref[...] + lnb_ref[...]
    gate = _dot(jax.nn.sigmoid(gd_ref[...]), g2_ref[...])
    rw = (yn + bonus_ref[0] + bonus_ref[1]) * gate

    mixed = jnp.dot(att.astype(BF16), wo_ref[0:ATT_WIDTH, :], preferred_element_type=F32)
    mixed = mixed + jnp.dot(rw.astype(BF16), wo_ref[ATT_WIDTH:, :], preferred_element_type=F32)
    out_ref[...] = x_ref[...] + mixed


def _mixout(x1, os, lses, y, bonus, z, batch, seq, g2, ln_w, ln_b, avg_bd, spread, w_out):
    gcol = GD_COL // 128
    tok = lambda w: pl.BlockSpec((None, MIX_TM, w), lambda b, i: (b, i, 0))
    const = lambda shape: pl.BlockSpec(shape, lambda b, i: (0,) * len(shape))
    dir_spec = lambda: pl.BlockSpec((2, None, MIX_TM, RWKV_WIDTH), lambda b, i: (0, b, i, 0))
    view = lambda d: pl.BlockSpec((None, MIX_TM // d, d * ATT_WIDTH), lambda b, i: (b, i, 0))
    in_specs = [tok(D_MODEL)] + [view(d) for d in DILATIONS] + [tok(LANES)] * 3 + [dir_spec(), dir_spec()] + [
        pl.BlockSpec((None, MIX_TM, 128), lambda b, i: (b, i, gcol)),
        const((128, RWKV_WIDTH)),
        const((1, RWKV_WIDTH)), const((1, RWKV_WIDTH)), const((HEAD_GROUP, HEAD_GROUP)),
        const((LANES, ATT_WIDTH)), const((D_MODEL, D_MODEL)),
    ]
    return pl.pallas_call(
        _mixout_kernel,
        out_shape=jax.ShapeDtypeStruct((batch, seq, D_MODEL), F32),
        grid=(batch, seq // MIX_TM),
        in_specs=in_specs,
        out_specs=tok(D_MODEL),
        scratch_shapes=[pltpu.VMEM((len(DILATIONS), ATT_WIDTH // LANES, MIX_TM, LANES), F32)],
        compiler_params=_cparams(("parallel", "arbitrary")),
        name="mixout",
    )(x1, *os, *lses, y, bonus, z, g2, ln_w, ln_b, avg_bd, spread, w_out)


def _block_diag(width, block, value):
    idx = jnp.arange(width) // block
    return jnp.where(idx[:, None] == idx[None, :], value, 0.0).astype(BF16)


def _trunk(x, p):
    batch, seq, _ = x.shape
    xf = x.reshape(batch * seq, D_MODEL)
    x1 = _ffn(xf, p["ffn1_norm"], p["ffn1_gate"], p["ffn1_up"], p["ffn1_down"], p["final_norm"],
              final_norm=False)
    views = _inproj_views(x1, p["mix_norm"], p["w_in_att"])
    z = _inproj_shift(x1, p["mix_norm"], p["w_in_rwkv"], p["mu_prev"], p["mu_next"], seq,
                      tn=1152).reshape(batch, seq, N_RWKV_COLS)
    os, lses = [], []
    for dil, view in zip(DILATIONS, views):
        o, lse = _attention(view.reshape(3, batch, seq // dil, dil * ATT_WIDTH), p["slopes"], batch, seq, dil)
        os.append(o)
        lses.append(lse)
    y, bonus = _rwkv(z, batch, seq, p["w0"], p["w2"], p["a0"], p["a2"],
                     p["k_k"], p["k_a"], p["r_k"], p["ones_bd"])
    x2 = _mixout(x1.reshape(batch, seq, D_MODEL), os, lses, y, bonus, z, batch, seq,
                 p["g2"], p["ln_x_w"], p["ln_x_b"], p["avg_bd"], p["spread"], p["w_out"])
    out = _ffn(x2.reshape(batch * seq, D_MODEL), p["ffn2_norm"], p["ffn2_gate"], p["ffn2_up"],
               p["ffn2_down"], p["final_norm"], final_norm=True)
    return out.reshape(batch, seq, D_MODEL)


def kernel(x_prompt, x_sample, ffn1_norm, ffn1_gate, ffn1_up, ffn1_down, mix_norm, w_in, w_out, mu_prev, mu_next, w0_f, w2_f, w0_b, w2_b, a0_f, a2_f, a0_b, a2_b, g2, k_k, k_a, r_k, ln_x_w, ln_x_b, ffn2_norm, ffn2_gate, ffn2_up, ffn2_down, final_norm):
    assert ffn1_norm.shape[0] == 1, "single layer"
    row = lambda t: t.reshape(1, -1)
    zero64 = jnp.zeros((64, RWKV_WIDTH), F32)
    w_in_b = w_in[0].astype(BF16)
    p = {
        "ffn1_norm": row(ffn1_norm[0]), "ffn1_gate": ffn1_gate[0].astype(BF16),
        "ffn1_up": ffn1_up[0].astype(BF16), "ffn1_down": ffn1_down[0].astype(BF16),
        "mix_norm": row(mix_norm[0]),
        "w_in_att": w_in_b[:, :N_ATT_COLS], "w_in_rwkv": w_in_b[:, N_ATT_COLS:],
        "w_out": w_out[0].astype(BF16),
        "mu_prev": row(mu_prev[0]), "mu_next": row(mu_next[0]),
        "w0": jnp.stack([row(w0_f[0]), row(w0_b[0])]),
        "w2": jnp.stack([jnp.concatenate([w2_f[0], zero64]), jnp.concatenate([zero64, w2_b[0]])]).astype(BF16),
        "a0": jnp.stack([row(a0_f[0]), row(a0_b[0])]),
        "a2": jnp.stack([jnp.concatenate([a2_f[0], zero64]), jnp.concatenate([zero64, a2_b[0]])]).astype(BF16),
        "g2": g2[0].astype(BF16),
        "k_k": row(k_k[0]), "k_a": row(k_a[0]), "r_k": row(r_k[0]),
        "ln_x_w": row(ln_x_w[0]), "ln_x_b": row(ln_x_b[0]),
        "ffn2_norm": row(ffn2_norm[0]), "ffn2_gate": ffn2_gate[0].astype(BF16),
        "ffn2_up": ffn2_up[0].astype(BF16), "ffn2_down": ffn2_down[0].astype(BF16),
        "final_norm": row(final_norm),
        "slopes": jnp.exp2(-8.0 * jnp.arange(1, ATT_HEADS + 1, dtype=F32) / ATT_HEADS),
        "ones_bd": _block_diag(HEAD_GROUP, HEAD_DIM, 1.0),
        "avg_bd": _block_diag(HEAD_GROUP, HEAD_DIM, 1.0 / HEAD_DIM),
        "spread": (jnp.arange(LANES)[:, None] == jnp.arange(ATT_WIDTH)[None, :] // HEAD_DIM).astype(BF16),
    }
    return (_trunk(x_prompt, p), _trunk(x_sample, p))
```

```python
import functools
import itertools

import jax
import jax.numpy as jnp
from jax import lax
from jax.experimental import pallas as pl
from jax.experimental.pallas import tpu as pltpu

F32 = jnp.float32
BF16 = jnp.bfloat16

D_MODEL = 2048
D_FF = 5632
HEAD_DIM = 64
ATT_WIDTH = 1024
ATT_HEADS = 16
RWKV_WIDTH = 1024
N_ATT_COLS = 3 * ATT_WIDTH
N_RWKV_COLS = 3 * RWKV_WIDTH + 64 * 4 + 128
WD_COL = 3 * RWKV_WIDTH
AD_COL = WD_COL + 128
GD_COL = AD_COL + 128
DILATIONS = (1, 4, 16)
BAND_HALF = 64
NORM_EPS = 1e-6
RWKV_LN_EPS = 64e-5
NEG_INF = -1e30

LANES = 128
HEAD_GROUP = 256
PROJ_CHUNK = 256
CHUNK = 64
VMEM_LIMIT = 56 * 1024 * 1024


def _cparams(sem):
    return pltpu.CompilerParams(dimension_semantics=sem, vmem_limit_bytes=VMEM_LIMIT)


def _dot(a, b):
    return jnp.dot(a.astype(BF16), b.astype(BF16), preferred_element_type=F32)


def _dot_nt(a, b):
    return lax.dot_general(a.astype(BF16), b.astype(BF16), (((1,), (1,)), ((), ())),
                           preferred_element_type=F32)


def _dot_tn(a, b):
    return lax.dot_general(a.astype(BF16), b.astype(BF16), (((0,), (0,)), ((), ())),
                           preferred_element_type=F32)


def _split2(x):
    hi = x.astype(BF16)
    lo = (x - hi.astype(F32)).astype(BF16)
    return hi, lo


def _head_sums(x, seg):
    hi, lo = _split2(x)
    outs = []
    for g in range(x.shape[1] // HEAD_GROUP):
        cols = slice(g * HEAD_GROUP, (g + 1) * HEAD_GROUP)
        outs.append(jnp.dot(hi[:, cols], seg, preferred_element_type=F32)
                    + jnp.dot(lo[:, cols], seg, preferred_element_type=F32))
    return jnp.concatenate(outs, axis=1)


def _cumsum_dot(tri_bf16, x):
    hi, lo = _split2(x)
    return (jnp.dot(tri_bf16, hi, preferred_element_type=F32)
            + jnp.dot(tri_bf16, lo, preferred_element_type=F32))


def _rmsnorm(x, g):
    return x * lax.rsqrt(jnp.mean(x * x, axis=-1, keepdims=True) + NORM_EPS) * g


def _ffn_kernel(x_ref, g_ref, wg_ref, wu_ref, wd_ref, fg_ref, o_ref, n_scr, *, final_norm):
    f = pl.program_id(1)

    @pl.when(f == 0)
    def _():
        x = x_ref[...]
        n_scr[...] = _rmsnorm(x, g_ref[...]).astype(BF16)
        o_ref[...] = x

    n = n_scr[...]
    hg = jnp.dot(n, wg_ref[...], preferred_element_type=F32)
    hu = jnp.dot(n, wu_ref[...], preferred_element_type=F32)
    h = (hg * jax.nn.sigmoid(hg) * (0.5 * hu)).astype(BF16)
    o_ref[...] += jnp.dot(h, wd_ref[...], preferred_element_type=F32)

    if final_norm:
        @pl.when(f == pl.num_programs(1) - 1)
        def _():
            o_ref[...] = _rmsnorm(o_ref[...], fg_ref[...])


def _ffn(x, g, wg, wu, wd, fg, *, final_norm, tm=1024, tf=512):
    n_tok = x.shape[0]
    return pl.pallas_call(
        functools.partial(_ffn_kernel, final_norm=final_norm),
        out_shape=jax.ShapeDtypeStruct((n_tok, D_MODEL), F32),
        grid=(n_tok // tm, D_FF // tf),
        in_specs=[
            pl.BlockSpec((tm, D_MODEL), lambda i, f: (i, 0)),
            pl.BlockSpec((1, D_MODEL), lambda i, f: (0, 0)),
            pl.BlockSpec((D_MODEL, tf), lambda i, f: (0, f)),
            pl.BlockSpec((D_MODEL, tf), lambda i, f: (0, f)),
            pl.BlockSpec((tf, D_MODEL), lambda i, f: (f, 0)),
            pl.BlockSpec((1, D_MODEL), lambda i, f: (0, 0)),
        ],
        out_specs=pl.BlockSpec((tm, D_MODEL), lambda i, f: (i, 0)),
        scratch_shapes=[pltpu.VMEM((tm, D_MODEL), BF16)],
        compiler_params=_cparams(("parallel", "arbitrary")),
        name="ffn_final" if final_norm else "ffn",
    )(x, g, wg, wu, wd, fg)


def _inproj_shift_kernel(x_ref, xp_ref, xn_ref, g_ref, w_ref, mup_ref, mun_ref, o_ref, n_scr, *, seq):
    i = pl.program_id(0)
    tm = x_ref.shape[0]

    @pl.when(pl.program_id(1) == 0)
    def _():
        g = g_ref[...]
        n_scr[0:tm, :] = _rmsnorm(x_ref[...], g).astype(BF16)
        halo = jnp.concatenate([xp_ref[...], xn_ref[...]], axis=0)
        n_scr[tm:, :] = _rmsnorm(halo, g).astype(BF16)

    n = n_scr[...]
    first_of_seq = (i * tm) % seq == 0
    last_of_seq = ((i + 1) * tm) % seq == 0
    tn = o_ref.shape[1]
    chunks = [slice(c0, min(c0 + PROJ_CHUNK, tn)) for c0 in range(0, tn, PROJ_CHUNK)]

    def project(cols):
        z_all = jnp.dot(n, w_ref[:, cols], preferred_element_type=F32)
        return z_all[:tm], z_all[tm:]

    def finish(cols, z, z_halo):
        prev_row = jnp.where(first_of_seq, 0.0, z_halo[7:8, :])
        next_row = jnp.where(last_of_seq, 0.0, z_halo[8:9, :])
        o_ref[:, cols] = _shifted(z, prev_row, next_row, mup_ref[:, cols], mun_ref[:, cols])

    pending = None
    for cols in chunks:
        done = project(cols)
        if pending is not None:
            finish(*pending)
        pending = (cols,) + done
    finish(*pending)


def _inproj_shift(x, g, w, mu_prev, mu_next, seq, *, tn, tm=1024):
    n_tok = x.shape[0]
    n_cols = w.shape[1]
    hb = tm // 8
    return pl.pallas_call(
        functools.partial(_inproj_shift_kernel, seq=seq),
        out_shape=jax.ShapeDtypeStruct((n_tok, n_cols), F32),
        grid=(n_tok // tm, n_cols // tn),
        in_specs=[
            pl.BlockSpec((tm, D_MODEL), lambda i, j: (i, 0)),
            pl.BlockSpec((8, D_MODEL), lambda i, j: (jnp.maximum(i * hb - 1, 0), 0)),
            pl.BlockSpec((8, D_MODEL), lambda i, j: (jnp.minimum((i + 1) * hb, n_tok // 8 - 1), 0)),
            pl.BlockSpec((1, D_MODEL), lambda i, j: (0, 0)),
            pl.BlockSpec((D_MODEL, tn), lambda i, j: (0, j)),
            pl.BlockSpec((1, tn), lambda i, j: (0, j)),
            pl.BlockSpec((1, tn), lambda i, j: (0, j)),
        ],
        out_specs=pl.BlockSpec((tm, tn), lambda i, j: (i, j)),
        scratch_shapes=[pltpu.VMEM((tm + 16, D_MODEL), BF16)],
        compiler_params=_cparams(("parallel", "arbitrary")),
        name="inproj_shift",
    )(x, x, x, g, w, mu_prev, mu_next)


def _inproj_views_kernel(x_ref, g_ref, w_ref, *refs):
    out_refs, (n_scr, acc_scr) = refs[:len(DILATIONS)], refs[len(DILATIONS):]

    @pl.when(pl.program_id(1) == 0)
    def _():
        n_scr[...] = _rmsnorm(x_ref[...], g_ref[...]).astype(BF16)

    n = n_scr[...]
    n_tiles, tm, _ = acc_scr.shape

    def finish(c0, acc):
        out_refs[0][:, c0:c0 + PROJ_CHUNK] = acc.astype(BF16)
        tiles = range(c0 // LANES, (c0 + PROJ_CHUNK) // LANES)
        for c in tiles:
            acc_scr[c] = acc[:, c * LANES - c0:(c + 1) * LANES - c0]
        for d, o_ref in zip(DILATIONS[1:], out_refs[1:]):
            for r in range(d):
                for c in tiles:
                    col = r * ATT_WIDTH + c * LANES
                    o_ref[:, col:col + LANES] = acc_scr[c, pl.ds(r, tm // d, stride=d), :].astype(BF16)

    pending = None
    for c0 in range(0, ATT_WIDTH, PROJ_CHUNK):
        acc = jnp.dot(n, w_ref[:, c0:c0 + PROJ_CHUNK], preferred_element_type=F32)
        if pending is not None:
            finish(*pending)
        pending = (c0, acc)
    finish(*pending)


def _inproj_views(x, g, w, *, tm=1024):
    n_tok = x.shape[0]
    return pl.pallas_call(
        _inproj_views_kernel,
        out_shape=[jax.ShapeDtypeStruct((3, n_tok // d, d * ATT_WIDTH), BF16) for d in DILATIONS],
        grid=(n_tok // tm, 3),
        in_specs=[
            pl.BlockSpec((tm, D_MODEL), lambda i, j: (i, 0)),
            pl.BlockSpec((1, D_MODEL), lambda i, j: (0, 0)),
            pl.BlockSpec((D_MODEL, ATT_WIDTH), lambda i, j: (0, j)),
        ],
        out_specs=[pl.BlockSpec((None, tm // d, d * ATT_WIDTH), lambda i, j: (j, i, 0)) for d in DILATIONS],
        scratch_shapes=[pltpu.VMEM((tm, D_MODEL), BF16), pltpu.VMEM((ATT_WIDTH // LANES, tm, LANES), F32)],
        compiler_params=_cparams(("parallel", "arbitrary")),
        name="inproj_views",
    )(x, g, w)


ATT_TQ = 128
ATT_TK = 64
ATT_SUB = 2
ATT_GROUP = 8
ATT_NKB = ATT_SUB * ATT_TQ // ATT_TK + 2


def _attn_kernel(q_ref, *refs):
    k_refs, v_refs = refs[:ATT_NKB], refs[ATT_NKB:2 * ATT_NKB]
    bias_ref, o_ref, lse_ref, kc_scr, vc_scr = refs[2 * ATT_NKB:]
    i = pl.program_id(2)
    for j, (kr, vr) in enumerate(zip(k_refs, v_refs)):
        kc_scr[j * ATT_TK:(j + 1) * ATT_TK, :] = kr[...]
        vc_scr[j * ATT_TK:(j + 1) * ATT_TK, :] = vr[...]

    lane = lax.broadcasted_iota(jnp.int32, (ATT_TQ, LANES), 1)
    first = lane < HEAD_DIM
    heads = range(ATT_HEADS)
    cols = [slice((h // 2) * LANES, (h // 2 + 1) * LANES) for h in heads]
    sel = [first if h % 2 == 0 else jnp.logical_not(first) for h in heads]
    scale = jnp.asarray(HEAD_DIM ** -0.5, BF16)
    last_tile = ATT_SUB * pl.num_programs(2) - 1

    for u in range(ATT_SUB):
        tile = ATT_SUB * i + u
        case = jnp.where(tile == 0, 1, jnp.where(tile == last_tile, 2, 0))
        qrows = slice(u * ATT_TQ, (u + 1) * ATT_TQ)
        krows = slice(u * ATT_TQ, u * ATT_TQ + 4 * ATT_TK)
        o, lse = [], []
        for g in range(0, ATT_HEADS, ATT_GROUP):
            grp = range(g, g + ATT_GROUP)
            qm = [jnp.where(sel[h], q_ref[qrows, cols[h]] * scale, jnp.zeros((ATT_TQ, LANES), BF16))
                  for h in grp]
            s = [_dot_nt(q_, kc_scr[krows, cols[h]]) + bias_ref[case, h] for q_, h in zip(qm, grp)]
            m = [jnp.max(x, axis=-1, keepdims=True) for x in s]
            p = [jnp.exp(x - mx) for x, mx in zip(s, m)]
            l = [jnp.sum(x, axis=-1, keepdims=True) for x in p]
            o += [jnp.dot(p_.astype(BF16), vc_scr[krows, cols[h]], preferred_element_type=F32) * (1.0 / l_)
                  for p_, l_, h in zip(p, l, grp)]
            lse += [mx + jnp.log(lx) for mx, lx in zip(m, l)]
        for h in range(0, ATT_HEADS, 2):
            o_ref[qrows, cols[h]] = jnp.where(first, o[h], o[h + 1]).astype(o_ref.dtype)
        lse_tile = jnp.zeros((ATT_TQ, LANES), F32)
        for h in heads:
            lse_tile = jnp.where(lane == h, lse[h], lse_tile)
        lse_ref[qrows, :] = lse_tile


def _attention_bias(slopes, dil):
    nk = 4 * ATT_TK
    row = jnp.arange(ATT_TQ)[:, None]
    col = jnp.arange(nk)[None, :]
    rel = col - BAND_HALF - row
    in_band = jnp.abs(rel) <= BAND_HALF
    dist = (jnp.abs(rel) * dil).astype(F32)
    inside = jnp.stack([in_band, in_band & (col >= BAND_HALF), in_band & (col < BAND_HALF + ATT_TQ)])
    return jnp.where(inside[:, None], -slopes[None, :, None, None] * dist, NEG_INF)


def _attention(view, slopes, batch, seq, dil):
    L = seq // dil
    step_q = ATT_SUB * ATT_TQ
    assert L % step_q == 0 and L // ATT_TQ >= 2, "first and last query block must differ"
    nkb = L // ATT_TK
    ratio = step_q // ATT_TK
    bias = _attention_bias(slopes, dil)

    def kv_spec(which, j):
        def imap(b, r, i):
            blk = jnp.clip(i * ratio - 1 + j, 0, nkb - 1)
            return (which, b, blk, r)
        return pl.BlockSpec((None, None, ATT_TK, ATT_WIDTH), imap)

    o, lse = pl.pallas_call(
        _attn_kernel,
        out_shape=[jax.ShapeDtypeStruct((batch, L, dil * ATT_WIDTH), BF16),
                   jax.ShapeDtypeStruct((batch, dil, L, LANES), F32)],
        grid=(batch, dil, L // step_q),
        in_specs=[pl.BlockSpec((None, None, step_q, ATT_WIDTH), lambda b, r, i: (0, b, i, r))]
        + [kv_spec(1, j) for j in range(ATT_NKB)] + [kv_spec(2, j) for j in range(ATT_NKB)]
        + [pl.BlockSpec(bias.shape, lambda b, r, i: (0, 0, 0, 0))],
        out_specs=[pl.BlockSpec((None, step_q, ATT_WIDTH), lambda b, r, i: (b, i, r)),
                   pl.BlockSpec((None, None, step_q, LANES), lambda b, r, i: (b, r, i, 0))],
        scratch_shapes=[pltpu.VMEM((ATT_NKB * ATT_TK, ATT_WIDTH), BF16),
                        pltpu.VMEM((ATT_NKB * ATT_TK, ATT_WIDTH), BF16)],
        compiler_params=_cparams(("parallel", "parallel", "arbitrary")),
        name=f"attn_d{dil}",
    )(view, *([view] * (2 * ATT_NKB)), bias)
    return o, lse.transpose(0, 2, 1, 3).reshape(batch, seq, LANES)


RWKV_TB = 256
PAIRS = RWKV_WIDTH // LANES
SOLVE_CHUNKS = 2


def _shifted(z, prev_row, next_row, mu_prev, mu_next):
    n = z.shape[0]
    ridx = lax.broadcasted_iota(jnp.int32, z.shape, 0)
    z_prev = jnp.where(ridx == 0, prev_row, pltpu.roll(z, 1, 0))
    z_next = jnp.where(ridx == n - 1, next_row, pltpu.roll(z, n - 1, 0))
    return z + mu_prev * (z_prev - z) + mu_next * (z_next - z)


def _stack_pair(x, first):
    zero = jnp.zeros_like(x)
    return jnp.concatenate([jnp.where(first, x, zero), jnp.where(first, zero, x)], axis=0)


def _rwkv_kernel(z_ref, w0_ref, w2_ref, a0_ref, a2_ref, kk_ref, ka_ref, rk_ref, ones_ref, y_ref, bonus_ref,
                 at_s, bt_s, kt_s, rt_s, bd_s, kd_s, v_s, pc_s, st_s, mask_s):
    d = pl.program_id(1)
    step = pl.program_id(2)
    nc = RWKV_TB // CHUNK
    fwd = d == 0
    sgn = jnp.where(fwd, 1, -1)
    wr = step % 2
    rd = 1 - wr
    n2 = 2 * CHUNK

    @pl.when(step == 0)
    def _():
        st_s[...] = jnp.zeros_like(st_s)
        rr = lax.broadcasted_iota(jnp.int32, (n2, n2), 0)
        cc = lax.broadcasted_iota(jnp.int32, (n2, n2), 1)
        same_head = (rr // CHUNK) == (cc // CHUNK)
        tt = rr % CHUNK
        ss = cc % CHUNK
        strict = same_head & ((tt - ss) * sgn > 0)
        masks = (strict,
                 same_head & ((tt - ss) * sgn >= 0),
                 strict & ((tt // 16) == (ss // 16)),
                 strict & ((tt // 32) == (ss // 32)) & ((tt // 16) != (ss // 16)),
                 strict & ((tt // 32) != (ss // 32)),
                 rr == cc)
        for idx, msk in enumerate(masks):
            mask_s[idx] = jnp.where(msk, 1.0, 0.0)

    ti = lax.broadcasted_iota(jnp.int32, (CHUNK, CHUNK), 0)
    si = lax.broadcasted_iota(jnp.int32, (CHUNK, CHUNK), 1)
    tri = jnp.where((ti - si) * sgn >= 0, 1.0, 0.0).astype(BF16)

    def prepare(cp):
        rows = pl.ds(pl.multiple_of(cp * CHUNK, CHUNK), CHUNK)
        r = z_ref[rows, 0:RWKV_WIDTH]
        k = z_ref[rows, RWKV_WIDTH:2 * RWKV_WIDTH]
        v = z_ref[rows, 2 * RWKV_WIDTH:3 * RWKV_WIDTH]
        v_s[wr, rows, :] = v.astype(BF16)
        yield
        wd = z_ref[rows, WD_COL:WD_COL + 128]
        ad = z_ref[rows, AD_COL:AD_COL + 128]
        u = -(w0_ref[...] + _dot(jnp.tanh(wd), w2_ref[...]))
        softplus = jnp.maximum(u, 0.0) + jnp.log(1.0 + jnp.exp(-jnp.abs(u)))
        lw = -jnp.exp(-softplus - 0.5)
        yield
        iclr = jax.nn.sigmoid(a0_ref[...] + _dot(ad, a2_ref[...]))
        ones = ones_ref[...]
        kkr = k * kk_ref[...]
        kk = kkr * lax.rsqrt(jnp.maximum(_head_sums(kkr * kkr, ones), 1e-24))
        yield
        kdir = k * (1.0 + (iclr - 1.0) * ka_ref[...])
        bonus_ref[rows, :] = _head_sums(r * kdir * rk_ref[...], ones) * v
        yield
        b = kk * iclr
        lp_incl = _cumsum_dot(tri, lw)
        total = jnp.sum(lw, axis=0, keepdims=True)
        pc_s[wr, cp] = jnp.broadcast_to(jnp.exp(total), (8, RWKV_WIDTH))
        yield
        e_neg = jnp.exp(-lp_incl)
        at_s[wr, rows, :] = (-kk * jnp.exp(lp_incl - lw)).astype(BF16)
        bt_s[wr, rows, :] = (b * e_neg).astype(BF16)
        yield
        kt_s[wr, rows, :] = (kdir * e_neg).astype(BF16)
        rt_s[wr, rows, :] = r * jnp.exp(lp_incl)
        yield
        e_rest = jnp.exp(total - lp_incl)
        bd_s[wr, rows, :] = (b * e_rest).astype(BF16)
        kd_s[wr, rows, :] = (kdir * e_rest).astype(BF16)

    strict, incl, blk16, off32, off64, eye = range(6)
    lane = lax.broadcasted_iota(jnp.int32, (CHUNK, LANES), 1)
    first = lane < HEAD_DIM

    def solve(chunks, tick):
        items = [(c, p) for c in chunks for p in range(PAIRS)]
        rows = [pl.ds(pl.multiple_of(c * CHUNK, CHUNK), CHUNK) for c, _ in items]
        lanes = [slice(p * LANES, (p + 1) * LANES) for _, p in items]
        each = lambda fn, *lists: [fn(*xs) for xs in zip(*lists)]
        load = lambda ref: [_stack_pair(ref[rd, r_, l_], first) for r_, l_ in zip(rows, lanes)]
        at, bt, kt, bd, kd, vv = (load(ref) for ref in (at_s, bt_s, kt_s, bd_s, kd_s, v_s))
        rt = [_stack_pair(rt_s[rd, r_, l_].astype(BF16), first) for r_, l_ in zip(rows, lanes)]
        tick()

        prod = each(lambda a_, r_, b_, k_: _dot_nt(jnp.concatenate([a_, r_], axis=0),
                                                   jnp.concatenate([b_, k_], axis=0)), at, rt, bt, kt)
        a_ak = [(x[:n2, n2:] * mask_s[strict]).astype(BF16) for x in prod]
        q_bk = [jnp.concatenate([x[n2:, :n2] * mask_s[incl], x[n2:, n2:] * mask_s[incl]], axis=1).astype(BF16)
                for x in prod]
        offs = [[(x[:n2, :n2] * mask_s[m]).astype(BF16) for x in prod] for m in (off32, off64)]
        tick()

        d0 = [x[:n2, :n2] * mask_s[blk16] for x in prod]
        t_inv = [mask_s[eye] + x for x in d0]
        pw = [_dot(x, x).astype(BF16) for x in d0]
        tick()
        for _ in range(2):
            both = each(lambda t_, p_: _dot(jnp.concatenate([t_.astype(BF16), p_], axis=0), p_), t_inv, pw)
            t_inv = each(lambda t_, b_: t_ + b_[:n2], t_inv, both)
            pw = [x[n2:].astype(BF16) for x in both]
            tick()
        t_inv = each(lambda t_, p_: t_ + _dot(t_, p_), t_inv, pw)
        tick()
        for off in offs:
            inner = each(lambda o_, t_: _dot(o_, t_).astype(BF16), off, t_inv)
            t_inv = each(lambda t_, i_: t_ + _dot(t_, i_), t_inv, inner)
            tick()

        av = each(lambda a_, v_: _dot(a_, v_).astype(BF16), a_ak, vv)
        tick()
        wu = each(lambda t_, a_, v_: _dot(t_, jnp.concatenate([a_, v_], axis=1)).astype(BF16),
                  t_inv, at, av)
        tick()
        zero = jnp.zeros((n2, LANES), BF16)
        rhs = each(lambda w_, v_: jnp.concatenate(
            [w_, jnp.concatenate([zero, v_], axis=1)], axis=0), wu, vv)
        ry = each(_dot, q_bk, rhs)
        gh = each(lambda b_, k_, x_: _dot_tn(jnp.concatenate([b_, k_], axis=0), x_),
                  bd, kd, rhs)
        tick()

        st = [st_s[p] for p in range(PAIRS)]
        for j, c in enumerate(chunks):
            sl = slice(j * PAIRS, (j + 1) * PAIRS)
            rt_f = [_stack_pair(rt_s[rd, rows[i], lanes[i]], first) for i in range(sl.start, sl.stop)]
            y_st = each(lambda r_, x_, s_: _dot(r_ + x_[:, :LANES], s_) + x_[:, LANES:], rt_f, ry[sl], st)
            gs = each(lambda g_, s_: _dot(g_[:, :LANES], s_), gh[sl], st)
            pc_row = pc_s[rd, c]
            for p in range(PAIRS):
                i = j * PAIRS + p
                y_ref[rows[i], lanes[i]] = y_st[p][:CHUNK] + y_st[p][CHUNK:]
                pc_col = jnp.sum(mask_s[eye] * pc_row[0:1, lanes[i]], axis=1, keepdims=True)
                st[p] = pc_col * st[p] + gs[p] + gh[i][:, LANES:]
        for p in range(PAIRS):
            st_s[p] = st[p]

    def chunk_body(ci, carry):
        first_chunk = ci * SOLVE_CHUNKS
        stages = itertools.chain(*[prepare(first_chunk + j) for j in range(SOLVE_CHUNKS)])

        def tick():
            for _ in range(SOLVE_CHUNKS):
                next(stages, None)

        solve([jnp.where(fwd, first_chunk + j, nc - 1 - first_chunk - j) for j in range(SOLVE_CHUNKS)], tick)
        for _ in stages:
            pass
        return carry

    def prepare_only(cp, carry):
        for _ in prepare(cp):
            pass
        return carry

    @pl.when(step == 0)
    def _():
        lax.fori_loop(0, nc, prepare_only, 0)

    @pl.when(step > 0)
    def _():
        lax.fori_loop(0, nc // SOLVE_CHUNKS, chunk_body, 0)


def _rwkv(z, batch, seq, w0, w2, a0, a2, k_k, k_a, r_k, ones_bd):
    n_blk = seq // RWKV_TB

    def time_block(d, i):
        return jnp.where(d == 0, i, n_blk - 1 - i)

    def tmap(d, i):
        return time_block(d, jnp.minimum(i, n_blk - 1))

    def smap(d, i):
        return time_block(d, jnp.maximum(i - 1, 0))

    vec = lambda: pl.BlockSpec((1, RWKV_WIDTH), lambda b, d, i: (0, 0))
    in_specs = [
        pl.BlockSpec((None, RWKV_TB, N_RWKV_COLS), lambda b, d, i: (b, tmap(d, i), 0)),
        pl.BlockSpec((None, 1, RWKV_WIDTH), lambda b, d, i: (d, 0, 0)),
        pl.BlockSpec((None, 128, RWKV_WIDTH), lambda b, d, i: (d, 0, 0)),
        pl.BlockSpec((None, 1, RWKV_WIDTH), lambda b, d, i: (d, 0, 0)),
        pl.BlockSpec((None, 128, RWKV_WIDTH), lambda b, d, i: (d, 0, 0)),
        vec(), vec(), vec(),
        pl.BlockSpec((HEAD_GROUP, HEAD_GROUP), lambda b, d, i: (0, 0)),
    ]
    out_spec = lambda m: pl.BlockSpec((None, None, RWKV_TB, RWKV_WIDTH), lambda b, d, i: (d, b, m(d, i), 0))
    tok = lambda dt: pltpu.VMEM((2, RWKV_TB, RWKV_WIDTH), dt)
    return pl.pallas_call(
        _rwkv_kernel,
        out_shape=[jax.ShapeDtypeStruct((2, batch, seq, RWKV_WIDTH), F32)] * 2,
        grid=(batch, 2, n_blk + 1),
        in_specs=in_specs,
        out_specs=[out_spec(smap), out_spec(tmap)],
        scratch_shapes=[tok(BF16), tok(BF16), tok(BF16), tok(F32), tok(BF16), tok(BF16), tok(BF16),
                        pltpu.VMEM((2, RWKV_TB // CHUNK, 8, RWKV_WIDTH), F32),
                        pltpu.VMEM((PAIRS, LANES, LANES), F32),
                        pltpu.VMEM((6, 2 * CHUNK, 2 * CHUNK), F32)],
        compiler_params=_cparams(("parallel", "parallel", "arbitrary")),
        name="rwkv",
    )(z, w0, w2, a0, a2, k_k, k_a, r_k, ones_bd)


MIX_TM = 256


def _mixout_kernel(x_ref, o1, o2, o3, l1, l2, l3, y_ref, bonus_ref, gd_ref, g2_ref, lnw_ref, lnb_ref,
                   avg_ref, spread_ref, wo_ref, out_ref, o_scr):
    n_tiles = ATT_WIDTH // LANES
    o_refs = (o1, o2, o3)
    for k, d in enumerate(DILATIONS):
        for r in range(d if d > 1 else 0):
            for c in range(n_tiles):
                col = r * ATT_WIDTH + c * LANES
                o_scr[k, c, pl.ds(r, MIX_TM // d, stride=d), :] = o_refs[k][:, col:col + LANES].astype(F32)
    m = jnp.maximum(jnp.maximum(l1[...], l2[...]), l3[...])
    e = [jnp.exp(l[...] - m) for l in (l1, l2, l3)]
    inv = 1.0 / (e[0] + e[1] + e[2])
    spread2 = jnp.concatenate([spread_ref[...]] * 2, axis=0)
    att = jnp.zeros((MIX_TM, ATT_WIDTH), F32)
    for k, (d, ep) in enumerate(zip(DILATIONS, e)):
        wts = jnp.dot(jnp.concatenate(_split2(ep * inv), axis=1), spread2, preferred_element_type=F32)
        o_tok = (o_refs[k][...].astype(F32) if d == 1
                 else jnp.concatenate([o_scr[k, c] for c in range(n_tiles)], axis=1))
        att = att + wts * o_tok

    y = y_ref[0] + y_ref[1]
    avg = avg_ref[...]
    yc = y - _head_sums(y, avg)
    var = _head_sums(yc * yc, avg)
    yn = yc * lax.rsqrt(var + RWKV_LN_EPS) * lnw_ref[...] + lnb_ref[...]
    gate = _dot(jax.nn.sigmoid(gd_ref[...]), g2_ref[...])
    rw = (yn + bonus_ref[0] + bonus_ref[1]) * gate

    mixed = jnp.dot(att.astype(BF16), wo_ref[0:ATT_WIDTH, :], preferred_element_type=F32)
    mixed = mixed + jnp.dot(rw.astype(BF16), wo_ref[ATT_WIDTH:, :], preferred_element_type=F32)
    out_ref[...] = x_ref[...] + mixed


def _mixout(x1, os, lses, y, bonus, z, batch, seq, g2, ln_w, ln_b, avg_bd, spread, w_out):
    gcol = GD_COL // 128
    tok = lambda w: pl.BlockSpec((None, MIX_TM, w), lambda b, i: (b, i, 0))
    const = lambda shape: pl.BlockSpec(shape, lambda b, i: (0,) * len(shape))
    dir_spec = lambda: pl.BlockSpec((2, None, MIX_TM, RWKV_WIDTH), lambda b, i: (0, b, i, 0))
    view = lambda d: pl.BlockSpec((None, MIX_TM // d, d * ATT_WIDTH), lambda b, i: (b, i, 0))
    in_specs = [tok(D_MODEL)] + [view(d) for d in DILATIONS] + [tok(LANES)] * 3 + [dir_spec(), dir_spec()] + [
        pl.BlockSpec((None, MIX_TM, 128), lambda b, i: (b, i, gcol)),
        const((128, RWKV_WIDTH)),
        const((1, RWKV_WIDTH)), const((1, RWKV_WIDTH)), const((HEAD_GROUP, HEAD_GROUP)),
        const((LANES, ATT_WIDTH)), const((D_MODEL, D_MODEL)),
    ]
    return pl.pallas_call(
        _mixout_kernel,
        out_shape=jax.ShapeDtypeStruct((batch, seq, D_MODEL), F32),
        grid=(batch, seq // MIX_TM),
        in_specs=in_specs,
        out_specs=tok(D_MODEL),
        scratch_shapes=[pltpu.VMEM((len(DILATIONS), ATT_WIDTH // LANES, MIX_TM, LANES), F32)],
        compiler_params=_cparams(("parallel", "arbitrary")),
        name="mixout",
    )(x1, *os, *lses, y, bonus, z, g2, ln_w, ln_b, avg_bd, spread, w_out)


def _block_diag(width, block, value):
    idx = jnp.arange(width) // block
    return jnp.where(idx[:, None] == idx[None, :], value, 0.0).astype(BF16)


def _trunk(x, p):
    batch, seq, _ = x.shape
    xf = x.reshape(batch * seq, D_MODEL)
    x1 = _ffn(xf, p["ffn1_norm"], p["ffn1_gate"], p["ffn1_up"], p["ffn1_down"], p["final_norm"],
              final_norm=False)
    views = _inproj_views(x1, p["mix_norm"], p["w_in_att"])
    z = _inproj_shift(x1, p["mix_norm"], p["w_in_rwkv"], p["mu_prev"], p["mu_next"], seq,
                      tn=1152).reshape(batch, seq, N_RWKV_COLS)
    os, lses = [], []
    for dil, view in zip(DILATIONS, views):
        o, lse = _attention(view.reshape(3, batch, seq // dil, dil * ATT_WIDTH), p["slopes"], batch, seq, dil)
        os.append(o)
        lses.append(lse)
    y, bonus = _rwkv(z, batch, seq, p["w0"], p["w2"], p["a0"], p["a2"],
                     p["k_k"], p["k_a"], p["r_k"], p["ones_bd"])
    x2 = _mixout(x1.reshape(batch, seq, D_MODEL), os, lses, y, bonus, z, batch, seq,
                 p["g2"], p["ln_x_w"], p["ln_x_b"], p["avg_bd"], p["spread"], p["w_out"])
    out = _ffn(x2.reshape(batch * seq, D_MODEL), p["ffn2_norm"], p["ffn2_gate"], p["ffn2_up"],
               p["ffn2_down"], p["final_norm"], final_norm=True)
    return out.reshape(batch, seq, D_MODEL)


def kernel(x_prompt, x_sample, ffn1_norm, ffn1_gate, ffn1_up, ffn1_down, mix_norm, w_in, w_out, mu_prev, mu_next, w0_f, w2_f, w0_b, w2_b, a0_f, a2_f, a0_b, a2_b, g2, k_k, k_a, r_k, ln_x_w, ln_x_b, ffn2_norm, ffn2_gate, ffn2_up, ffn2_down, final_norm):
    assert ffn1_norm.shape[0] == 1, "single layer"
    row = lambda t: t.reshape(1, -1)
    zero64 = jnp.zeros((64, RWKV_WIDTH), F32)
    w_in_b = w_in[0].astype(BF16)
    p = {
        "ffn1_norm": row(ffn1_norm[0]), "ffn1_gate": ffn1_gate[0].astype(BF16),
        "ffn1_up": ffn1_up[0].astype(BF16), "ffn1_down": ffn1_down[0].astype(BF16),
        "mix_norm": row(mix_norm[0]),
        "w_in_att": w_in_b[:, :N_ATT_COLS], "w_in_rwkv": w_in_b[:, N_ATT_COLS:],
        "w_out": w_out[0].astype(BF16),
        "mu_prev": row(mu_prev[0]), "mu_next": row(mu_next[0]),
        "w0": jnp.stack([row(w0_f[0]), row(w0_b[0])]),
        "w2": jnp.stack([jnp.concatenate([w2_f[0], zero64]), jnp.concatenate([zero64, w2_b[0]])]).astype(BF16),
        "a0": jnp.stack([row(a0_f[0]), row(a0_b[0])]),
        "a2": jnp.stack([jnp.concatenate([a2_f[0], zero64]), jnp.concatenate([zero64, a2_b[0]])]).astype(BF16),
        "g2": g2[0].astype(BF16),
        "k_k": row(k_k[0]), "k_a": row(k_a[0]), "r_k": row(r_k[0]),
        "ln_x_w": row(ln_x_w[0]), "ln_x_b": row(ln_x_b[0]),
        "ffn2_norm": row(ffn2_norm[0]), "ffn2_gate": ffn2_gate[0].astype(BF16),
        "ffn2_up": ffn2_up[0].astype(BF16), "ffn2_down": ffn2_down[0].astype(BF16),
        "final_norm": row(final_norm),
        "slopes": jnp.exp2(-8.0 * jnp.arange(1, ATT_HEADS + 1, dtype=F32) / ATT_HEADS),
        "ones_bd": _block_diag(HEAD_GROUP, HEAD_DIM, 1.0),
        "avg_bd": _block_diag(HEAD_GROUP, HEAD_DIM, 1.0 / HEAD_DIM),
        "spread": (jnp.arange(LANES)[:, None] == jnp.arange(ATT_WIDTH)[None, :] // HEAD_DIM).astype(BF16),
    }
    return (_trunk(x_prompt, p), _trunk(x_sample, p))
```

```python
import functools
import itertools

import jax
import jax.numpy as jnp
from jax import lax
from jax.experimental import pallas as pl
from jax.experimental.pallas import tpu as pltpu

F32 = jnp.float32
BF16 = jnp.bfloat16

D_MODEL = 2048
D_FF = 5632
HEAD_DIM = 64
ATT_WIDTH = 1024
ATT_HEADS = 16
RWKV_WIDTH = 1024
N_ATT_COLS = 3 * ATT_WIDTH
N_RWKV_COLS = 3 * RWKV_WIDTH + 64 * 4 + 128
WD_COL = 3 * RWKV_WIDTH
AD_COL = WD_COL + 128
GD_COL = AD_COL + 128
DILATIONS = (1, 4, 16)
BAND_HALF = 64
NORM_EPS = 1e-6
RWKV_LN_EPS = 64e-5
NEG_INF = -1e30

LANES = 128
HEAD_GROUP = 256
CHUNK = 64
VMEM_LIMIT = 56 * 1024 * 1024


def _cparams(sem):
    return pltpu.CompilerParams(dimension_semantics=sem, vmem_limit_bytes=VMEM_LIMIT)


def _dot(a, b):
    return jnp.dot(a.astype(BF16), b.astype(BF16), preferred_element_type=F32)


def _dot_nt(a, b):
    return lax.dot_general(a.astype(BF16), b.astype(BF16), (((1,), (1,)), ((), ())),
                           preferred_element_type=F32)


def _dot_tn(a, b):
    return lax.dot_general(a.astype(BF16), b.astype(BF16), (((0,), (0,)), ((), ())),
                           preferred_element_type=F32)


def _split2(x):
    hi = x.astype(BF16)
    lo = (x - hi.astype(F32)).astype(BF16)
    return hi, lo


def _head_sums(x, seg):
    hi, lo = _split2(x)
    outs = []
    for g in range(x.shape[1] // HEAD_GROUP):
        cols = slice(g * HEAD_GROUP, (g + 1) * HEAD_GROUP)
        outs.append(jnp.dot(hi[:, cols], seg, preferred_element_type=F32)
                    + jnp.dot(lo[:, cols], seg, preferred_element_type=F32))
    return jnp.concatenate(outs, axis=1)


def _cumsum_dot(tri_bf16, x):
    hi, lo = _split2(x)
    return (jnp.dot(tri_bf16, hi, preferred_element_type=F32)
            + jnp.dot(tri_bf16, lo, preferred_element_type=F32))


def _rmsnorm(x, g):
    return x * lax.rsqrt(jnp.mean(x * x, axis=-1, keepdims=True) + NORM_EPS) * g


def _ffn_kernel(x_ref, g_ref, wg_ref, wu_ref, wd_ref, fg_ref, o_ref, n_scr, *, final_norm):
    f = pl.program_id(1)

    @pl.when(f == 0)
    def _():
        x = x_ref[...]
        n_scr[...] = _rmsnorm(x, g_ref[...]).astype(BF16)
        o_ref[...] = x

    n = n_scr[...]
    hg = jnp.dot(n, wg_ref[...], preferred_element_type=F32)
    hu = jnp.dot(n, wu_ref[...], preferred_element_type=F32)
    h = (hg * jax.nn.sigmoid(hg) * (0.5 * hu)).astype(BF16)
    o_ref[...] += jnp.dot(h, wd_ref[...], preferred_element_type=F32)

    if final_norm:
        @pl.when(f == pl.num_programs(1) - 1)
        def _():
            o_ref[...] = _rmsnorm(o_ref[...], fg_ref[...])


def _ffn(x, g, wg, wu, wd, fg, *, final_norm, tm=1024, tf=512):
    n_tok = x.shape[0]
    return pl.pallas_call(
        functools.partial(_ffn_kernel, final_norm=final_norm),
        out_shape=jax.ShapeDtypeStruct((n_tok, D_MODEL), F32),
        grid=(n_tok // tm, D_FF // tf),
        in_specs=[
            pl.BlockSpec((tm, D_MODEL), lambda i, f: (i, 0)),
            pl.BlockSpec((1, D_MODEL), lambda i, f: (0, 0)),
            pl.BlockSpec((D_MODEL, tf), lambda i, f: (0, f)),
            pl.BlockSpec((D_MODEL, tf), lambda i, f: (0, f)),
            pl.BlockSpec((tf, D_MODEL), lambda i, f: (f, 0)),
            pl.BlockSpec((1, D_MODEL), lambda i, f: (0, 0)),
        ],
        out_specs=pl.BlockSpec((tm, D_MODEL), lambda i, f: (i, 0)),
        scratch_shapes=[pltpu.VMEM((tm, D_MODEL), BF16)],
        compiler_params=_cparams(("parallel", "arbitrary")),
        name="ffn_final" if final_norm else "ffn",
    )(x, g, wg, wu, wd, fg)


def _inproj_shift_kernel(x_ref, xp_ref, xn_ref, g_ref, w_ref, mup_ref, mun_ref, o_ref, n_scr, nh_scr, *, seq):
    i = pl.program_id(0)
    tm = x_ref.shape[0]

    @pl.when(pl.program_id(1) == 0)
    def _():
        g = g_ref[...]
        n_scr[...] = _rmsnorm(x_ref[...], g).astype(BF16)
        halo = jnp.concatenate([xp_ref[...], xn_ref[...]], axis=0)
        nh_scr[...] = _rmsnorm(halo, g).astype(BF16)

    w = w_ref[...]
    z = jnp.dot(n_scr[...], w, preferred_element_type=F32)
    z_halo = jnp.dot(nh_scr[...], w, preferred_element_type=F32)
    prev_row = jnp.where((i * tm) % seq == 0, 0.0, z_halo[7:8, :])
    next_row = jnp.where(((i + 1) * tm) % seq == 0, 0.0, z_halo[8:9, :])
    o_ref[...] = _shifted(z, prev_row, next_row, mup_ref[...], mun_ref[...])


def _inproj_shift(x, g, w, mu_prev, mu_next, seq, *, tn, tm=1024):
    n_tok = x.shape[0]
    n_cols = w.shape[1]
    hb = tm // 8
    return pl.pallas_call(
        functools.partial(_inproj_shift_kernel, seq=seq),
        out_shape=jax.ShapeDtypeStruct((n_tok, n_cols), F32),
        grid=(n_tok // tm, n_cols // tn),
        in_specs=[
            pl.BlockSpec((tm, D_MODEL), lambda i, j: (i, 0)),
            pl.BlockSpec((8, D_MODEL), lambda i, j: (jnp.maximum(i * hb - 1, 0), 0)),
            pl.BlockSpec((8, D_MODEL), lambda i, j: (jnp.minimum((i + 1) * hb, n_tok // 8 - 1), 0)),
            pl.BlockSpec((1, D_MODEL), lambda i, j: (0, 0)),
            pl.BlockSpec((D_MODEL, tn), lambda i, j: (0, j)),
            pl.BlockSpec((1, tn), lambda i, j: (0, j)),
            pl.BlockSpec((1, tn), lambda i, j: (0, j)),
        ],
        out_specs=pl.BlockSpec((tm, tn), lambda i, j: (i, j)),
        scratch_shapes=[pltpu.VMEM((tm, D_MODEL), BF16), pltpu.VMEM((16, D_MODEL), BF16)],
        compiler_params=_cparams(("parallel", "arbitrary")),
        name="inproj_shift",
    )(x, x, x, g, w, mu_prev, mu_next)


def _inproj_views_kernel(x_ref, g_ref, w_ref, *refs):
    out_refs, (n_scr, acc_scr) = refs[:len(DILATIONS)], refs[len(DILATIONS):]

    @pl.when(pl.program_id(1) == 0)
    def _():
        n_scr[...] = _rmsnorm(x_ref[...], g_ref[...]).astype(BF16)

    acc = jnp.dot(n_scr[...], w_ref[...], preferred_element_type=F32)
    out_refs[0][...] = acc.astype(BF16)
    n_tiles, tm, _ = acc_scr.shape
    for c in range(n_tiles):
        acc_scr[c] = acc[:, c * LANES:(c + 1) * LANES]
    for d, o_ref in zip(DILATIONS[1:], out_refs[1:]):
        for r in range(d):
            for c in range(n_tiles):
                col = r * ATT_WIDTH + c * LANES
                o_ref[:, col:col + LANES] = acc_scr[c, pl.ds(r, tm // d, stride=d), :].astype(BF16)


def _inproj_views(x, g, w, *, tm=1024):
    n_tok = x.shape[0]
    return pl.pallas_call(
        _inproj_views_kernel,
        out_shape=[jax.ShapeDtypeStruct((3, n_tok // d, d * ATT_WIDTH), BF16) for d in DILATIONS],
        grid=(n_tok // tm, 3),
        in_specs=[
            pl.BlockSpec((tm, D_MODEL), lambda i, j: (i, 0)),
            pl.BlockSpec((1, D_MODEL), lambda i, j: (0, 0)),
            pl.BlockSpec((D_MODEL, ATT_WIDTH), lambda i, j: (0, j)),
        ],
        out_specs=[pl.BlockSpec((None, tm // d, d * ATT_WIDTH), lambda i, j: (j, i, 0)) for d in DILATIONS],
        scratch_shapes=[pltpu.VMEM((tm, D_MODEL), BF16), pltpu.VMEM((ATT_WIDTH // LANES, tm, LANES), F32)],
        compiler_params=_cparams(("parallel", "arbitrary")),
        name="inproj_views",
    )(x, g, w)


ATT_TQ = 128
ATT_TK = 64
ATT_GROUP = 8


def _attn_kernel(q_ref, *refs, sub):
    nkb = sub * ATT_TQ // ATT_TK + 2
    k_refs, v_refs = refs[:nkb], refs[nkb:2 * nkb]
    bias_ref, o_ref, lse_ref, kc_scr, vc_scr = refs[2 * nkb:]
    i = pl.program_id(2)
    for j, (kr, vr) in enumerate(zip(k_refs, v_refs)):
        kc_scr[j * ATT_TK:(j + 1) * ATT_TK, :] = kr[...]
        vc_scr[j * ATT_TK:(j + 1) * ATT_TK, :] = vr[...]

    lane = lax.broadcasted_iota(jnp.int32, (ATT_TQ, LANES), 1)
    first = lane < HEAD_DIM
    heads = range(ATT_HEADS)
    cols = [slice((h // 2) * LANES, (h // 2 + 1) * LANES) for h in heads]
    sel = [first if h % 2 == 0 else jnp.logical_not(first) for h in heads]
    scale = jnp.asarray(HEAD_DIM ** -0.5, BF16)
    last_tile = sub * pl.num_programs(2) - 1

    for u in range(sub):
        tile = sub * i + u
        case = jnp.where(tile == 0, 1, jnp.where(tile == last_tile, 2, 0))
        qrows = slice(u * ATT_TQ, (u + 1) * ATT_TQ)
        krows = slice(u * ATT_TQ, u * ATT_TQ + 4 * ATT_TK)
        o, lse = [], []
        for g in range(0, ATT_HEADS, ATT_GROUP):
            grp = range(g, g + ATT_GROUP)
            qm = [jnp.where(sel[h], q_ref[qrows, cols[h]] * scale, jnp.zeros((ATT_TQ, LANES), BF16))
                  for h in grp]
            s = [_dot_nt(q_, kc_scr[krows, cols[h]]) + bias_ref[case, h] for q_, h in zip(qm, grp)]
            m = [jnp.max(x, axis=-1, keepdims=True) for x in s]
            p = [jnp.exp(x - mx) for x, mx in zip(s, m)]
            l = [jnp.sum(x, axis=-1, keepdims=True) for x in p]
            o += [jnp.dot(p_.astype(BF16), vc_scr[krows, cols[h]], preferred_element_type=F32) * (1.0 / l_)
                  for p_, l_, h in zip(p, l, grp)]
            lse += [mx + jnp.log(lx) for mx, lx in zip(m, l)]
        for h in range(0, ATT_HEADS, 2):
            o_ref[qrows, cols[h]] = jnp.where(first, o[h], o[h + 1]).astype(o_ref.dtype)
        lse_tile = jnp.zeros((ATT_TQ, LANES), F32)
        for h in heads:
            lse_tile = jnp.where(lane == h, lse[h], lse_tile)
        lse_ref[qrows, :] = lse_tile


def _attention_bias(slopes, dil):
    nk = 4 * ATT_TK
    row = jnp.arange(ATT_TQ)[:, None]
    col = jnp.arange(nk)[None, :]
    rel = col - BAND_HALF - row
    in_band = jnp.abs(rel) <= BAND_HALF
    dist = (jnp.abs(rel) * dil).astype(F32)
    inside = jnp.stack([in_band, in_band & (col >= BAND_HALF), in_band & (col < BAND_HALF + ATT_TQ)])
    return jnp.where(inside[:, None], -slopes[None, :, None, None] * dist, NEG_INF)


def _attention(view, slopes, batch, seq, dil):
    L = seq // dil
    sub = 4 if L % (4 * ATT_TQ) == 0 else 2
    step_q = sub * ATT_TQ
    assert L % step_q == 0 and L // ATT_TQ >= 2, "first and last query block must differ"
    n_kv = step_q // ATT_TK + 2
    nkb = L // ATT_TK
    ratio = step_q // ATT_TK
    bias = _attention_bias(slopes, dil)

    def kv_spec(which, j):
        def imap(b, r, i):
            blk = jnp.clip(i * ratio - 1 + j, 0, nkb - 1)
            return (which, b, blk, r)
        return pl.BlockSpec((None, None, ATT_TK, ATT_WIDTH), imap)

    o, lse = pl.pallas_call(
        functools.partial(_attn_kernel, sub=sub),
        out_shape=[jax.ShapeDtypeStruct((batch, L, dil * ATT_WIDTH), BF16),
                   jax.ShapeDtypeStruct((batch, dil, L, LANES), F32)],
        grid=(batch, dil, L // step_q),
        in_specs=[pl.BlockSpec((None, None, step_q, ATT_WIDTH), lambda b, r, i: (0, b, i, r))]
        + [kv_spec(1, j) for j in range(n_kv)] + [kv_spec(2, j) for j in range(n_kv)]
        + [pl.BlockSpec(bias.shape, lambda b, r, i: (0, 0, 0, 0))],
        out_specs=[pl.BlockSpec((None, step_q, ATT_WIDTH), lambda b, r, i: (b, i, r)),
                   pl.BlockSpec((None, None, step_q, LANES), lambda b, r, i: (b, r, i, 0))],
        scratch_shapes=[pltpu.VMEM((n_kv * ATT_TK, ATT_WIDTH), BF16),
                        pltpu.VMEM((n_kv * ATT_TK, ATT_WIDTH), BF16)],
        compiler_params=_cparams(("parallel", "parallel", "arbitrary")),
        name=f"attn_d{dil}",
    )(view, *([view] * (2 * n_kv)), bias)
    return o, lse.transpose(0, 2, 1, 3).reshape(batch, seq, LANES)


RWKV_TB = 256
PAIRS = RWKV_WIDTH // LANES
SOLVE_CHUNKS = 2


def _shifted(z, prev_row, next_row, mu_prev, mu_next):
    n = z.shape[0]
    ridx = lax.broadcasted_iota(jnp.int32, z.shape, 0)
    z_prev = jnp.where(ridx == 0, prev_row, pltpu.roll(z, 1, 0))
    z_next = jnp.where(ridx == n - 1, next_row, pltpu.roll(z, n - 1, 0))
    return z + mu_prev * (z_prev - z) + mu_next * (z_next - z)


def _stack_pair(x, first):
    zero = jnp.zeros_like(x)
    return jnp.concatenate([jnp.where(first, x, zero), jnp.where(first, zero, x)], axis=0)


def _rwkv_kernel(z_ref, w0_ref, w2_ref, a0_ref, a2_ref, kk_ref, ka_ref, rk_ref, ones_ref, y_ref, bonus_ref,
                 at_s, bt_s, kt_s, rt_s, bd_s, kd_s, v_s, pc_s, st_s, mask_s):
    d = pl.program_id(1)
    step = pl.program_id(2)
    nc = RWKV_TB // CHUNK
    fwd = d == 0
    sgn = jnp.where(fwd, 1, -1)
    wr = step % 2
    rd = 1 - wr
    n2 = 2 * CHUNK

    @pl.when(step == 0)
    def _():
        st_s[...] = jnp.zeros_like(st_s)
        rr = lax.broadcasted_iota(jnp.int32, (n2, n2), 0)
        cc = lax.broadcasted_iota(jnp.int32, (n2, n2), 1)
        same_head = (rr // CHUNK) == (cc // CHUNK)
        tt = rr % CHUNK
        ss = cc % CHUNK
        strict = same_head & ((tt - ss) * sgn > 0)
        masks = (strict,
                 same_head & ((tt - ss) * sgn >= 0),
                 strict & ((tt // 16) == (ss // 16)),
                 strict & ((tt // 32) == (ss // 32)) & ((tt // 16) != (ss // 16)),
                 strict & ((tt // 32) != (ss // 32)),
                 rr == cc)
        for idx, msk in enumerate(masks):
            mask_s[idx] = jnp.where(msk, 1.0, 0.0)

    ti = lax.broadcasted_iota(jnp.int32, (CHUNK, CHUNK), 0)
    si = lax.broadcasted_iota(jnp.int32, (CHUNK, CHUNK), 1)
    tri = jnp.where((ti - si) * sgn >= 0, 1.0, 0.0).astype(BF16)

    def prepare(cp):
        rows = pl.ds(pl.multiple_of(cp * CHUNK, CHUNK), CHUNK)
        r = z_ref[rows, 0:RWKV_WIDTH]
        k = z_ref[rows, RWKV_WIDTH:2 * RWKV_WIDTH]
        v = z_ref[rows, 2 * RWKV_WIDTH:3 * RWKV_WIDTH]
        v_s[wr, rows, :] = v.astype(BF16)
        yield
        wd = z_ref[rows, WD_COL:WD_COL + 128]
        ad = z_ref[rows, AD_COL:AD_COL + 128]
        u = -(w0_ref[...] + _dot(jnp.tanh(wd), w2_ref[...]))
        softplus = jnp.maximum(u, 0.0) + jnp.log(1.0 + jnp.exp(-jnp.abs(u)))
        lw = -jnp.exp(-softplus - 0.5)
        yield
        iclr = jax.nn.sigmoid(a0_ref[...] + _dot(ad, a2_ref[...]))
        ones = ones_ref[...]
        kkr = k * kk_ref[...]
        kk = kkr * lax.rsqrt(jnp.maximum(_head_sums(kkr * kkr, ones), 1e-24))
        yield
        kdir = k * (1.0 + (iclr - 1.0) * ka_ref[...])
        bonus_ref[rows, :] = _head_sums(r * kdir * rk_ref[...], ones) * v
        yield
        b = kk * iclr
        lp_incl = _cumsum_dot(tri, lw)
        total = jnp.sum(lw, axis=0, keepdims=True)
        pc_s[wr, cp] = jnp.broadcast_to(jnp.exp(total), (8, RWKV_WIDTH))
        yield
        e_neg = jnp.exp(-lp_incl)
        at_s[wr, rows, :] = (-kk * jnp.exp(lp_incl - lw)).astype(BF16)
        bt_s[wr, rows, :] = (b * e_neg).astype(BF16)
        yield
        kt_s[wr, rows, :] = (kdir * e_neg).astype(BF16)
        rt_s[wr, rows, :] = r * jnp.exp(lp_incl)
        yield
        e_rest = jnp.exp(total - lp_incl)
        bd_s[wr, rows, :] = (b * e_rest).astype(BF16)
        kd_s[wr, rows, :] = (kdir * e_rest).astype(BF16)

    strict, incl, blk16, off32, off64, eye = range(6)
    lane = lax.broadcasted_iota(jnp.int32, (CHUNK, LANES), 1)
    first = lane < HEAD_DIM

    def solve(chunks, tick):
        items = [(c, p) for c in chunks for p in range(PAIRS)]
        rows = [pl.ds(pl.multiple_of(c * CHUNK, CHUNK), CHUNK) for c, _ in items]
        lanes = [slice(p * LANES, (p + 1) * LANES) for _, p in items]
        each = lambda fn, *lists: [fn(*xs) for xs in zip(*lists)]
        load = lambda ref: [_stack_pair(ref[rd, r_, l_], first) for r_, l_ in zip(rows, lanes)]
        at, bt, kt, bd, kd, vv = (load(ref) for ref in (at_s, bt_s, kt_s, bd_s, kd_s, v_s))
        rt = [_stack_pair(rt_s[rd, r_, l_].astype(BF16), first) for r_, l_ in zip(rows, lanes)]
        tick()

        prod = each(lambda a_, r_, b_, k_: _dot_nt(jnp.concatenate([a_, r_], axis=0),
                                                   jnp.concatenate([b_, k_], axis=0)), at, rt, bt, kt)
        a_ak = [(x[:n2, n2:] * mask_s[strict]).astype(BF16) for x in prod]
        q_bk = [jnp.concatenate([x[n2:, :n2] * mask_s[incl], x[n2:, n2:] * mask_s[incl]], axis=1).astype(BF16)
                for x in prod]
        offs = [[(x[:n2, :n2] * mask_s[m]).astype(BF16) for x in prod] for m in (off32, off64)]
        tick()

        d0 = [x[:n2, :n2] * mask_s[blk16] for x in prod]
        t_inv = [mask_s[eye] + x for x in d0]
        pw = [_dot(x, x).astype(BF16) for x in d0]
        tick()
        for _ in range(2):
            both = each(lambda t_, p_: _dot(jnp.concatenate([t_.astype(BF16), p_], axis=0), p_), t_inv, pw)
            t_inv = each(lambda t_, b_: t_ + b_[:n2], t_inv, both)
            pw = [x[n2:].astype(BF16) for x in both]
            tick()
        t_inv = each(lambda t_, p_: t_ + _dot(t_, p_), t_inv, pw)
        tick()
        for off in offs:
            inner = each(lambda o_, t_: _dot(o_, t_).astype(BF16), off, t_inv)
            t_inv = each(lambda t_, i_: t_ + _dot(t_, i_), t_inv, inner)
            tick()

        av = each(lambda a_, v_: _dot(a_, v_).astype(BF16), a_ak, vv)
        tick()
        wu = each(lambda t_, a_, v_: _dot(t_, jnp.concatenate([a_, v_], axis=1)).astype(BF16),
                  t_inv, at, av)
        tick()
        zero = jnp.zeros((n2, LANES), BF16)
        rhs = each(lambda w_, v_: jnp.concatenate(
            [w_, jnp.concatenate([zero, v_], axis=1)], axis=0), wu, vv)
        ry = each(_dot, q_bk, rhs)
        gh = each(lambda b_, k_, x_: _dot_tn(jnp.concatenate([b_, k_], axis=0), x_),
                  bd, kd, rhs)
        tick()

        st = [st_s[p] for p in range(PAIRS)]
        for j, c in enumerate(chunks):
            sl = slice(j * PAIRS, (j + 1) * PAIRS)
            rt_f = [_stack_pair(rt_s[rd, rows[i], lanes[i]], first) for i in range(sl.start, sl.stop)]
            y_st = each(lambda r_, x_, s_: _dot(r_ + x_[:, :LANES], s_) + x_[:, LANES:], rt_f, ry[sl], st)
            gs = each(lambda g_, s_: _dot(g_[:, :LANES], s_), gh[sl], st)
            pc_row = pc_s[rd, c]
            for p in range(PAIRS):
                i = j * PAIRS + p
                y_ref[rows[i], lanes[i]] = y_st[p][:CHUNK] + y_st[p][CHUNK:]
                pc_col = jnp.sum(mask_s[eye] * pc_row[0:1, lanes[i]], axis=1, keepdims=True)
                st[p] = pc_col * st[p] + gs[p] + gh[i][:, LANES:]
        for p in range(PAIRS):
            st_s[p] = st[p]

    def chunk_body(ci, carry):
        first_chunk = ci * SOLVE_CHUNKS
        stages = itertools.chain(*[prepare(first_chunk + j) for j in range(SOLVE_CHUNKS)])

        def tick():
            for _ in range(SOLVE_CHUNKS):
                next(stages, None)

        solve([jnp.where(fwd, first_chunk + j, nc - 1 - first_chunk - j) for j in range(SOLVE_CHUNKS)], tick)
        for _ in stages:
            pass
        return carry

    def prepare_only(cp, carry):
        for _ in prepare(cp):
            pass
        return carry

    @pl.when(step == 0)
    def _():
        lax.fori_loop(0, nc, prepare_only, 0)

    @pl.when(step > 0)
    def _():
        lax.fori_loop(0, nc // SOLVE_CHUNKS, chunk_body, 0)


def _rwkv(z, batch, seq, w0, w2, a0, a2, k_k, k_a, r_k, ones_bd):
    n_blk = seq // RWKV_TB

    def time_block(d, i):
        return jnp.where(d == 0, i, n_blk - 1 - i)

    def tmap(d, i):
        return time_block(d, jnp.minimum(i, n_blk - 1))

    def smap(d, i):
        return time_block(d, jnp.maximum(i - 1, 0))

    vec = lambda: pl.BlockSpec((1, RWKV_WIDTH), lambda b, d, i: (0, 0))
    in_specs = [
        pl.BlockSpec((None, RWKV_TB, N_RWKV_COLS), lambda b, d, i: (b, tmap(d, i), 0)),
        pl.BlockSpec((None, 1, RWKV_WIDTH), lambda b, d, i: (d, 0, 0)),
        pl.BlockSpec((None, 128, RWKV_WIDTH), lambda b, d, i: (d, 0, 0)),
        pl.BlockSpec((None, 1, RWKV_WIDTH), lambda b, d, i: (d, 0, 0)),
        pl.BlockSpec((None, 128, RWKV_WIDTH), lambda b, d, i: (d, 0, 0)),
        vec(), vec(), vec(),
        pl.BlockSpec((HEAD_GROUP, HEAD_GROUP), lambda b, d, i: (0, 0)),
    ]
    out_spec = lambda m: pl.BlockSpec((None, None, RWKV_TB, RWKV_WIDTH), lambda b, d, i: (d, b, m(d, i), 0))
    tok = lambda dt: pltpu.VMEM((2, RWKV_TB, RWKV_WIDTH), dt)
    return pl.pallas_call(
        _rwkv_kernel,
        out_shape=[jax.ShapeDtypeStruct((2, batch, seq, RWKV_WIDTH), F32)] * 2,
        grid=(batch, 2, n_blk + 1),
        in_specs=in_specs,
        out_specs=[out_spec(smap), out_spec(tmap)],
        scratch_shapes=[tok(BF16), tok(BF16), tok(BF16), tok(F32), tok(BF16), tok(BF16), tok(BF16),
                        pltpu.VMEM((2, RWKV_TB // CHUNK, 8, RWKV_WIDTH), F32),
                        pltpu.VMEM((PAIRS, LANES, LANES), F32),
                        pltpu.VMEM((6, 2 * CHUNK, 2 * CHUNK), F32)],
        compiler_params=_cparams(("parallel", "parallel", "arbitrary")),
        name="rwkv",
    )(z, w0, w2, a0, a2, k_k, k_a, r_k, ones_bd)


MIX_TM = 256


def _mixout_kernel(x_ref, o1, o2, o3, l1, l2, l3, y_ref, bonus_ref, gd_ref, g2_ref, lnw_ref, lnb_ref,
                   avg_ref, spread_ref, wo_ref, out_ref, o_scr):
    n_tiles = ATT_WIDTH // LANES
    o_refs = (o1, o2, o3)
    for k, d in enumerate(DILATIONS):
        for r in range(d if d > 1 else 0):
            for c in range(n_tiles):
                col = r * ATT_WIDTH + c * LANES
                o_scr[k, c, pl.ds(r, MIX_TM // d, stride=d), :] = o_refs[k][:, col:col + LANES].astype(F32)
    m = jnp.maximum(jnp.maximum(l1[...], l2[...]), l3[...])
    e = [jnp.exp(l[...] - m) for l in (l1, l2, l3)]
    inv = 1.0 / (e[0] + e[1] + e[2])
    spread2 = jnp.concatenate([spread_ref[...]] * 2, axis=0)
    att = jnp.zeros((MIX_TM, ATT_WIDTH), F32)
    for k, (d, ep) in enumerate(zip(DILATIONS, e)):
        wts = jnp.dot(jnp.concatenate(_split2(ep * inv), axis=1), spread2, preferred_element_type=F32)
        o_tok = (o_refs[k][...].astype(F32) if d == 1
                 else jnp.concatenate([o_scr[k, c] for c in range(n_tiles)], axis=1))
        att = att + wts * o_tok

    y = y_ref[0] + y_ref[1]
    avg = avg_ref[...]
    yc = y - _head_sums(y, avg)
    var = _head_sums(yc * yc, avg)
    yn = yc * lax.rsqrt(var + RWKV_LN_EPS) * lnw_ref[...] + lnb_ref[...]
    gate = _dot(jax.nn.sigmoid(gd_ref[...]), g2_ref[...])
    rw = (yn + bonus_ref[0] + bonus_ref[1]) * gate

    mixed = jnp.dot(att.astype(BF16), wo_ref[0:ATT_WIDTH, :], preferred_element_type=F32)
    mixed = mixed + jnp.dot(rw.astype(BF16), wo_ref[ATT_WIDTH:, :], preferred_element_type=F32)
    out_ref[...] = x_ref[...] + mixed


def _mixout(x1, os, lses, y, bonus, z, batch, seq, g2, ln_w, ln_b, avg_bd, spread, w_out):
    gcol = GD_COL // 128
    tok = lambda w: pl.BlockSpec((None, MIX_TM, w), lambda b, i: (b, i, 0))
    const = lambda shape: pl.BlockSpec(shape, lambda b, i: (0,) * len(shape))
    dir_spec = lambda: pl.BlockSpec((2, None, MIX_TM, RWKV_WIDTH), lambda b, i: (0, b, i, 0))
    view = lambda d: pl.BlockSpec((None, MIX_TM // d, d * ATT_WIDTH), lambda b, i: (b, i, 0))
    in_specs = [tok(D_MODEL)] + [view(d) for d in DILATIONS] + [tok(LANES)] * 3 + [dir_spec(), dir_spec()] + [
        pl.BlockSpec((None, MIX_TM, 128), lambda b, i: (b, i, gcol)),
        const((128, RWKV_WIDTH)),
        const((1, RWKV_WIDTH)), const((1, RWKV_WIDTH)), const((HEAD_GROUP, HEAD_GROUP)),
        const((LANES, ATT_WIDTH)), const((D_MODEL, D_MODEL)),
    ]
    return pl.pallas_call(
        _mixout_kernel,
        out_shape=jax.ShapeDtypeStruct((batch, seq, D_MODEL), F32),
        grid=(batch, seq // MIX_TM),
        in_specs=in_specs,
        out_specs=tok(D_MODEL),
        scratch_shapes=[pltpu.VMEM((len(DILATIONS), ATT_WIDTH // LANES, MIX_TM, LANES), F32)],
        compiler_params=_cparams(("parallel", "arbitrary")),
        name="mixout",
    )(x1, *os, *lses, y, bonus, z, g2, ln_w, ln_b, avg_bd, spread, w_out)


def _block_diag(width, block, value):
    idx = jnp.arange(width) // block
    return jnp.where(idx[:, None] == idx[None, :], value, 0.0).astype(BF16)


def _trunk(x, p):
    batch, seq, _ = x.shape
    xf = x.reshape(batch * seq, D_MODEL)
    x1 = _ffn(xf, p["ffn1_norm"], p["ffn1_gate"], p["ffn1_up"], p["ffn1_down"], p["final_norm"],
              final_norm=False)
    views = _inproj_views(x1, p["mix_norm"], p["w_in_att"])
    z = _inproj_shift(x1, p["mix_norm"], p["w_in_rwkv"], p["mu_prev"], p["mu_next"], seq,
                      tn=1152).reshape(batch, seq, N_RWKV_COLS)
    os, lses = [], []
    for dil, view in zip(DILATIONS, views):
        o, lse = _attention(view.reshape(3, batch, seq // dil, dil * ATT_WIDTH), p["slopes"], batch, seq, dil)
        os.append(o)
        lses.append(lse)
    y, bonus = _rwkv(z, batch, seq, p["w0"], p["w2"], p["a0"], p["a2"],
                     p["k_k"], p["k_a"], p["r_k"], p["ones_bd"])
    x2 = _mixout(x1.reshape(batch, seq, D_MODEL), os, lses, y, bonus, z, batch, seq,
                 p["g2"], p["ln_x_w"], p["ln_x_b"], p["avg_bd"], p["spread"], p["w_out"])
    out = _ffn(x2.reshape(batch * seq, D_MODEL), p["ffn2_norm"], p["ffn2_gate"], p["ffn2_up"],
               p["ffn2_down"], p["final_norm"], final_norm=True)
    return out.reshape(batch, seq, D_MODEL)


def kernel(x_prompt, x_sample, ffn1_norm, ffn1_gate, ffn1_up, ffn1_down, mix_norm, w_in, w_out, mu_prev, mu_next, w0_f, w2_f, w0_b, w2_b, a0_f, a2_f, a0_b, a2_b, g2, k_k, k_a, r_k, ln_x_w, ln_x_b, ffn2_norm, ffn2_gate, ffn2_up, ffn2_down, final_norm):
    assert ffn1_norm.shape[0] == 1, "single layer"
    row = lambda t: t.reshape(1, -1)
    zero64 = jnp.zeros((64, RWKV_WIDTH), F32)
    w_in_b = w_in[0].astype(BF16)
    p = {
        "ffn1_norm": row(ffn1_norm[0]), "ffn1_gate": ffn1_gate[0].astype(BF16),
        "ffn1_up": ffn1_up[0].astype(BF16), "ffn1_down": ffn1_down[0].astype(BF16),
        "mix_norm": row(mix_norm[0]),
        "w_in_att": w_in_b[:, :N_ATT_COLS], "w_in_rwkv": w_in_b[:, N_ATT_COLS:],
        "w_out": w_out[0].astype(BF16),
        "mu_prev": row(mu_prev[0]), "mu_next": row(mu_next[0]),
        "w0": jnp.stack([row(w0_f[0]), row(w0_b[0])]),
        "w2": jnp.stack([jnp.concatenate([w2_f[0], zero64]), jnp.concatenate([zero64, w2_b[0]])]).astype(BF16),
        "a0": jnp.stack([row(a0_f[0]), row(a0_b[0])]),
        "a2": jnp.stack([jnp.concatenate([a2_f[0], zero64]), jnp.concatenate([zero64, a2_b[0]])]).astype(BF16),
        "g2": g2[0].astype(BF16),
        "k_k": row(k_k[0]), "k_a": row(k_a[0]), "r_k": row(r_k[0]),
        "ln_x_w": row(ln_x_w[0]), "ln_x_b": row(ln_x_b[0]),
        "ffn2_norm": row(ffn2_norm[0]), "ffn2_gate": ffn2_gate[0].astype(BF16),
        "ffn2_up": ffn2_up[0].astype(BF16), "ffn2_down": ffn2_down[0].astype(BF16),
        "final_norm": row(final_norm),
        "slopes": jnp.exp2(-8.0 * jnp.arange(1, ATT_HEADS + 1, dtype=F32) / ATT_HEADS),
        "ones_bd": _block_diag(HEAD_GROUP, HEAD_DIM, 1.0),
        "avg_bd": _block_diag(HEAD_GROUP, HEAD_DIM, 1.0 / HEAD_DIM),
        "spread": (jnp.arange(LANES)[:, None] == jnp.arange(ATT_WIDTH)[None, :] // HEAD_DIM).astype(BF16),
    }
    return (_trunk(x_prompt, p), _trunk(x_sample, p))
```

```python
import functools
import itertools

import jax
import jax.numpy as jnp
from jax import lax
from jax.experimental import pallas as pl
from jax.experimental.pallas import tpu as pltpu

F32 = jnp.float32
BF16 = jnp.bfloat16

D_MODEL = 2048
D_FF = 5632
HEAD_DIM = 64
ATT_WIDTH = 1024
ATT_HEADS = 16
RWKV_WIDTH = 1024
N_ATT_COLS = 3 * ATT_WIDTH
N_RWKV_COLS = 3 * RWKV_WIDTH + 64 * 4 + 128
WD_COL = 3 * RWKV_WIDTH
AD_COL = WD_COL + 128
GD_COL = AD_COL + 128
DILATIONS = (1, 4, 16)
BAND_HALF = 64
NORM_EPS = 1e-6
RWKV_LN_EPS = 64e-5
NEG_INF = -1e30

LANES = 128
CHUNK = 64
VMEM_LIMIT = 56 * 1024 * 1024


def _cparams(sem):
    return pltpu.CompilerParams(dimension_semantics=sem, vmem_limit_bytes=VMEM_LIMIT)


def _dot(a, b):
    return jnp.dot(a.astype(BF16), b.astype(BF16), preferred_element_type=F32)


def _dot_nt(a, b):
    return lax.dot_general(a.astype(BF16), b.astype(BF16), (((1,), (1,)), ((), ())),
                           preferred_element_type=F32)


def _dot_tn(a, b):
    return lax.dot_general(a.astype(BF16), b.astype(BF16), (((0,), (0,)), ((), ())),
                           preferred_element_type=F32)


def _split2(x):
    hi = x.astype(BF16)
    lo = (x - hi.astype(F32)).astype(BF16)
    return hi, lo


def _head_sums(x, seg2):
    hi, lo = _split2(x)
    outs = []
    for g in range(x.shape[1] // LANES):
        cols = slice(g * LANES, (g + 1) * LANES)
        outs.append(jnp.dot(jnp.concatenate([hi[:, cols], lo[:, cols]], axis=1), seg2,
                            preferred_element_type=F32))
    return jnp.concatenate(outs, axis=1)


def _cumsum_dot(tri_bf16, x):
    hi, lo = _split2(x)
    return (jnp.dot(tri_bf16, hi, preferred_element_type=F32)
            + jnp.dot(tri_bf16, lo, preferred_element_type=F32))


def _rmsnorm(x, g):
    return x * lax.rsqrt(jnp.mean(x * x, axis=-1, keepdims=True) + NORM_EPS) * g


def _ffn_kernel(x_ref, g_ref, wg_ref, wu_ref, wd_ref, fg_ref, o_ref, n_scr, *, final_norm):
    f = pl.program_id(1)

    @pl.when(f == 0)
    def _():
        x = x_ref[...]
        n_scr[...] = _rmsnorm(x, g_ref[...]).astype(BF16)
        o_ref[...] = x

    n = n_scr[...]
    hg = jnp.dot(n, wg_ref[...], preferred_element_type=F32)
    hu = jnp.dot(n, wu_ref[...], preferred_element_type=F32)
    h = (hg * jax.nn.sigmoid(hg) * (0.5 * hu)).astype(BF16)
    o_ref[...] += jnp.dot(h, wd_ref[...], preferred_element_type=F32)

    if final_norm:
        @pl.when(f == pl.num_programs(1) - 1)
        def _():
            o_ref[...] = _rmsnorm(o_ref[...], fg_ref[...])


def _ffn(x, g, wg, wu, wd, fg, *, final_norm, tm=1024, tf=512):
    n_tok = x.shape[0]
    return pl.pallas_call(
        functools.partial(_ffn_kernel, final_norm=final_norm),
        out_shape=jax.ShapeDtypeStruct((n_tok, D_MODEL), F32),
        grid=(n_tok // tm, D_FF // tf),
        in_specs=[
            pl.BlockSpec((tm, D_MODEL), lambda i, f: (i, 0)),
            pl.BlockSpec((1, D_MODEL), lambda i, f: (0, 0)),
            pl.BlockSpec((D_MODEL, tf), lambda i, f: (0, f)),
            pl.BlockSpec((D_MODEL, tf), lambda i, f: (0, f)),
            pl.BlockSpec((tf, D_MODEL), lambda i, f: (f, 0)),
            pl.BlockSpec((1, D_MODEL), lambda i, f: (0, 0)),
        ],
        out_specs=pl.BlockSpec((tm, D_MODEL), lambda i, f: (i, 0)),
        scratch_shapes=[pltpu.VMEM((tm, D_MODEL), BF16)],
        compiler_params=_cparams(("parallel", "arbitrary")),
        name="ffn_final" if final_norm else "ffn",
    )(x, g, wg, wu, wd, fg)


def _inproj_shift_kernel(x_ref, xp_ref, xn_ref, g_ref, w_ref, mup_ref, mun_ref, o_ref, n_scr, nh_scr, *, seq):
    i = pl.program_id(0)
    tm = x_ref.shape[0]

    @pl.when(pl.program_id(1) == 0)
    def _():
        g = g_ref[...]
        n_scr[...] = _rmsnorm(x_ref[...], g).astype(BF16)
        halo = jnp.concatenate([xp_ref[...], xn_ref[...]], axis=0)
        nh_scr[...] = _rmsnorm(halo, g).astype(BF16)

    w = w_ref[...]
    z = jnp.dot(n_scr[...], w, preferred_element_type=F32)
    z_halo = jnp.dot(nh_scr[...], w, preferred_element_type=F32)
    prev_row = jnp.where((i * tm) % seq == 0, 0.0, z_halo[7:8, :])
    next_row = jnp.where(((i + 1) * tm) % seq == 0, 0.0, z_halo[8:9, :])
    o_ref[...] = _shifted(z, prev_row, next_row, mup_ref[...], mun_ref[...])


def _inproj_shift(x, g, w, mu_prev, mu_next, seq, *, tn, tm=1024):
    n_tok = x.shape[0]
    n_cols = w.shape[1]
    hb = tm // 8
    return pl.pallas_call(
        functools.partial(_inproj_shift_kernel, seq=seq),
        out_shape=jax.ShapeDtypeStruct((n_tok, n_cols), F32),
        grid=(n_tok // tm, n_cols // tn),
        in_specs=[
            pl.BlockSpec((tm, D_MODEL), lambda i, j: (i, 0)),
            pl.BlockSpec((8, D_MODEL), lambda i, j: (jnp.maximum(i * hb - 1, 0), 0)),
            pl.BlockSpec((8, D_MODEL), lambda i, j: (jnp.minimum((i + 1) * hb, n_tok // 8 - 1), 0)),
            pl.BlockSpec((1, D_MODEL), lambda i, j: (0, 0)),
            pl.BlockSpec((D_MODEL, tn), lambda i, j: (0, j)),
            pl.BlockSpec((1, tn), lambda i, j: (0, j)),
            pl.BlockSpec((1, tn), lambda i, j: (0, j)),
        ],
        out_specs=pl.BlockSpec((tm, tn), lambda i, j: (i, j)),
        scratch_shapes=[pltpu.VMEM((tm, D_MODEL), BF16), pltpu.VMEM((16, D_MODEL), BF16)],
        compiler_params=_cparams(("parallel", "arbitrary")),
        name="inproj_shift",
    )(x, x, x, g, w, mu_prev, mu_next)


def _inproj_views_kernel(x_ref, g_ref, w_ref, *refs):
    out_refs, (n_scr, acc_scr) = refs[:len(DILATIONS)], refs[len(DILATIONS):]

    @pl.when(pl.program_id(1) == 0)
    def _():
        n_scr[...] = _rmsnorm(x_ref[...], g_ref[...]).astype(BF16)

    acc = jnp.dot(n_scr[...], w_ref[...], preferred_element_type=F32)
    out_refs[0][...] = acc.astype(BF16)
    n_tiles, tm, _ = acc_scr.shape
    for c in range(n_tiles):
        acc_scr[c] = acc[:, c * LANES:(c + 1) * LANES]
    for d, o_ref in zip(DILATIONS[1:], out_refs[1:]):
        for r in range(d):
            for c in range(n_tiles):
                col = r * ATT_WIDTH + c * LANES
                o_ref[:, col:col + LANES] = acc_scr[c, pl.ds(r, tm // d, stride=d), :].astype(BF16)


def _inproj_views(x, g, w, *, tm=1024):
    n_tok = x.shape[0]
    return pl.pallas_call(
        _inproj_views_kernel,
        out_shape=[jax.ShapeDtypeStruct((3, n_tok // d, d * ATT_WIDTH), BF16) for d in DILATIONS],
        grid=(n_tok // tm, 3),
        in_specs=[
            pl.BlockSpec((tm, D_MODEL), lambda i, j: (i, 0)),
            pl.BlockSpec((1, D_MODEL), lambda i, j: (0, 0)),
            pl.BlockSpec((D_MODEL, ATT_WIDTH), lambda i, j: (0, j)),
        ],
        out_specs=[pl.BlockSpec((None, tm // d, d * ATT_WIDTH), lambda i, j: (j, i, 0)) for d in DILATIONS],
        scratch_shapes=[pltpu.VMEM((tm, D_MODEL), BF16), pltpu.VMEM((ATT_WIDTH // LANES, tm, LANES), F32)],
        compiler_params=_cparams(("parallel", "arbitrary")),
        name="inproj_views",
    )(x, g, w)


ATT_TQ = 128
ATT_TK = 64
ATT_GROUP = 8


def _attn_kernel(q_ref, *refs, sub):
    nkb = sub * ATT_TQ // ATT_TK + 2
    k_refs, v_refs = refs[:nkb], refs[nkb:2 * nkb]
    bias_ref, o_ref, lse_ref, kc_scr, vc_scr = refs[2 * nkb:]
    i = pl.program_id(2)
    for j, (kr, vr) in enumerate(zip(k_refs, v_refs)):
        kc_scr[j * ATT_TK:(j + 1) * ATT_TK, :] = kr[...]
        vc_scr[j * ATT_TK:(j + 1) * ATT_TK, :] = vr[...]

    lane = lax.broadcasted_iota(jnp.int32, (ATT_TQ, LANES), 1)
    first = lane < HEAD_DIM
    heads = range(ATT_HEADS)
    cols = [slice((h // 2) * LANES, (h // 2 + 1) * LANES) for h in heads]
    sel = [first if h % 2 == 0 else jnp.logical_not(first) for h in heads]
    scale = jnp.asarray(HEAD_DIM ** -0.5, BF16)
    last_tile = sub * pl.num_programs(2) - 1

    for u in range(sub):
        tile = sub * i + u
        case = jnp.where(tile == 0, 1, jnp.where(tile == last_tile, 2, 0))
        qrows = slice(u * ATT_TQ, (u + 1) * ATT_TQ)
        krows = slice(u * ATT_TQ, u * ATT_TQ + 4 * ATT_TK)
        o, lse = [], []
        for g in range(0, ATT_HEADS, ATT_GROUP):
            grp = range(g, g + ATT_GROUP)
            qm = [jnp.where(sel[h], q_ref[qrows, cols[h]] * scale, jnp.zeros((ATT_TQ, LANES), BF16))
                  for h in grp]
            s = [_dot_nt(q_, kc_scr[krows, cols[h]]) + bias_ref[case, h] for q_, h in zip(qm, grp)]
            m = [jnp.max(x, axis=-1, keepdims=True) for x in s]
            p = [jnp.exp(x - mx) for x, mx in zip(s, m)]
            l = [jnp.sum(x, axis=-1, keepdims=True) for x in p]
            o += [jnp.dot(p_.astype(BF16), vc_scr[krows, cols[h]], preferred_element_type=F32) * (1.0 / l_)
                  for p_, l_, h in zip(p, l, grp)]
            lse += [mx + jnp.log(lx) for mx, lx in zip(m, l)]
        for h in range(0, ATT_HEADS, 2):
            o_ref[qrows, cols[h]] = jnp.where(first, o[h], o[h + 1]).astype(o_ref.dtype)
        lse_tile = jnp.zeros((ATT_TQ, LANES), F32)
        for h in heads:
            lse_tile = jnp.where(lane == h, lse[h], lse_tile)
        lse_ref[qrows, :] = lse_tile


def _attention_bias(slopes, dil):
    nk = 4 * ATT_TK
    row = jnp.arange(ATT_TQ)[:, None]
    col = jnp.arange(nk)[None, :]
    rel = col - BAND_HALF - row
    in_band = jnp.abs(rel) <= BAND_HALF
    dist = (jnp.abs(rel) * dil).astype(F32)
    inside = jnp.stack([in_band, in_band & (col >= BAND_HALF), in_band & (col < BAND_HALF + ATT_TQ)])
    return jnp.where(inside[:, None], -slopes[None, :, None, None] * dist, NEG_INF)


def _attention(view, slopes, batch, seq, dil):
    L = seq // dil
    sub = 4 if L % (4 * ATT_TQ) == 0 else 2
    step_q = sub * ATT_TQ
    assert L % step_q == 0 and L // ATT_TQ >= 2, "first and last query block must differ"
    n_kv = step_q // ATT_TK + 2
    nkb = L // ATT_TK
    ratio = step_q // ATT_TK
    bias = _attention_bias(slopes, dil)

    def kv_spec(which, j):
        def imap(b, r, i):
            blk = jnp.clip(i * ratio - 1 + j, 0, nkb - 1)
            return (which, b, blk, r)
        return pl.BlockSpec((None, None, ATT_TK, ATT_WIDTH), imap)

    o, lse = pl.pallas_call(
        functools.partial(_attn_kernel, sub=sub),
        out_shape=[jax.ShapeDtypeStruct((batch, L, dil * ATT_WIDTH), BF16),
                   jax.ShapeDtypeStruct((batch, dil, L, LANES), F32)],
        grid=(batch, dil, L // step_q),
        in_specs=[pl.BlockSpec((None, None, step_q, ATT_WIDTH), lambda b, r, i: (0, b, i, r))]
        + [kv_spec(1, j) for j in range(n_kv)] + [kv_spec(2, j) for j in range(n_kv)]
        + [pl.BlockSpec(bias.shape, lambda b, r, i: (0, 0, 0, 0))],
        out_specs=[pl.BlockSpec((None, step_q, ATT_WIDTH), lambda b, r, i: (b, i, r)),
                   pl.BlockSpec((None, None, step_q, LANES), lambda b, r, i: (b, r, i, 0))],
        scratch_shapes=[pltpu.VMEM((n_kv * ATT_TK, ATT_WIDTH), BF16),
                        pltpu.VMEM((n_kv * ATT_TK, ATT_WIDTH), BF16)],
        compiler_params=_cparams(("parallel", "parallel", "arbitrary")),
        name=f"attn_d{dil}",
    )(view, *([view] * (2 * n_kv)), bias)
    return o, lse.transpose(0, 2, 1, 3).reshape(batch, seq, LANES)


RWKV_TB = 256
PAIRS = RWKV_WIDTH // LANES
SOLVE_CHUNKS = 2


def _shifted(z, prev_row, next_row, mu_prev, mu_next):
    n = z.shape[0]
    ridx = lax.broadcasted_iota(jnp.int32, z.shape, 0)
    z_prev = jnp.where(ridx == 0, prev_row, pltpu.roll(z, 1, 0))
    z_next = jnp.where(ridx == n - 1, next_row, pltpu.roll(z, n - 1, 0))
    return z + mu_prev * (z_prev - z) + mu_next * (z_next - z)


def _stack_pair(x, first):
    zero = jnp.zeros_like(x)
    return jnp.concatenate([jnp.where(first, x, zero), jnp.where(first, zero, x)], axis=0)


def _rwkv_kernel(z_ref, w0_ref, w2_ref, a0_ref, a2_ref, kk_ref, ka_ref, rk_ref, ones_ref, y_ref, bonus_ref,
                 at_s, bt_s, kt_s, rt_s, bd_s, kd_s, v_s, pc_s, st_s, mask_s):
    d = pl.program_id(1)
    step = pl.program_id(2)
    nc = RWKV_TB // CHUNK
    fwd = d == 0
    sgn = jnp.where(fwd, 1, -1)
    wr = step % 2
    rd = 1 - wr
    n2 = 2 * CHUNK

    @pl.when(step == 0)
    def _():
        st_s[...] = jnp.zeros_like(st_s)
        rr = lax.broadcasted_iota(jnp.int32, (n2, n2), 0)
        cc = lax.broadcasted_iota(jnp.int32, (n2, n2), 1)
        same_head = (rr // CHUNK) == (cc // CHUNK)
        tt = rr % CHUNK
        ss = cc % CHUNK
        strict = same_head & ((tt - ss) * sgn > 0)
        masks = (strict,
                 same_head & ((tt - ss) * sgn >= 0),
                 strict & ((tt // 16) == (ss // 16)),
                 strict & ((tt // 32) == (ss // 32)) & ((tt // 16) != (ss // 16)),
                 strict & ((tt // 32) != (ss // 32)),
                 rr == cc)
        for idx, msk in enumerate(masks):
            mask_s[idx] = jnp.where(msk, 1.0, 0.0)

    ti = lax.broadcasted_iota(jnp.int32, (CHUNK, CHUNK), 0)
    si = lax.broadcasted_iota(jnp.int32, (CHUNK, CHUNK), 1)
    tri = jnp.where((ti - si) * sgn >= 0, 1.0, 0.0).astype(BF16)

    def prepare(cp):
        rows = pl.ds(pl.multiple_of(cp * CHUNK, CHUNK), CHUNK)
        r = z_ref[rows, 0:RWKV_WIDTH]
        k = z_ref[rows, RWKV_WIDTH:2 * RWKV_WIDTH]
        v = z_ref[rows, 2 * RWKV_WIDTH:3 * RWKV_WIDTH]
        v_s[wr, rows, :] = v.astype(BF16)
        yield
        wd = z_ref[rows, WD_COL:WD_COL + 128]
        ad = z_ref[rows, AD_COL:AD_COL + 128]
        u = -(w0_ref[...] + _dot(jnp.tanh(wd), w2_ref[...]))
        softplus = jnp.maximum(u, 0.0) + jnp.log(1.0 + jnp.exp(-jnp.abs(u)))
        lw = -jnp.exp(-softplus - 0.5)
        yield
        iclr = jax.nn.sigmoid(a0_ref[...] + _dot(ad, a2_ref[...]))
        ones = ones_ref[...]
        kkr = k * kk_ref[...]
        kk = kkr * lax.rsqrt(jnp.maximum(_head_sums(kkr * kkr, ones), 1e-24))
        yield
        kdir = k * (1.0 + (iclr - 1.0) * ka_ref[...])
        bonus_ref[rows, :] = _head_sums(r * kdir * rk_ref[...], ones) * v
        yield
        b = kk * iclr
        lp_incl = _cumsum_dot(tri, lw)
        total = jnp.sum(lw, axis=0, keepdims=True)
        pc_s[wr, cp] = jnp.broadcast_to(jnp.exp(total), (8, RWKV_WIDTH))
        yield
        e_neg = jnp.exp(-lp_incl)
        at_s[wr, rows, :] = (-kk * jnp.exp(lp_incl - lw)).astype(BF16)
        bt_s[wr, rows, :] = (b * e_neg).astype(BF16)
        yield
        kt_s[wr, rows, :] = (kdir * e_neg).astype(BF16)
        rt_s[wr, rows, :] = r * jnp.exp(lp_incl)
        yield
        e_rest = jnp.exp(total - lp_incl)
        bd_s[wr, rows, :] = (b * e_rest).astype(BF16)
        kd_s[wr, rows, :] = (kdir * e_rest).astype(BF16)

    strict, incl, blk16, off32, off64, eye = range(6)
    lane = lax.broadcasted_iota(jnp.int32, (CHUNK, LANES), 1)
    first = lane < HEAD_DIM

    def solve(chunks, tick):
        items = [(c, p) for c in chunks for p in range(PAIRS)]
        rows = [pl.ds(pl.multiple_of(c * CHUNK, CHUNK), CHUNK) for c, _ in items]
        lanes = [slice(p * LANES, (p + 1) * LANES) for _, p in items]
        each = lambda fn, *lists: [fn(*xs) for xs in zip(*lists)]
        load = lambda ref: [_stack_pair(ref[rd, r_, l_], first) for r_, l_ in zip(rows, lanes)]
        at, bt, kt, bd, kd, vv = (load(ref) for ref in (at_s, bt_s, kt_s, bd_s, kd_s, v_s))
        rt = [_stack_pair(rt_s[rd, r_, l_].astype(BF16), first) for r_, l_ in zip(rows, lanes)]
        tick()

        prod = each(lambda a_, r_, b_, k_: _dot_nt(jnp.concatenate([a_, r_], axis=0),
                                                   jnp.concatenate([b_, k_], axis=0)), at, rt, bt, kt)
        a_ak = [(x[:n2, n2:] * mask_s[strict]).astype(BF16) for x in prod]
        q_bk = [jnp.concatenate([x[n2:, :n2] * mask_s[incl], x[n2:, n2:] * mask_s[incl]], axis=1).astype(BF16)
                for x in prod]
        offs = [[(x[:n2, :n2] * mask_s[m]).astype(BF16) for x in prod] for m in (off32, off64)]
        tick()

        d0 = [x[:n2, :n2] * mask_s[blk16] for x in prod]
        t_inv = [mask_s[eye] + x for x in d0]
        pw = [_dot(x, x).astype(BF16) for x in d0]
        tick()
        for _ in range(2):
            both = each(lambda t_, p_: _dot(jnp.concatenate([t_.astype(BF16), p_], axis=0), p_), t_inv, pw)
            t_inv = each(lambda t_, b_: t_ + b_[:n2], t_inv, both)
            pw = [x[n2:].astype(BF16) for x in both]
            tick()
        t_inv = each(lambda t_, p_: t_ + _dot(t_, p_), t_inv, pw)
        tick()
        for off in offs:
            inner = each(lambda o_, t_: _dot(o_, t_).astype(BF16), off, t_inv)
            t_inv = each(lambda t_, i_: t_ + _dot(t_, i_), t_inv, inner)
            tick()

        av = each(lambda a_, v_: _dot(a_, v_).astype(BF16), a_ak, vv)
        tick()
        wu = each(lambda t_, a_, v_: _dot(t_, jnp.concatenate([a_, v_], axis=1)).astype(BF16),
                  t_inv, at, av)
        tick()
        zero = jnp.zeros((n2, LANES), BF16)
        rhs = each(lambda w_, v_: jnp.concatenate(
            [w_, jnp.concatenate([zero, v_], axis=1)], axis=0), wu, vv)
        ry = each(_dot, q_bk, rhs)
        gh = each(lambda b_, k_, x_: _dot_tn(jnp.concatenate([b_, k_], axis=0), x_),
                  bd, kd, rhs)
        tick()

        st = [st_s[p] for p in range(PAIRS)]
        for j, c in enumerate(chunks):
            sl = slice(j * PAIRS, (j + 1) * PAIRS)
            rt_f = [_stack_pair(rt_s[rd, rows[i], lanes[i]], first) for i in range(sl.start, sl.stop)]
            y_st = each(lambda r_, x_, s_: _dot(r_ + x_[:, :LANES], s_) + x_[:, LANES:], rt_f, ry[sl], st)
            gs = each(lambda g_, s_: _dot(g_[:, :LANES], s_), gh[sl], st)
            pc_row = pc_s[rd, c]
            for p in range(PAIRS):
                i = j * PAIRS + p
                y_ref[rows[i], lanes[i]] = y_st[p][:CHUNK] + y_st[p][CHUNK:]
                pc_col = jnp.sum(mask_s[eye] * pc_row[0:1, lanes[i]], axis=1, keepdims=True)
                st[p] = pc_col * st[p] + gs[p] + gh[i][:, LANES:]
        for p in range(PAIRS):
            st_s[p] = st[p]

    def chunk_body(ci, carry):
        first_chunk = ci * SOLVE_CHUNKS
        stages = itertools.chain(*[prepare(first_chunk + j) for j in range(SOLVE_CHUNKS)])

        def tick():
            for _ in range(SOLVE_CHUNKS):
                next(stages, None)

        solve([jnp.where(fwd, first_chunk + j, nc - 1 - first_chunk - j) for j in range(SOLVE_CHUNKS)], tick)
        for _ in stages:
            pass
        return carry

    def prepare_only(cp, carry):
        for _ in prepare(cp):
            pass
        return carry

    @pl.when(step == 0)
    def _():
        lax.fori_loop(0, nc, prepare_only, 0)

    @pl.when(step > 0)
    def _():
        lax.fori_loop(0, nc // SOLVE_CHUNKS, chunk_body, 0)


def _rwkv(z, batch, seq, w0, w2, a0, a2, k_k, k_a, r_k, ones_bd):
    n_blk = seq // RWKV_TB

    def time_block(d, i):
        return jnp.where(d == 0, i, n_blk - 1 - i)

    def tmap(d, i):
        return time_block(d, jnp.minimum(i, n_blk - 1))

    def smap(d, i):
        return time_block(d, jnp.maximum(i - 1, 0))

    vec = lambda: pl.BlockSpec((1, RWKV_WIDTH), lambda b, d, i: (0, 0))
    in_specs = [
        pl.BlockSpec((None, RWKV_TB, N_RWKV_COLS), lambda b, d, i: (b, tmap(d, i), 0)),
        pl.BlockSpec((None, 1, RWKV_WIDTH), lambda b, d, i: (d, 0, 0)),
        pl.BlockSpec((None, 128, RWKV_WIDTH), lambda b, d, i: (d, 0, 0)),
        pl.BlockSpec((None, 1, RWKV_WIDTH), lambda b, d, i: (d, 0, 0)),
        pl.BlockSpec((None, 128, RWKV_WIDTH), lambda b, d, i: (d, 0, 0)),
        vec(), vec(), vec(),
        pl.BlockSpec((2 * LANES, LANES), lambda b, d, i: (0, 0)),
    ]
    out_spec = lambda m: pl.BlockSpec((None, None, RWKV_TB, RWKV_WIDTH), lambda b, d, i: (d, b, m(d, i), 0))
    tok = lambda dt: pltpu.VMEM((2, RWKV_TB, RWKV_WIDTH), dt)
    return pl.pallas_call(
        _rwkv_kernel,
        out_shape=[jax.ShapeDtypeStruct((2, batch, seq, RWKV_WIDTH), F32)] * 2,
        grid=(batch, 2, n_blk + 1),
        in_specs=in_specs,
        out_specs=[out_spec(smap), out_spec(tmap)],
        scratch_shapes=[tok(BF16), tok(BF16), tok(BF16), tok(F32), tok(BF16), tok(BF16), tok(BF16),
                        pltpu.VMEM((2, RWKV_TB // CHUNK, 8, RWKV_WIDTH), F32),
                        pltpu.VMEM((PAIRS, LANES, LANES), F32),
                        pltpu.VMEM((6, 2 * CHUNK, 2 * CHUNK), F32)],
        compiler_params=_cparams(("parallel", "parallel", "arbitrary")),
        name="rwkv",
    )(z, w0, w2, a0, a2, k_k, k_a, r_k, ones_bd)


MIX_TM = 256


def _mixout_kernel(x_ref, o1, o2, o3, l1, l2, l3, y_ref, bonus_ref, gd_ref, g2_ref, lnw_ref, lnb_ref,
                   avg_ref, spread_ref, wo_ref, out_ref, o_scr):
    n_tiles = ATT_WIDTH // LANES
    o_refs = (o1, o2, o3)
    for k, d in enumerate(DILATIONS):
        for r in range(d if d > 1 else 0):
            for c in range(n_tiles):
                col = r * ATT_WIDTH + c * LANES
                o_scr[k, c, pl.ds(r, MIX_TM // d, stride=d), :] = o_refs[k][:, col:col + LANES].astype(F32)
    m = jnp.maximum(jnp.maximum(l1[...], l2[...]), l3[...])
    e = [jnp.exp(l[...] - m) for l in (l1, l2, l3)]
    inv = 1.0 / (e[0] + e[1] + e[2])
    spread2 = jnp.concatenate([spread_ref[...]] * 2, axis=0)
    att = jnp.zeros((MIX_TM, ATT_WIDTH), F32)
    for k, (d, ep) in enumerate(zip(DILATIONS, e)):
        wts = jnp.dot(jnp.concatenate(_split2(ep * inv), axis=1), spread2, preferred_element_type=F32)
        o_tok = (o_refs[k][...].astype(F32) if d == 1
                 else jnp.concatenate([o_scr[k, c] for c in range(n_tiles)], axis=1))
        att = att + wts * o_tok

    y = y_ref[0] + y_ref[1]
    avg = avg_ref[...]
    yc = y - _head_sums(y, avg)
    var = _head_sums(yc * yc, avg)
    yn = yc * lax.rsqrt(var + RWKV_LN_EPS) * lnw_ref[...] + lnb_ref[...]
    gate = _dot(jax.nn.sigmoid(gd_ref[...]), g2_ref[...])
    rw = (yn + bonus_ref[0] + bonus_ref[1]) * gate

    mixed = jnp.dot(att.astype(BF16), wo_ref[0:ATT_WIDTH, :], preferred_element_type=F32)
    mixed = mixed + jnp.dot(rw.astype(BF16), wo_ref[ATT_WIDTH:, :], preferred_element_type=F32)
    out_ref[...] = x_ref[...] + mixed


def _mixout(x1, os, lses, y, bonus, z, batch, seq, g2, ln_w, ln_b, avg_bd, spread, w_out):
    gcol = GD_COL // 128
    tok = lambda w: pl.BlockSpec((None, MIX_TM, w), lambda b, i: (b, i, 0))
    const = lambda shape: pl.BlockSpec(shape, lambda b, i: (0,) * len(shape))
    dir_spec = lambda: pl.BlockSpec((2, None, MIX_TM, RWKV_WIDTH), lambda b, i: (0, b, i, 0))
    view = lambda d: pl.BlockSpec((None, MIX_TM // d, d * ATT_WIDTH), lambda b, i: (b, i, 0))
    in_specs = [tok(D_MODEL)] + [view(d) for d in DILATIONS] + [tok(LANES)] * 3 + [dir_spec(), dir_spec()] + [
        pl.BlockSpec((None, MIX_TM, 128), lambda b, i: (b, i, gcol)),
        const((128, RWKV_WIDTH)),
        const((1, RWKV_WIDTH)), const((1, RWKV_WIDTH)), const((2 * LANES, LANES)),
        const((LANES, ATT_WIDTH)), const((D_MODEL, D_MODEL)),
    ]
    return pl.pallas_call(
        _mixout_kernel,
        out_shape=jax.ShapeDtypeStruct((batch, seq, D_MODEL), F32),
        grid=(batch, seq // MIX_TM),
        in_specs=in_specs,
        out_specs=tok(D_MODEL),
        scratch_shapes=[pltpu.VMEM((len(DILATIONS), ATT_WIDTH // LANES, MIX_TM, LANES), F32)],
        compiler_params=_cparams(("parallel", "arbitrary")),
        name="mixout",
    )(x1, *os, *lses, y, bonus, z, g2, ln_w, ln_b, avg_bd, spread, w_out)


def _block_diag(width, block, value):
    idx = jnp.arange(width) // block
    return jnp.where(idx[:, None] == idx[None, :], value, 0.0).astype(BF16)


def _trunk(x, p):
    batch, seq, _ = x.shape
    xf = x.reshape(batch * seq, D_MODEL)
    x1 = _ffn(xf, p["ffn1_norm"], p["ffn1_gate"], p["ffn1_up"], p["ffn1_down"], p["final_norm"],
              final_norm=False)
    views = _inproj_views(x1, p["mix_norm"], p["w_in_att"])
    z = _inproj_shift(x1, p["mix_norm"], p["w_in_rwkv"], p["mu_prev"], p["mu_next"], seq,
                      tn=1152).reshape(batch, seq, N_RWKV_COLS)
    os, lses = [], []
    for dil, view in zip(DILATIONS, views):
        o, lse = _attention(view.reshape(3, batch, seq // dil, dil * ATT_WIDTH), p["slopes"], batch, seq, dil)
        os.append(o)
        lses.append(lse)
    y, bonus = _rwkv(z, batch, seq, p["w0"], p["w2"], p["a0"], p["a2"],
                     p["k_k"], p["k_a"], p["r_k"], p["ones_bd"])
    x2 = _mixout(x1.reshape(batch, seq, D_MODEL), os, lses, y, bonus, z, batch, seq,
                 p["g2"], p["ln_x_w"], p["ln_x_b"], p["avg_bd"], p["spread"], p["w_out"])
    out = _ffn(x2.reshape(batch * seq, D_MODEL), p["ffn2_norm"], p["ffn2_gate"], p["ffn2_up"],
               p["ffn2_down"], p["final_norm"], final_norm=True)
    return out.reshape(batch, seq, D_MODEL)


def kernel(x_prompt, x_sample, ffn1_norm, ffn1_gate, ffn1_up, ffn1_down, mix_norm, w_in, w_out, mu_prev, mu_next, w0_f, w2_f, w0_b, w2_b, a0_f, a2_f, a0_b, a2_b, g2, k_k, k_a, r_k, ln_x_w, ln_x_b, ffn2_norm, ffn2_gate, ffn2_up, ffn2_down, final_norm):
    assert ffn1_norm.shape[0] == 1, "single layer"
    row = lambda t: t.reshape(1, -1)
    zero64 = jnp.zeros((64, RWKV_WIDTH), F32)
    w_in_b = w_in[0].astype(BF16)
    p = {
        "ffn1_norm": row(ffn1_norm[0]), "ffn1_gate": ffn1_gate[0].astype(BF16),
        "ffn1_up": ffn1_up[0].astype(BF16), "ffn1_down": ffn1_down[0].astype(BF16),
        "mix_norm": row(mix_norm[0]),
        "w_in_att": w_in_b[:, :N_ATT_COLS], "w_in_rwkv": w_in_b[:, N_ATT_COLS:],
        "w_out": w_out[0].astype(BF16),
        "mu_prev": row(mu_prev[0]), "mu_next": row(mu_next[0]),
        "w0": jnp.stack([row(w0_f[0]), row(w0_b[0])]),
        "w2": jnp.stack([jnp.concatenate([w2_f[0], zero64]), jnp.concatenate([zero64, w2_b[0]])]).astype(BF16),
        "a0": jnp.stack([row(a0_f[0]), row(a0_b[0])]),
        "a2": jnp.stack([jnp.concatenate([a2_f[0], zero64]), jnp.concatenate([zero64, a2_b[0]])]).astype(BF16),
        "g2": g2[0].astype(BF16),
        "k_k": row(k_k[0]), "k_a": row(k_a[0]), "r_k": row(r_k[0]),
        "ln_x_w": row(ln_x_w[0]), "ln_x_b": row(ln_x_b[0]),
        "ffn2_norm": row(ffn2_norm[0]), "ffn2_gate": ffn2_gate[0].astype(BF16),
        "ffn2_up": ffn2_up[0].astype(BF16), "ffn2_down": ffn2_down[0].astype(BF16),
        "final_norm": row(final_norm),
        "slopes": jnp.exp2(-8.0 * jnp.arange(1, ATT_HEADS + 1, dtype=F32) / ATT_HEADS),
        "ones_bd": jnp.concatenate([_block_diag(LANES, HEAD_DIM, 1.0)] * 2),
        "avg_bd": jnp.concatenate([_block_diag(LANES, HEAD_DIM, 1.0 / HEAD_DIM)] * 2),
        "spread": (jnp.arange(LANES)[:, None] == jnp.arange(ATT_WIDTH)[None, :] // HEAD_DIM).astype(BF16),
    }
    return (_trunk(x_prompt, p), _trunk(x_sample, p))
```

```python
import functools
import itertools

import jax
import jax.numpy as jnp
from jax import lax
from jax.experimental import pallas as pl
from jax.experimental.pallas import tpu as pltpu

F32 = jnp.float32
BF16 = jnp.bfloat16

D_MODEL = 2048
D_FF = 5632
HEAD_DIM = 64
ATT_WIDTH = 1024
ATT_HEADS = 16
RWKV_WIDTH = 1024
N_ATT_COLS = 3 * ATT_WIDTH
N_RWKV_COLS = 3 * RWKV_WIDTH + 64 * 4 + 128
WD_COL = 3 * RWKV_WIDTH
AD_COL = WD_COL + 128
GD_COL = AD_COL + 128
DILATIONS = (1, 4, 16)
BAND_HALF = 64
NORM_EPS = 1e-6
RWKV_LN_EPS = 64e-5
NEG_INF = -1e30

LANES = 128
CHUNK = 64
VMEM_LIMIT = 56 * 1024 * 1024


def _cparams(sem):
    return pltpu.CompilerParams(dimension_semantics=sem, vmem_limit_bytes=VMEM_LIMIT)


def _dot(a, b):
    return jnp.dot(a.astype(BF16), b.astype(BF16), preferred_element_type=F32)


def _dot_nt(a, b):
    return lax.dot_general(a.astype(BF16), b.astype(BF16), (((1,), (1,)), ((), ())),
                           preferred_element_type=F32)


def _dot_tn(a, b):
    return lax.dot_general(a.astype(BF16), b.astype(BF16), (((0,), (0,)), ((), ())),
                           preferred_element_type=F32)


def _split2(x):
    hi = x.astype(BF16)
    lo = (x - hi.astype(F32)).astype(BF16)
    return hi, lo


def _head_sums(x, seg2):
    hi, lo = _split2(x)
    outs = []
    for g in range(x.shape[1] // LANES):
        cols = slice(g * LANES, (g + 1) * LANES)
        outs.append(jnp.dot(jnp.concatenate([hi[:, cols], lo[:, cols]], axis=1), seg2,
                            preferred_element_type=F32))
    return jnp.concatenate(outs, axis=1)


def _cumsum_dot(tri_bf16, x):
    hi, lo = _split2(x)
    return (jnp.dot(tri_bf16, hi, preferred_element_type=F32)
            + jnp.dot(tri_bf16, lo, preferred_element_type=F32))


def _rmsnorm(x, g):
    return x * lax.rsqrt(jnp.mean(x * x, axis=-1, keepdims=True) + NORM_EPS) * g


def _ffn_kernel(x_ref, g_ref, wg_ref, wu_ref, wd_ref, fg_ref, o_ref, n_scr, *, final_norm):
    f = pl.program_id(1)

    @pl.when(f == 0)
    def _():
        x = x_ref[...]
        n_scr[...] = _rmsnorm(x, g_ref[...]).astype(BF16)
        o_ref[...] = x

    n = n_scr[...]
    hg = jnp.dot(n, wg_ref[...], preferred_element_type=F32)
    hu = jnp.dot(n, wu_ref[...], preferred_element_type=F32)
    h = (hg * jax.nn.sigmoid(hg) * (0.5 * hu)).astype(BF16)
    o_ref[...] += jnp.dot(h, wd_ref[...], preferred_element_type=F32)

    if final_norm:
        @pl.when(f == pl.num_programs(1) - 1)
        def _():
            o_ref[...] = _rmsnorm(o_ref[...], fg_ref[...])


def _ffn(x, g, wg, wu, wd, fg, *, final_norm, tm=1024, tf=512):
    n_tok = x.shape[0]
    return pl.pallas_call(
        functools.partial(_ffn_kernel, final_norm=final_norm),
        out_shape=jax.ShapeDtypeStruct((n_tok, D_MODEL), F32),
        grid=(n_tok // tm, D_FF // tf),
        in_specs=[
            pl.BlockSpec((tm, D_MODEL), lambda i, f: (i, 0)),
            pl.BlockSpec((1, D_MODEL), lambda i, f: (0, 0)),
            pl.BlockSpec((D_MODEL, tf), lambda i, f: (0, f)),
            pl.BlockSpec((D_MODEL, tf), lambda i, f: (0, f)),
            pl.BlockSpec((tf, D_MODEL), lambda i, f: (f, 0)),
            pl.BlockSpec((1, D_MODEL), lambda i, f: (0, 0)),
        ],
        out_specs=pl.BlockSpec((tm, D_MODEL), lambda i, f: (i, 0)),
        scratch_shapes=[pltpu.VMEM((tm, D_MODEL), BF16)],
        compiler_params=_cparams(("parallel", "arbitrary")),
        name="ffn_final" if final_norm else "ffn",
    )(x, g, wg, wu, wd, fg)


def _inproj_shift_kernel(x_ref, xp_ref, xn_ref, g_ref, w_ref, mup_ref, mun_ref, o_ref, n_scr, nh_scr, *, seq):
    i = pl.program_id(0)
    tm = x_ref.shape[0]

    @pl.when(pl.program_id(1) == 0)
    def _():
        g = g_ref[...]
        n_scr[...] = _rmsnorm(x_ref[...], g).astype(BF16)
        halo = jnp.concatenate([xp_ref[...], xn_ref[...]], axis=0)
        nh_scr[...] = _rmsnorm(halo, g).astype(BF16)

    w = w_ref[...]
    z = jnp.dot(n_scr[...], w, preferred_element_type=F32)
    z_halo = jnp.dot(nh_scr[...], w, preferred_element_type=F32)
    prev_row = jnp.where((i * tm) % seq == 0, 0.0, z_halo[7:8, :])
    next_row = jnp.where(((i + 1) * tm) % seq == 0, 0.0, z_halo[8:9, :])
    o_ref[...] = _shifted(z, prev_row, next_row, mup_ref[...], mun_ref[...])


def _inproj_shift(x, g, w, mu_prev, mu_next, seq, *, tn, tm=1024):
    n_tok = x.shape[0]
    n_cols = w.shape[1]
    hb = tm // 8
    return pl.pallas_call(
        functools.partial(_inproj_shift_kernel, seq=seq),
        out_shape=jax.ShapeDtypeStruct((n_tok, n_cols), F32),
        grid=(n_tok // tm, n_cols // tn),
        in_specs=[
            pl.BlockSpec((tm, D_MODEL), lambda i, j: (i, 0)),
            pl.BlockSpec((8, D_MODEL), lambda i, j: (jnp.maximum(i * hb - 1, 0), 0)),
            pl.BlockSpec((8, D_MODEL), lambda i, j: (jnp.minimum((i + 1) * hb, n_tok // 8 - 1), 0)),
            pl.BlockSpec((1, D_MODEL), lambda i, j: (0, 0)),
            pl.BlockSpec((D_MODEL, tn), lambda i, j: (0, j)),
            pl.BlockSpec((1, tn), lambda i, j: (0, j)),
            pl.BlockSpec((1, tn), lambda i, j: (0, j)),
        ],
        out_specs=pl.BlockSpec((tm, tn), lambda i, j: (i, j)),
        scratch_shapes=[pltpu.VMEM((tm, D_MODEL), BF16), pltpu.VMEM((16, D_MODEL), BF16)],
        compiler_params=_cparams(("parallel", "arbitrary")),
        name="inproj_shift",
    )(x, x, x, g, w, mu_prev, mu_next)


def _inproj_views_kernel(x_ref, g_ref, w_ref, *refs):
    out_refs, (n_scr, acc_scr) = refs[:len(DILATIONS)], refs[len(DILATIONS):]

    @pl.when(pl.program_id(1) == 0)
    def _():
        n_scr[...] = _rmsnorm(x_ref[...], g_ref[...]).astype(BF16)

    acc = jnp.dot(n_scr[...], w_ref[...], preferred_element_type=F32)
    out_refs[0][...] = acc.astype(BF16)
    n_tiles, tm, _ = acc_scr.shape
    for c in range(n_tiles):
        acc_scr[c] = acc[:, c * LANES:(c + 1) * LANES]
    for d, o_ref in zip(DILATIONS[1:], out_refs[1:]):
        for r in range(d):
            for c in range(n_tiles):
                col = r * ATT_WIDTH + c * LANES
                o_ref[:, col:col + LANES] = acc_scr[c, pl.ds(r, tm // d, stride=d), :].astype(BF16)


def _inproj_views(x, g, w, *, tm=1024):
    n_tok = x.shape[0]
    return pl.pallas_call(
        _inproj_views_kernel,
        out_shape=[jax.ShapeDtypeStruct((3, n_tok // d, d * ATT_WIDTH), BF16) for d in DILATIONS],
        grid=(n_tok // tm, 3),
        in_specs=[
            pl.BlockSpec((tm, D_MODEL), lambda i, j: (i, 0)),
            pl.BlockSpec((1, D_MODEL), lambda i, j: (0, 0)),
            pl.BlockSpec((D_MODEL, ATT_WIDTH), lambda i, j: (0, j)),
        ],
        out_specs=[pl.BlockSpec((None, tm // d, d * ATT_WIDTH), lambda i, j: (j, i, 0)) for d in DILATIONS],
        scratch_shapes=[pltpu.VMEM((tm, D_MODEL), BF16), pltpu.VMEM((ATT_WIDTH // LANES, tm, LANES), F32)],
        compiler_params=_cparams(("parallel", "arbitrary")),
        name="inproj_views",
    )(x, g, w)


ATT_TQ = 128
ATT_TK = 64
ATT_GROUP = 8


def _attn_kernel(q_ref, *refs, sub):
    nkb = sub * ATT_TQ // ATT_TK + 2
    k_refs, v_refs = refs[:nkb], refs[nkb:2 * nkb]
    bias_ref, o_ref, lse_ref, kc_scr, vc_scr = refs[2 * nkb:]
    i = pl.program_id(2)
    for j, (kr, vr) in enumerate(zip(k_refs, v_refs)):
        kc_scr[j * ATT_TK:(j + 1) * ATT_TK, :] = kr[...]
        vc_scr[j * ATT_TK:(j + 1) * ATT_TK, :] = vr[...]

    lane = lax.broadcasted_iota(jnp.int32, (ATT_TQ, LANES), 1)
    first = lane < HEAD_DIM
    heads = range(ATT_HEADS)
    cols = [slice((h // 2) * LANES, (h // 2 + 1) * LANES) for h in heads]
    sel = [first if h % 2 == 0 else jnp.logical_not(first) for h in heads]
    scale = jnp.asarray(HEAD_DIM ** -0.5, BF16)
    last_tile = sub * pl.num_programs(2) - 1

    for u in range(sub):
        tile = sub * i + u
        case = jnp.where(tile == 0, 1, jnp.where(tile == last_tile, 2, 0))
        qrows = slice(u * ATT_TQ, (u + 1) * ATT_TQ)
        krows = slice(u * ATT_TQ, u * ATT_TQ + 4 * ATT_TK)
        o, lse = [], []
        for g in range(0, ATT_HEADS, ATT_GROUP):
            grp = range(g, g + ATT_GROUP)
            qm = [jnp.where(sel[h], q_ref[qrows, cols[h]] * scale, jnp.zeros((ATT_TQ, LANES), BF16))
                  for h in grp]
            s = [_dot_nt(q_, kc_scr[krows, cols[h]]) + bias_ref[case, h] for q_, h in zip(qm, grp)]
            m = [jnp.max(x, axis=-1, keepdims=True) for x in s]
            p = [jnp.exp(x - mx) for x, mx in zip(s, m)]
            l = [jnp.sum(x, axis=-1, keepdims=True) for x in p]
            o += [jnp.dot(p_.astype(BF16), vc_scr[krows, cols[h]], preferred_element_type=F32) * (1.0 / l_)
                  for p_, l_, h in zip(p, l, grp)]
            lse += [mx + jnp.log(lx) for mx, lx in zip(m, l)]
        for h in range(0, ATT_HEADS, 2):
            o_ref[qrows, cols[h]] = jnp.where(first, o[h], o[h + 1]).astype(o_ref.dtype)
        lse_tile = jnp.zeros((ATT_TQ, LANES), F32)
        for h in heads:
            lse_tile = jnp.where(lane == h, lse[h], lse_tile)
        lse_ref[qrows, :] = lse_tile


def _attention_bias(slopes, dil):
    nk = 4 * ATT_TK
    row = jnp.arange(ATT_TQ)[:, None]
    col = jnp.arange(nk)[None, :]
    rel = col - BAND_HALF - row
    in_band = jnp.abs(rel) <= BAND_HALF
    dist = (jnp.abs(rel) * dil).astype(F32)
    inside = jnp.stack([in_band, in_band & (col >= BAND_HALF), in_band & (col < BAND_HALF + ATT_TQ)])
    return jnp.where(inside[:, None], -slopes[None, :, None, None] * dist, NEG_INF)


def _attention(view, slopes, batch, seq, dil):
    L = seq // dil
    sub = 4 if L % (4 * ATT_TQ) == 0 else 2
    step_q = sub * ATT_TQ
    assert L % step_q == 0 and L // ATT_TQ >= 2, "first and last query block must differ"
    n_kv = step_q // ATT_TK + 2
    nkb = L // ATT_TK
    ratio = step_q // ATT_TK
    bias = _attention_bias(slopes, dil)

    def kv_spec(which, j):
        def imap(b, r, i):
            blk = jnp.clip(i * ratio - 1 + j, 0, nkb - 1)
            return (which, b, blk, r)
        return pl.BlockSpec((None, None, ATT_TK, ATT_WIDTH), imap)

    o, lse = pl.pallas_call(
        functools.partial(_attn_kernel, sub=sub),
        out_shape=[jax.ShapeDtypeStruct((batch, L, dil * ATT_WIDTH), BF16),
                   jax.ShapeDtypeStruct((batch, dil, L, LANES), F32)],
        grid=(batch, dil, L // step_q),
        in_specs=[pl.BlockSpec((None, None, step_q, ATT_WIDTH), lambda b, r, i: (0, b, i, r))]
        + [kv_spec(1, j) for j in range(n_kv)] + [kv_spec(2, j) for j in range(n_kv)]
        + [pl.BlockSpec(bias.shape, lambda b, r, i: (0, 0, 0, 0), pipeline_mode=pl.Buffered(1))],
        out_specs=[pl.BlockSpec((None, step_q, ATT_WIDTH), lambda b, r, i: (b, i, r)),
                   pl.BlockSpec((None, None, step_q, LANES), lambda b, r, i: (b, r, i, 0))],
        scratch_shapes=[pltpu.VMEM((n_kv * ATT_TK, ATT_WIDTH), BF16),
                        pltpu.VMEM((n_kv * ATT_TK, ATT_WIDTH), BF16)],
        compiler_params=_cparams(("parallel", "parallel", "arbitrary")),
        name=f"attn_d{dil}",
    )(view, *([view] * (2 * n_kv)), bias)
    return o, lse.transpose(0, 2, 1, 3).reshape(batch, seq, LANES)


RWKV_TB = 256
PAIRS = RWKV_WIDTH // LANES
SOLVE_CHUNKS = 2


def _shifted(z, prev_row, next_row, mu_prev, mu_next):
    n = z.shape[0]
    ridx = lax.broadcasted_iota(jnp.int32, z.shape, 0)
    z_prev = jnp.where(ridx == 0, prev_row, pltpu.roll(z, 1, 0))
    z_next = jnp.where(ridx == n - 1, next_row, pltpu.roll(z, n - 1, 0))
    return z + mu_prev * (z_prev - z) + mu_next * (z_next - z)


def _stack_pair(x, first):
    zero = jnp.zeros_like(x)
    return jnp.concatenate([jnp.where(first, x, zero), jnp.where(first, zero, x)], axis=0)


def _rwkv_kernel(z_ref, w0_ref, w2_ref, a0_ref, a2_ref, kk_ref, ka_ref, rk_ref, ones_ref, y_ref, bonus_ref,
                 at_s, bt_s, kt_s, rt_s, bd_s, kd_s, v_s, pc_s, st_s, mask_s):
    d = pl.program_id(1)
    step = pl.program_id(2)
    nc = RWKV_TB // CHUNK
    fwd = d == 0
    sgn = jnp.where(fwd, 1, -1)
    wr = step % 2
    rd = 1 - wr
    n2 = 2 * CHUNK

    @pl.when(step == 0)
    def _():
        st_s[...] = jnp.zeros_like(st_s)
        rr = lax.broadcasted_iota(jnp.int32, (n2, n2), 0)
        cc = lax.broadcasted_iota(jnp.int32, (n2, n2), 1)
        same_head = (rr // CHUNK) == (cc // CHUNK)
        tt = rr % CHUNK
        ss = cc % CHUNK
        strict = same_head & ((tt - ss) * sgn > 0)
        masks = (strict,
                 same_head & ((tt - ss) * sgn >= 0),
                 strict & ((tt // 16) == (ss // 16)),
                 strict & ((tt // 32) == (ss // 32)) & ((tt // 16) != (ss // 16)),
                 strict & ((tt // 32) != (ss // 32)),
                 rr == cc)
        for idx, msk in enumerate(masks):
            mask_s[idx] = jnp.where(msk, 1.0, 0.0)

    ti = lax.broadcasted_iota(jnp.int32, (CHUNK, CHUNK), 0)
    si = lax.broadcasted_iota(jnp.int32, (CHUNK, CHUNK), 1)
    tri = jnp.where((ti - si) * sgn >= 0, 1.0, 0.0).astype(BF16)

    def prepare(cp):
        rows = pl.ds(pl.multiple_of(cp * CHUNK, CHUNK), CHUNK)
        r = z_ref[rows, 0:RWKV_WIDTH]
        k = z_ref[rows, RWKV_WIDTH:2 * RWKV_WIDTH]
        v = z_ref[rows, 2 * RWKV_WIDTH:3 * RWKV_WIDTH]
        v_s[wr, rows, :] = v.astype(BF16)
        yield
        wd = z_ref[rows, WD_COL:WD_COL + 128]
        ad = z_ref[rows, AD_COL:AD_COL + 128]
        u = -(w0_ref[...] + _dot(jnp.tanh(wd), w2_ref[...]))
        softplus = jnp.maximum(u, 0.0) + jnp.log(1.0 + jnp.exp(-jnp.abs(u)))
        lw = -jnp.exp(-softplus - 0.5)
        yield
        iclr = jax.nn.sigmoid(a0_ref[...] + _dot(ad, a2_ref[...]))
        ones = ones_ref[...]
        kkr = k * kk_ref[...]
        kk = kkr * lax.rsqrt(jnp.maximum(_head_sums(kkr * kkr, ones), 1e-24))
        yield
        kdir = k * (1.0 + (iclr - 1.0) * ka_ref[...])
        bonus_ref[rows, :] = _head_sums(r * kdir * rk_ref[...], ones) * v
        yield
        b = kk * iclr
        lp_incl = _cumsum_dot(tri, lw)
        total = jnp.sum(lw, axis=0, keepdims=True)
        pc_s[wr, cp] = jnp.broadcast_to(jnp.exp(total), (8, RWKV_WIDTH))
        yield
        e_neg = jnp.exp(-lp_incl)
        at_s[wr, rows, :] = (-kk * jnp.exp(lp_incl - lw)).astype(BF16)
        bt_s[wr, rows, :] = (b * e_neg).astype(BF16)
        yield
        kt_s[wr, rows, :] = (kdir * e_neg).astype(BF16)
        rt_s[wr, rows, :] = r * jnp.exp(lp_incl)
        yield
        e_rest = jnp.exp(total - lp_incl)
        bd_s[wr, rows, :] = (b * e_rest).astype(BF16)
        kd_s[wr, rows, :] = (kdir * e_rest).astype(BF16)

    strict, incl, blk16, off32, off64, eye = range(6)
    lane = lax.broadcasted_iota(jnp.int32, (CHUNK, LANES), 1)
    first = lane < HEAD_DIM

    def solve(chunks, tick):
        items = [(c, p) for c in chunks for p in range(PAIRS)]
        rows = [pl.ds(pl.multiple_of(c * CHUNK, CHUNK), CHUNK) for c, _ in items]
        lanes = [slice(p * LANES, (p + 1) * LANES) for _, p in items]
        each = lambda fn, *lists: [fn(*xs) for xs in zip(*lists)]
        load = lambda ref: [_stack_pair(ref[rd, r_, l_], first) for r_, l_ in zip(rows, lanes)]
        at, bt, kt, bd, kd, vv = (load(ref) for ref in (at_s, bt_s, kt_s, bd_s, kd_s, v_s))
        rt = [_stack_pair(rt_s[rd, r_, l_].astype(BF16), first) for r_, l_ in zip(rows, lanes)]
        tick()

        prod = each(lambda a_, r_, b_, k_: _dot_nt(jnp.concatenate([a_, r_], axis=0),
                                                   jnp.concatenate([b_, k_], axis=0)), at, rt, bt, kt)
        a_ak = [(x[:n2, n2:] * mask_s[strict]).astype(BF16) for x in prod]
        q_bk = [jnp.concatenate([x[n2:, :n2] * mask_s[incl], x[n2:, n2:] * mask_s[incl]], axis=1).astype(BF16)
                for x in prod]
        offs = [[(x[:n2, :n2] * mask_s[m]).astype(BF16) for x in prod] for m in (off32, off64)]
        tick()

        d0 = [x[:n2, :n2] * mask_s[blk16] for x in prod]
        t_inv = [mask_s[eye] + x for x in d0]
        pw = [_dot(x, x).astype(BF16) for x in d0]
        tick()
        for _ in range(2):
            both = each(lambda t_, p_: _dot(jnp.concatenate([t_.astype(BF16), p_], axis=0), p_), t_inv, pw)
            t_inv = each(lambda t_, b_: t_ + b_[:n2], t_inv, both)
            pw = [x[n2:].astype(BF16) for x in both]
            tick()
        t_inv = each(lambda t_, p_: t_ + _dot(t_, p_), t_inv, pw)
        tick()
        for off in offs:
            inner = each(lambda o_, t_: _dot(o_, t_).astype(BF16), off, t_inv)
            t_inv = each(lambda t_, i_: t_ + _dot(t_, i_), t_inv, inner)
            tick()

        av = each(lambda a_, v_: _dot(a_, v_).astype(BF16), a_ak, vv)
        tick()
        wu = each(lambda t_, a_, v_: _dot(t_, jnp.concatenate([a_, v_], axis=1)).astype(BF16),
                  t_inv, at, av)
        tick()
        zero = jnp.zeros((n2, LANES), BF16)
        rhs = each(lambda w_, v_: jnp.concatenate(
            [w_, jnp.concatenate([zero, v_], axis=1)], axis=0), wu, vv)
        ry = each(_dot, q_bk, rhs)
        gh = each(lambda b_, k_, x_: _dot_tn(jnp.concatenate([b_, k_], axis=0), x_),
                  bd, kd, rhs)
        tick()

        st = [st_s[p] for p in range(PAIRS)]
        for j, c in enumerate(chunks):
            sl = slice(j * PAIRS, (j + 1) * PAIRS)
            rt_f = [_stack_pair(rt_s[rd, rows[i], lanes[i]], first) for i in range(sl.start, sl.stop)]
            y_st = each(lambda r_, x_, s_: _dot(r_ + x_[:, :LANES], s_) + x_[:, LANES:], rt_f, ry[sl], st)
            gs = each(lambda g_, s_: _dot(g_[:, :LANES], s_), gh[sl], st)
            pc_row = pc_s[rd, c]
            for p in range(PAIRS):
                i = j * PAIRS + p
                y_ref[rows[i], lanes[i]] = y_st[p][:CHUNK] + y_st[p][CHUNK:]
                pc_col = jnp.sum(mask_s[eye] * pc_row[0:1, lanes[i]], axis=1, keepdims=True)
                st[p] = pc_col * st[p] + gs[p] + gh[i][:, LANES:]
        for p in range(PAIRS):
            st_s[p] = st[p]

    def chunk_body(ci, carry):
        first_chunk = ci * SOLVE_CHUNKS
        stages = itertools.chain(*[prepare(first_chunk + j) for j in range(SOLVE_CHUNKS)])

        def tick():
            for _ in range(SOLVE_CHUNKS):
                next(stages, None)

        solve([jnp.where(fwd, first_chunk + j, nc - 1 - first_chunk - j) for j in range(SOLVE_CHUNKS)], tick)
        for _ in stages:
            pass
        return carry

    def prepare_only(cp, carry):
        for _ in prepare(cp):
            pass
        return carry

    @pl.when(step == 0)
    def _():
        lax.fori_loop(0, nc, prepare_only, 0)

    @pl.when(step > 0)
    def _():
        lax.fori_loop(0, nc // SOLVE_CHUNKS, chunk_body, 0)


def _rwkv(z, batch, seq, w0, w2, a0, a2, k_k, k_a, r_k, ones_bd):
    n_blk = seq // RWKV_TB

    def time_block(d, i):
        return jnp.where(d == 0, i, n_blk - 1 - i)

    def tmap(d, i):
        return time_block(d, jnp.minimum(i, n_blk - 1))

    def smap(d, i):
        return time_block(d, jnp.maximum(i - 1, 0))

    vec = lambda: pl.BlockSpec((1, RWKV_WIDTH), lambda b, d, i: (0, 0))
    in_specs = [
        pl.BlockSpec((None, RWKV_TB, N_RWKV_COLS), lambda b, d, i: (b, tmap(d, i), 0)),
        pl.BlockSpec((None, 1, RWKV_WIDTH), lambda b, d, i: (d, 0, 0)),
        pl.BlockSpec((None, 128, RWKV_WIDTH), lambda b, d, i: (d, 0, 0)),
        pl.BlockSpec((None, 1, RWKV_WIDTH), lambda b, d, i: (d, 0, 0)),
        pl.BlockSpec((None, 128, RWKV_WIDTH), lambda b, d, i: (d, 0, 0)),
        vec(), vec(), vec(),
        pl.BlockSpec((2 * LANES, LANES), lambda b, d, i: (0, 0)),
    ]
    out_spec = lambda m: pl.BlockSpec((None, None, RWKV_TB, RWKV_WIDTH), lambda b, d, i: (d, b, m(d, i), 0))
    tok = lambda dt: pltpu.VMEM((2, RWKV_TB, RWKV_WIDTH), dt)
    return pl.pallas_call(
        _rwkv_kernel,
        out_shape=[jax.ShapeDtypeStruct((2, batch, seq, RWKV_WIDTH), F32)] * 2,
        grid=(batch, 2, n_blk + 1),
        in_specs=in_specs,
        out_specs=[out_spec(smap), out_spec(tmap)],
        scratch_shapes=[tok(BF16), tok(BF16), tok(BF16), tok(F32), tok(BF16), tok(BF16), tok(BF16),
                        pltpu.VMEM((2, RWKV_TB // CHUNK, 8, RWKV_WIDTH), F32),
                        pltpu.VMEM((PAIRS, LANES, LANES), F32),
                        pltpu.VMEM((6, 2 * CHUNK, 2 * CHUNK), F32)],
        compiler_params=_cparams(("parallel", "parallel", "arbitrary")),
        name="rwkv",
    )(z, w0, w2, a0, a2, k_k, k_a, r_k, ones_bd)


MIX_TM = 256


def _mixout_kernel(x_ref, o1, o2, o3, l1, l2, l3, y_ref, bonus_ref, gd_ref, g2_ref, lnw_ref, lnb_ref,
                   avg_ref, spread_ref, wo_ref, out_ref, o_scr):
    n_tiles = ATT_WIDTH // LANES
    o_refs = (o1, o2, o3)
    for k, d in enumerate(DILATIONS):
        for r in range(d if d > 1 else 0):
            for c in range(n_tiles):
                col = r * ATT_WIDTH + c * LANES
                o_scr[k, c, pl.ds(r, MIX_TM // d, stride=d), :] = o_refs[k][:, col:col + LANES].astype(F32)
    m = jnp.maximum(jnp.maximum(l1[...], l2[...]), l3[...])
    e = [jnp.exp(l[...] - m) for l in (l1, l2, l3)]
    inv = 1.0 / (e[0] + e[1] + e[2])
    spread2 = jnp.concatenate([spread_ref[...]] * 2, axis=0)
    att = jnp.zeros((MIX_TM, ATT_WIDTH), F32)
    for k, (d, ep) in enumerate(zip(DILATIONS, e)):
        wts = jnp.dot(jnp.concatenate(_split2(ep * inv), axis=1), spread2, preferred_element_type=F32)
        o_tok = (o_refs[k][...].astype(F32) if d == 1
                 else jnp.concatenate([o_scr[k, c] for c in range(n_tiles)], axis=1))
        att = att + wts * o_tok

    y = y_ref[0] + y_ref[1]
    avg = avg_ref[...]
    yc = y - _head_sums(y, avg)
    var = _head_sums(yc * yc, avg)
    yn = yc * lax.rsqrt(var + RWKV_LN_EPS) * lnw_ref[...] + lnb_ref[...]
    gate = _dot(jax.nn.sigmoid(gd_ref[...]), g2_ref[...])
    rw = (yn + bonus_ref[0] + bonus_ref[1]) * gate

    mixed = jnp.dot(att.astype(BF16), wo_ref[0:ATT_WIDTH, :], preferred_element_type=F32)
    mixed = mixed + jnp.dot(rw.astype(BF16), wo_ref[ATT_WIDTH:, :], preferred_element_type=F32)
    out_ref[...] = x_ref[...] + mixed


def _mixout(x1, os, lses, y, bonus, z, batch, seq, g2, ln_w, ln_b, avg_bd, spread, w_out):
    gcol = GD_COL // 128
    tok = lambda w: pl.BlockSpec((None, MIX_TM, w), lambda b, i: (b, i, 0))
    const = lambda shape: pl.BlockSpec(shape, lambda b, i: (0,) * len(shape))
    dir_spec = lambda: pl.BlockSpec((2, None, MIX_TM, RWKV_WIDTH), lambda b, i: (0, b, i, 0))
    view = lambda d: pl.BlockSpec((None, MIX_TM // d, d * ATT_WIDTH), lambda b, i: (b, i, 0))
    in_specs = [tok(D_MODEL)] + [view(d) for d in DILATIONS] + [tok(LANES)] * 3 + [dir_spec(), dir_spec()] + [
        pl.BlockSpec((None, MIX_TM, 128), lambda b, i: (b, i, gcol)),
        const((128, RWKV_WIDTH)),
        const((1, RWKV_WIDTH)), const((1, RWKV_WIDTH)), const((2 * LANES, LANES)),
        const((LANES, ATT_WIDTH)),
        pl.BlockSpec((D_MODEL, D_MODEL), lambda b, i: (0, 0), pipeline_mode=pl.Buffered(1)),
    ]
    return pl.pallas_call(
        _mixout_kernel,
        out_shape=jax.ShapeDtypeStruct((batch, seq, D_MODEL), F32),
        grid=(batch, seq // MIX_TM),
        in_specs=in_specs,
        out_specs=tok(D_MODEL),
        scratch_shapes=[pltpu.VMEM((len(DILATIONS), ATT_WIDTH // LANES, MIX_TM, LANES), F32)],
        compiler_params=_cparams(("parallel", "arbitrary")),
        name="mixout",
    )(x1, *os, *lses, y, bonus, z, g2, ln_w, ln_b, avg_bd, spread, w_out)


def _block_diag(width, block, value):
    idx = jnp.arange(width) // block
    return jnp.where(idx[:, None] == idx[None, :], value, 0.0).astype(BF16)


def _trunk(x, p):
    batch, seq, _ = x.shape
    xf = x.reshape(batch * seq, D_MODEL)
    x1 = _ffn(xf, p["ffn1_norm"], p["ffn1_gate"], p["ffn1_up"], p["ffn1_down"], p["final_norm"],
              final_norm=False)
    views = _inproj_views(x1, p["mix_norm"], p["w_in_att"])
    z = _inproj_shift(x1, p["mix_norm"], p["w_in_rwkv"], p["mu_prev"], p["mu_next"], seq,
                      tn=1152).reshape(batch, seq, N_RWKV_COLS)
    os, lses = [], []
    for dil, view in zip(DILATIONS, views):
        o, lse = _attention(view.reshape(3, batch, seq // dil, dil * ATT_WIDTH), p["slopes"], batch, seq, dil)
        os.append(o)
        lses.append(lse)
    y, bonus = _rwkv(z, batch, seq, p["w0"], p["w2"], p["a0"], p["a2"],
                     p["k_k"], p["k_a"], p["r_k"], p["ones_bd"])
    x2 = _mixout(x1.reshape(batch, seq, D_MODEL), os, lses, y, bonus, z, batch, seq,
                 p["g2"], p["ln_x_w"], p["ln_x_b"], p["avg_bd"], p["spread"], p["w_out"])
    out = _ffn(x2.reshape(batch * seq, D_MODEL), p["ffn2_norm"], p["ffn2_gate"], p["ffn2_up"],
               p["ffn2_down"], p["final_norm"], final_norm=True)
    return out.reshape(batch, seq, D_MODEL)


def kernel(x_prompt, x_sample, ffn1_norm, ffn1_gate, ffn1_up, ffn1_down, mix_norm, w_in, w_out, mu_prev, mu_next, w0_f, w2_f, w0_b, w2_b, a0_f, a2_f, a0_b, a2_b, g2, k_k, k_a, r_k, ln_x_w, ln_x_b, ffn2_norm, ffn2_gate, ffn2_up, ffn2_down, final_norm):
    assert ffn1_norm.shape[0] == 1, "single layer"
    row = lambda t: t.reshape(1, -1)
    zero64 = jnp.zeros((64, RWKV_WIDTH), F32)
    w_in_b = w_in[0].astype(BF16)
    p = {
        "ffn1_norm": row(ffn1_norm[0]), "ffn1_gate": ffn1_gate[0].astype(BF16),
        "ffn1_up": ffn1_up[0].astype(BF16), "ffn1_down": ffn1_down[0].astype(BF16),
        "mix_norm": row(mix_norm[0]),
        "w_in_att": w_in_b[:, :N_ATT_COLS], "w_in_rwkv": w_in_b[:, N_ATT_COLS:],
        "w_out": w_out[0].astype(BF16),
        "mu_prev": row(mu_prev[0]), "mu_next": row(mu_next[0]),
        "w0": jnp.stack([row(w0_f[0]), row(w0_b[0])]),
        "w2": jnp.stack([jnp.concatenate([w2_f[0], zero64]), jnp.concatenate([zero64, w2_b[0]])]).astype(BF16),
        "a0": jnp.stack([row(a0_f[0]), row(a0_b[0])]),
        "a2": jnp.stack([jnp.concatenate([a2_f[0], zero64]), jnp.concatenate([zero64, a2_b[0]])]).astype(BF16),
        "g2": g2[0].astype(BF16),
        "k_k": row(k_k[0]), "k_a": row(k_a[0]), "r_k": row(r_k[0]),
        "ln_x_w": row(ln_x_w[0]), "ln_x_b": row(ln_x_b[0]),
        "ffn2_norm": row(ffn2_norm[0]), "ffn2_gate": ffn2_gate[0].astype(BF16),
        "ffn2_up": ffn2_up[0].astype(BF16), "ffn2_down": ffn2_down[0].astype(BF16),
        "final_norm": row(final_norm),
        "slopes": jnp.exp2(-8.0 * jnp.arange(1, ATT_HEADS + 1, dtype=F32) / ATT_HEADS),
        "ones_bd": jnp.concatenate([_block_diag(LANES, HEAD_DIM, 1.0)] * 2),
        "avg_bd": jnp.concatenate([_block_diag(LANES, HEAD_DIM, 1.0 / HEAD_DIM)] * 2),
        "spread": (jnp.arange(LANES)[:, None] == jnp.arange(ATT_WIDTH)[None, :] // HEAD_DIM).astype(BF16),
    }
    return (_trunk(x_prompt, p), _trunk(x_sample, p))
```
